```python
import jax, jax.numpy as jnp
from jax import lax
import numpy as np

D_MODEL = 1024
BATCH = 8
SEQ = 2048
DEPTH = 2

CONV_WIDTH = 3
CONV_CHANNELS = D_MODEL // 2
HGRN_HEAD_DIM = 128
HGRN_HEADS = (D_MODEL // 2) // HGRN_HEAD_DIM
HGRN_WIDTH = HGRN_HEADS * HGRN_HEAD_DIM
HGRN_CHUNK = 64
ATTN_HEAD_DIM = 64
ATTN_HEADS = D_MODEL // ATTN_HEAD_DIM
DILATED_BRANCHES = ((128, 1), (512, 4), (2048, 16))
ATTN_BLOCK = 128
ROPE_THETA = 10000.0
D_FF = -(-8 * D_MODEL // (3 * 256)) * 256
RMS_EPS = 1e-6
N_EVEN = (DEPTH + 1) // 2
N_ODD = DEPTH // 2
REC_IN_WIDTH = 3 * CONV_CHANNELS + 4 * HGRN_WIDTH

kernel_name = "hybrid_conv_hgrn2_dilated_attn_block"


def rmsnorm(x, g):
    xf = x.astype(jnp.float32)
    y = xf * lax.rsqrt(jnp.mean(xf * xf, axis=-1, keepdims=True) + RMS_EPS)
    return (y * g.astype(jnp.float32)).astype(x.dtype)


def short_conv_mixer(b_gate, c_gate, v, conv_w):
    u = c_gate * v
    S = u.shape[1]
    up = jnp.pad(u, ((0, 0), (CONV_WIDTH - 1, 0), (0, 0)))
    y = conv_w[0] * up[:, 0:S]
    for j in range(1, CONV_WIDTH):
        y = y + conv_w[j] * up[:, j:j + S]
    return b_gate * y


def hgrn2_mixer(q, f_logit, i_in, g, lb, norm_g):
    B, S, _ = q.shape
    H, Dh, C = HGRN_HEADS, HGRN_HEAD_DIM, HGRN_CHUNK
    nc = S // C

    def heads(t):
        return t.reshape(B, S, H, Dh).transpose(0, 2, 1, 3).astype(jnp.float32)

    z = heads(f_logit)
    lbh = lb.reshape(H, 1, Dh)
    log_f = jnp.logaddexp(jnp.log(lbh), jnp.log1p(-lbh) + jax.nn.log_sigmoid(z))
    k = (1.0 - lbh) * jax.nn.sigmoid(-z)
    qh = heads(q)
    vh = heads(jax.nn.silu(i_in))

    def chunks(t):
        return t.reshape(B, H, nc, C, Dh).transpose(2, 0, 1, 3, 4)

    causal = jnp.tril(jnp.ones((C, C), dtype=bool))

    def step(state, inp):
        qc, kc, vc, lfc = inp
        G = jnp.cumsum(lfc, axis=2)
        o_inter = jnp.einsum('bhtk,bhkv->bhtv', qc * jnp.exp(G), state)
        rel = G[:, :, :, None, :] - G[:, :, None, :, :]
        decay = jnp.exp(jnp.where(causal[:, :, None], rel, -jnp.inf))
        scores = jnp.einsum('bhtk,bhtsk,bhsk->bhts', qc, decay, kc)
        o = o_inter + jnp.einsum('bhts,bhsv->bhtv', scores, vc)
        G_last = G[:, :, -1:, :]
        new_state = (jnp.exp(G_last[:, :, 0, :])[..., None] * state
                     + jnp.einsum('bhsk,bhsv->bhkv', kc * jnp.exp(G_last - G), vc))
        return new_state, o

    state0 = jnp.zeros((B, H, Dh, Dh), jnp.float32)
    _, o = lax.scan(step, state0, (chunks(qh), chunks(k), chunks(vh), chunks(log_f)))
    o = o.transpose(1, 2, 0, 3, 4).reshape(B, H, S, Dh)
    o = rmsnorm(o, norm_g)
    o = o.transpose(0, 2, 1, 3).reshape(B, S, HGRN_WIDTH)
    return (o * jax.nn.silu(g.astype(jnp.float32))).astype(q.dtype)


def rope(t, positions):
    hd = t.shape[-1]
    half = hd // 2
    freqs = ROPE_THETA ** (-jnp.arange(half, dtype=jnp.float32) / half)
    ang = positions.astype(jnp.float32)[:, None] * freqs[None, :]
    cos, sin = jnp.cos(ang), jnp.sin(ang)
    tf = t.astype(jnp.float32)
    t1, t2 = tf[..., :half], tf[..., half:]
    return jnp.concatenate([t1 * cos - t2 * sin, t1 * sin + t2 * cos], axis=-1)


def dilated_branch(q, k, v, dilation, span):
    B, H, S, hd = q.shape
    L = S // dilation
    nb = -(-L // ATTN_BLOCK)
    Lp = nb * ATTN_BLOCK

    def strided(t):
        t = t.reshape(B, H, L, dilation, hd).transpose(0, 1, 3, 2, 4)
        t = jnp.pad(t, ((0, 0), (0, 0), (0, 0), (0, Lp - L), (0, 0)))
        return t.reshape(B, H, dilation, nb, ATTN_BLOCK, hd)

    def with_prev(t):
        prev = jnp.pad(t, ((0, 0), (0, 0), (0, 0), (1, 0), (0, 0), (0, 0)))[:, :, :, :-1]
        return jnp.concatenate([prev, t], axis=4)

    qb = strided(q)
    kk = with_prev(strided(k))
    vv = with_prev(strided(v))
    s = jnp.einsum('bhrnqd,bhrnkd->bhrnqk', qb, kk).astype(jnp.float32)
    qi = jnp.arange(ATTN_BLOCK)[:, None]
    kj = jnp.arange(2 * ATTN_BLOCK)[None, :]
    rel = ATTN_BLOCK + qi - kj
    band = (rel >= 0) & (rel <= span)
    has_prev = (jnp.arange(nb)[:, None, None] > 0) | (kj >= ATTN_BLOCK)[None]
    mask = band[None] & has_prev
    s = jnp.where(mask, s, -jnp.inf)
    m = jnp.max(s, axis=-1, keepdims=True)
    p = jnp.exp(s - m)
    denom = jnp.sum(p, axis=-1, keepdims=True)
    o = jnp.einsum('bhrnqk,bhrnkd->bhrnqd', (p / denom).astype(vv.dtype), vv)
    lse = (m + jnp.log(denom))[..., 0]
    o = o.reshape(B, H, dilation, Lp, hd)[:, :, :, :L].transpose(0, 1, 3, 2, 4).reshape(B, H, S, hd)
    lse = lse.reshape(B, H, dilation, Lp)[..., :L].transpose(0, 1, 3, 2).reshape(B, H, S)
    return o, lse


def dilated_attention(q, k, v):
    outs, lses = [], []
    for window, dilation in DILATED_BRANCHES:
        o, lse = dilated_branch(q, k, v, dilation, window // dilation)
        outs.append(o.astype(jnp.float32))
        lses.append(lse)
    w = jax.nn.softmax(jnp.stack(lses, axis=0), axis=0)
    return jnp.einsum('nbhs,nbhsd->bhsd', w, jnp.stack(outs, axis=0))


def swiglu(h, w_in, w_out):
    gate, up = jnp.split(h @ w_in, 2, axis=-1)
    return (jax.nn.silu(gate) * up) @ w_out


def setup_inputs(seed: int = 0) -> dict:
    key = jax.random.key(seed)
    ks = jax.random.split(key, 13)
    D = D_MODEL
    f32 = jnp.float32

    def nrm(k, shape, scale):
        return jax.random.normal(k, shape, f32) * scale

    return {
        "x": nrm(ks[0], (BATCH, SEQ, D), 1.0),
        "mix_norm": 1.0 + nrm(ks[1], (DEPTH, D), 0.02),
        "ffn_norm": 1.0 + nrm(ks[2], (DEPTH, D), 0.02),
        "w_in_rec": nrm(ks[3], (N_EVEN, D, REC_IN_WIDTH), D ** -0.5),
        "conv_w": nrm(ks[4], (N_EVEN, CONV_WIDTH, CONV_CHANNELS), CONV_WIDTH ** -0.5),
        "hgrn_lb": nrm(ks[5], (DEPTH + 1, HGRN_WIDTH), 0.5),
        "hgrn_norm": 1.0 + nrm(ks[6], (N_EVEN, HGRN_HEAD_DIM), 0.02),
        "w_out_rec": nrm(ks[7], (N_EVEN, CONV_CHANNELS + HGRN_WIDTH, D), (CONV_CHANNELS + HGRN_WIDTH) ** -0.5),
        "w_qkv_attn": nrm(ks[8], (N_ODD, D, 3 * ATTN_HEADS * ATTN_HEAD_DIM), D ** -0.5),
        "w_o_attn": nrm(ks[9], (N_ODD, ATTN_HEADS * ATTN_HEAD_DIM, D), (ATTN_HEADS * ATTN_HEAD_DIM) ** -0.5),
        "w_ffn_in": nrm(ks[10], (DEPTH, D, 2 * D_FF), D ** -0.5),
        "w_ffn_out": nrm(ks[11], (DEPTH, D_FF, D), D_FF ** -0.5),
        "final_norm": 1.0 + nrm(ks[12], (D,), 0.02),
    }


def reference(x, mix_norm, ffn_norm, w_in_rec, conv_w, hgrn_lb, hgrn_norm, w_out_rec,
              w_qkv_attn, w_o_attn, w_ffn_in, w_ffn_out, final_norm):
    B, S, D = x.shape
    positions = jnp.arange(S, dtype=jnp.int32)
    lb_cum = jnp.cumsum(jax.nn.softmax(hgrn_lb.astype(jnp.float32), axis=0), axis=0)
    splits = [CONV_CHANNELS, 2 * CONV_CHANNELS, 3 * CONV_CHANNELS,
              3 * CONV_CHANNELS + HGRN_WIDTH, 3 * CONV_CHANNELS + 2 * HGRN_WIDTH,
              3 * CONV_CHANNELS + 3 * HGRN_WIDTH]
    for layer in range(DEPTH):
        h = rmsnorm(x, mix_norm[layer])
        if layer % 2 == 0:
            e = layer // 2
            proj = h @ w_in_rec[e]
            b_g, c_g, v_c, q_r, f_r, i_r, g_r = jnp.split(proj, splits, axis=-1)
            a_out = short_conv_mixer(b_g, c_g, v_c, conv_w[e])
            lb = lb_cum[layer + 1] - lb_cum[0]
            b_out = hgrn2_mixer(q_r, f_r, i_r, g_r, lb, hgrn_norm[e])
            mix = jnp.concatenate([a_out.astype(x.dtype), b_out.astype(x.dtype)], axis=-1) @ w_out_rec[e]
        else:
            o_idx = layer // 2
            qkv = (h @ w_qkv_attn[o_idx]).reshape(B, S, 3, ATTN_HEADS, ATTN_HEAD_DIM)
            qkv = qkv.transpose(2, 0, 3, 1, 4)
            q = rope(qkv[0], positions) * (ATTN_HEAD_DIM ** -0.5)
            k = rope(qkv[1], positions)
            v = qkv[2]
            o = dilated_attention(q, k, v)
            o = o.transpose(0, 2, 1, 3).reshape(B, S, ATTN_HEADS * ATTN_HEAD_DIM).astype(x.dtype)
            mix = o @ w_o_attn[o_idx]
        x = x + mix.astype(x.dtype)
        h = rmsnorm(x, ffn_norm[layer])
        x = x + swiglu(h, w_ffn_in[layer], w_ffn_out[layer]).astype(x.dtype)
    return rmsnorm(x, final_norm)
```

```python
import functools

import jax
import jax.numpy as jnp
from jax import lax
from jax.experimental import pallas as pl
from jax.experimental.pallas import tpu as pltpu

D_MODEL = 1024
CONV_WIDTH = 3
CONV_CHANNELS = 512
HGRN_HEAD_DIM = 128
HGRN_HEADS = 4
HGRN_WIDTH = 512
ATTN_HEAD_DIM = 64
ATTN_HEADS = 16
ATTN_BLOCK = 128
DILATIONS = (1, 4, 16)
ROPE_THETA = 10000.0
D_FF = 2816
RMS_EPS = 1e-6
REC_IN_WIDTH = 3 * CONV_CHANNELS + 4 * HGRN_WIDTH

LANES = 128
VMEM_LIMIT_BYTES = 56 * 1024 * 1024
ROW_TILE = 512
MIX_TILE = 256
HGRN_CHUNK = 32
MASK_VALUE = -1e30

F32 = jnp.float32
BF16 = jnp.bfloat16


def _rmsnorm(x, g):
    return x * lax.rsqrt(jnp.mean(x * x, axis=-1, keepdims=True) + RMS_EPS) * g


def _const_spec(shape):
    zeros = (0,) * len(shape)
    return pl.BlockSpec(shape, lambda *_: zeros, pipeline_mode=pl.Buffered(1))


def _params(n_grid):
    return pltpu.CompilerParams(
        dimension_semantics=("arbitrary",) * n_grid, vmem_limit_bytes=VMEM_LIMIT_BYTES)


def _norm_matmul_kernel(x_ref, g_ref, w_ref, o_ref):
    h = _rmsnorm(x_ref[...], g_ref[...]).astype(BF16)
    o_ref[...] = jnp.dot(h, w_ref[...], preferred_element_type=F32).astype(o_ref.dtype)


def _norm_matmul(x, g, w, out_dtype):
    m, d = x.shape
    n = w.shape[1]
    return pl.pallas_call(
        _norm_matmul_kernel,
        grid=(m // ROW_TILE,),
        in_specs=[pl.BlockSpec((ROW_TILE, d), lambda i: (i, 0)),
                  _const_spec((1, d)),
                  _const_spec((d, n))],
        out_specs=pl.BlockSpec((ROW_TILE, n), lambda i: (i, 0)),
        out_shape=jax.ShapeDtypeStruct((m, n), out_dtype),
        compiler_params=_params(1),
        name="norm_matmul",
    )(x, g, w)


def _mixer_kernel(proj_ref, convw_ref, lb_ref, hnorm_ref, out_ref,
                  ubuf, logf_s, k_s, state):
    ts = MIX_TILE
    cc = CONV_CHANNELS
    j = pl.program_id(1)

    @pl.when(j == 0)
    def _():
        state[...] = jnp.zeros_like(state)
        ubuf[0:8, :] = jnp.zeros((8, cc), F32)

    u = proj_ref[0, :, cc:2 * cc] * proj_ref[0, :, 2 * cc:3 * cc]
    ubuf[8:8 + ts, :] = u
    y = (convw_ref[0:1, :] * ubuf[6:6 + ts, :]
         + convw_ref[1:2, :] * ubuf[7:7 + ts, :]
         + convw_ref[2:3, :] * u)
    out_ref[0, :, 0:cc] = (proj_ref[0, :, 0:cc] * y).astype(out_ref.dtype)
    ubuf[0:8, :] = u[ts - 8:ts, :]

    lbp = lb_ref[...]
    e = jnp.exp(lbp - jnp.max(lbp, axis=0, keepdims=True))
    sm = e / jnp.sum(e, axis=0, keepdims=True)
    cum0 = sm[0:1, :]
    cum1 = cum0 + sm[1:2, :]
    lb = cum1 - cum0
    log_lb = jnp.log(lb)
    log_1mlb = jnp.log1p(-lb)
    q0 = 3 * cc
    z = proj_ref[0, :, q0 + HGRN_WIDTH:q0 + 2 * HGRN_WIDTH]
    ez = jnp.exp(-jnp.abs(z))
    log_sig = jnp.minimum(z, 0.0) - jnp.log1p(ez)
    b = log_1mlb + log_sig
    logf_s[...] = jnp.maximum(log_lb, b) + jnp.log1p(jnp.exp(-jnp.abs(log_lb - b)))
    inv = 1.0 / (1.0 + ez)
    k_s[...] = (1.0 - lb) * jnp.where(z >= 0.0, ez * inv, inv)

    c = HGRN_CHUNK
    row = lax.broadcasted_iota(jnp.int32, (c, c), 0)
    col = lax.broadcasted_iota(jnp.int32, (c, c), 1)
    causal = row >= col
    tril = causal.astype(BF16)
    hnorm = hnorm_ref[...]

    def chunk_body(ci, carry):
        r0 = pl.multiple_of(ci * c, c)
        rows = pl.ds(r0, c)
        lf = logf_s[rows, :]
        lf_hi = lf.astype(BF16)
        lf_lo = (lf - lf_hi.astype(F32)).astype(BF16)
        g_all = (jnp.dot(tril, lf_hi, preferred_element_type=F32)
                 + jnp.dot(tril, lf_lo, preferred_element_type=F32))
        for hd in range(HGRN_HEADS):
            lanes = slice(hd * LANES, (hd + 1) * LANES)
            gc = g_all[:, lanes]
            gl = gc[c - 1:c, :]
            rho = 0.5 * gl
            qh = proj_ref[0, rows, q0 + hd * LANES:q0 + (hd + 1) * LANES]
            ih = proj_ref[0, rows, q0 + 2 * HGRN_WIDTH + hd * LANES:q0 + 2 * HGRN_WIDTH + (hd + 1) * LANES]
            gh = proj_ref[0, rows, q0 + 3 * HGRN_WIDTH + hd * LANES:q0 + 3 * HGRN_WIDTH + (hd + 1) * LANES]
            kh = k_s[rows, lanes]
            vh = (ih * (1.0 / (1.0 + jnp.exp(-ih)))).astype(BF16)
            qd = (qh * jnp.exp(gc - rho)).astype(BF16)
            kd = (kh * jnp.exp(rho - gc)).astype(BF16)
            a = lax.dot_general(qd, kd, (((1,), (1,)), ((), ())), preferred_element_type=F32)
            a = jnp.where(causal, a, 0.0).astype(BF16)
            st = state[hd]
            qg = (qh * jnp.exp(gc)).astype(BF16)
            o = (lax.dot_general(qg, st.astype(BF16), (((1,), (1,)), ((), ())), preferred_element_type=F32)
                 + jnp.dot(a, vh, preferred_element_type=F32))
            kd2 = (kh * jnp.exp(gl - gc)).astype(BF16)
            upd = lax.dot_general(vh, kd2, (((0,), (0,)), ((), ())), preferred_element_type=F32)
            state[hd] = st * jnp.exp(gl) + upd
            o = _rmsnorm(o, hnorm)
            o = o * (gh * (1.0 / (1.0 + jnp.exp(-gh))))
            out_ref[0, rows, cc + hd * LANES:cc + (hd + 1) * LANES] = o.astype(out_ref.dtype)
        return carry

    lax.fori_loop(0, ts // c, chunk_body, 0)


def _mixer(proj, conv_w, hgrn_lb, hgrn_norm, batch, seq):
    ts = MIX_TILE
    return pl.pallas_call(
        _mixer_kernel,
        grid=(batch, seq // ts),
        in_specs=[pl.BlockSpec((1, ts, REC_IN_WIDTH), lambda b, j: (b, j, 0)),
                  _const_spec((CONV_WIDTH, CONV_CHANNELS)),
                  _const_spec((3, HGRN_WIDTH)),
                  _const_spec((1, HGRN_HEAD_DIM))],
        out_specs=pl.BlockSpec((1, ts, D_MODEL), lambda b, j: (b, j, 0)),
        out_shape=jax.ShapeDtypeStruct((batch, seq, D_MODEL), BF16),
        scratch_shapes=[pltpu.VMEM((ts + 8, CONV_CHANNELS), F32),
                        pltpu.VMEM((ts, HGRN_WIDTH), F32),
                        pltpu.VMEM((ts, HGRN_WIDTH), F32),
                        pltpu.VMEM((HGRN_HEADS, HGRN_HEAD_DIM, HGRN_HEAD_DIM), F32)],
        compiler_params=_params(2),
        name="conv_hgrn_mixer",
    )(proj, conv_w, hgrn_lb, hgrn_norm)


def _attn_kernel(q_ref, k_ref, v_ref, cos_ref, sin_ref, o_ref,
                 qs, ks, m_s, l_s, acc_s, *, seq):
    blk = ATTN_BLOCK
    n_chunks = seq // blk
    lane = lax.broadcasted_iota(jnp.int32, (blk, LANES), 1)
    first_head = lane < ATTN_HEAD_DIM
    first_half = (lane % ATTN_HEAD_DIM) < (ATTN_HEAD_DIM // 2)

    def rope_body(ci, carry):
        rows = pl.ds(pl.multiple_of(ci * blk, blk), blk)
        cos = cos_ref[rows, :]
        sin = sin_ref[rows, :]

        def rot(t):
            partner = jnp.where(first_half,
                                pltpu.roll(t, LANES - ATTN_HEAD_DIM // 2, 1),
                                pltpu.roll(t, ATTN_HEAD_DIM // 2, 1))
            return t * cos + partner * sin

        qs[rows, :] = rot(q_ref[0, rows, :]) * (ATTN_HEAD_DIM ** -0.5)
        ks[rows, :] = rot(k_ref[0, rows, :])
        return carry

    lax.fori_loop(0, n_chunks, rope_body, 0)

    qi = lax.broadcasted_iota(jnp.int32, (blk, blk), 0)
    kj = lax.broadcasted_iota(jnp.int32, (blk, blk), 1)
    bias_prev = jnp.where(kj >= qi, 0.0, MASK_VALUE).astype(F32)
    bias_cur = jnp.where(kj <= qi, 0.0, MASK_VALUE).astype(F32)

    def head_rows(t):
        return jnp.concatenate([jnp.where(first_head, t, 0.0), jnp.where(first_head, 0.0, t)], axis=0)

    def pick(t):
        return jnp.where(first_head, t[:blk], t[blk:])

    def attend(branch, d, q_start, prev_start, prev_pen):
        sl = lambda start: pl.ds(start, blk, stride=d) if d > 1 else pl.ds(start, blk)
        q2 = head_rows(qs[sl(q_start), :]).astype(BF16)
        kc = ks[sl(q_start), :]
        vc = v_ref[0, sl(q_start), :]
        if prev_start is None:
            kk = kc.astype(BF16)
            vv = vc.astype(BF16)
            bias = bias_cur
        else:
            kk = jnp.concatenate([ks[sl(prev_start), :], kc], axis=0).astype(BF16)
            vv = jnp.concatenate([v_ref[0, sl(prev_start), :], vc], axis=0).astype(BF16)
            bias = jnp.concatenate([bias_prev + prev_pen, bias_cur], axis=1)
        s = lax.dot_general(q2, kk, (((1,), (1,)), ((), ())), preferred_element_type=F32)
        s = s + jnp.concatenate([bias, bias], axis=0)
        m = jnp.max(s, axis=-1, keepdims=True)
        p = jnp.exp(s - m)
        l = jnp.sum(p, axis=-1, keepdims=True)
        pv = jnp.dot(p.astype(BF16), vv, preferred_element_type=F32)
        m_s[branch, sl(q_start), :] = pick(jnp.broadcast_to(m, (2 * blk, LANES)))
        l_s[branch, sl(q_start), :] = pick(jnp.broadcast_to(l, (2 * blk, LANES)))
        acc_s[branch, sl(q_start), :] = pick(pv)

    for branch, d in enumerate(DILATIONS):
        n_blocks = seq // (d * blk)

        if n_blocks == 1:
            def body(r, carry, branch=branch, d=d):
                attend(branch, d, r, None, None)
                return carry
        else:
            def body(idx, carry, branch=branch, d=d, n_blocks=n_blocks):
                r = idx // n_blocks
                n = idx % n_blocks
                q_start = r + n * (d * blk)
                prev_start = r + jnp.maximum(n - 1, 0) * (d * blk)
                if d == 1:
                    q_start = pl.multiple_of(q_start, blk)
                    prev_start = pl.multiple_of(prev_start, blk)
                prev_pen = jnp.where(n > 0, 0.0, MASK_VALUE).astype(F32)
                attend(branch, d, q_start, prev_start, prev_pen)
                return carry

        lax.fori_loop(0, d * n_blocks, body, 0)

    def merge_body(ci, carry):
        rows = pl.ds(pl.multiple_of(ci * blk, blk), blk)
        ms = [m_s[i, rows, :] for i in range(len(DILATIONS))]
        m_all = functools.reduce(jnp.maximum, ms)
        ws = [jnp.exp(mi - m_all) for mi in ms]
        num = sum(w * acc_s[i, rows, :] for i, w in enumerate(ws))
        den = sum(w * l_s[i, rows, :] for i, w in enumerate(ws))
        o_ref[0, rows, :] = (num / den).astype(o_ref.dtype)
        return carry

    lax.fori_loop(0, n_chunks, merge_body, 0)


def _attention(qkv, cos_t, sin_t, batch, seq):
    n_pairs = ATTN_HEADS * ATTN_HEAD_DIM // LANES
    blk_spec = lambda off: pl.BlockSpec((1, seq, LANES), lambda b, h: (b, 0, off + h))
    nb = len(DILATIONS)
    return pl.pallas_call(
        functools.partial(_attn_kernel, seq=seq),
        grid=(batch, n_pairs),
        in_specs=[blk_spec(0), blk_spec(n_pairs), blk_spec(2 * n_pairs),
                  _const_spec((seq, LANES)), _const_spec((seq, LANES))],
        out_specs=pl.BlockSpec((1, seq, LANES), lambda b, h: (b, 0, h)),
        out_shape=jax.ShapeDtypeStruct((batch, seq, D_MODEL), BF16),
        scratch_shapes=[pltpu.VMEM((seq, LANES), F32),
                        pltpu.VMEM((seq, LANES), F32),
                        pltpu.VMEM((nb, seq, LANES), F32),
                        pltpu.VMEM((nb, seq, LANES), F32),
                        pltpu.VMEM((nb, seq, LANES), F32)],
        compiler_params=_params(2),
        name="dilated_attention",
    )(qkv, qkv, qkv, cos_t, sin_t)


def _oproj_ffn_kernel(x_ref, a_ref, wo_ref, g_ref, win_ref, wout_ref, gfin_ref, o_ref, *, final_norm):
    x1 = x_ref[...] + jnp.dot(a_ref[...], wo_ref[...], preferred_element_type=F32)
    h = _rmsnorm(x1, g_ref[...]).astype(BF16)
    acc = x1
    n_split = 2
    fc = D_FF // n_split
    for c in range(n_split):
        gate = jnp.dot(h, win_ref[:, c * fc:(c + 1) * fc], preferred_element_type=F32)
        up = jnp.dot(h, win_ref[:, D_FF + c * fc:D_FF + (c + 1) * fc], preferred_element_type=F32)
        act = (gate * (1.0 / (1.0 + jnp.exp(-gate))) * up).astype(BF16)
        acc = acc + jnp.dot(act, wout_ref[c * fc:(c + 1) * fc, :], preferred_element_type=F32)
    if final_norm:
        acc = _rmsnorm(acc, gfin_ref[...])
    o_ref[...] = acc


def _oproj_ffn(x, a, wo, g, w_in, w_out, g_fin, final_norm):
    m, d = x.shape
    row_spec = pl.BlockSpec((ROW_TILE, d), lambda i: (i, 0))
    return pl.pallas_call(
        functools.partial(_oproj_ffn_kernel, final_norm=final_norm),
        grid=(m // ROW_TILE,),
        in_specs=[row_spec, row_spec,
                  _const_spec(wo.shape), _const_spec((1, d)),
                  _const_spec(w_in.shape), _const_spec(w_out.shape), _const_spec((1, d))],
        out_specs=row_spec,
        out_shape=jax.ShapeDtypeStruct((m, d), F32),
        compiler_params=_params(1),
        name="oproj_ffn",
    )(x, a, wo, g, w_in, w_out, g_fin)


def _rope_tables(seq):
    half = ATTN_HEAD_DIM // 2
    freqs = ROPE_THETA ** (-jnp.arange(half, dtype=F32) / half)
    ang = jnp.arange(seq, dtype=F32)[:, None] * freqs[None, :]
    cos, sin = jnp.cos(ang), jnp.sin(ang)
    reps = LANES // ATTN_HEAD_DIM
    cos_t = jnp.tile(jnp.concatenate([cos, cos], axis=1), (1, reps))
    sin_t = jnp.tile(jnp.concatenate([-sin, sin], axis=1), (1, reps))
    return cos_t, sin_t


def kernel(x, mix_norm, ffn_norm, w_in_rec, conv_w, hgrn_lb, hgrn_norm, w_out_rec,
           w_qkv_attn, w_o_attn, w_ffn_in, w_ffn_out, final_norm):
    batch, seq, d = x.shape
    m = batch * seq
    xf = x.reshape(m, d)
    row = lambda v: v.reshape(1, -1)

    proj = _norm_matmul(xf, row(mix_norm[0]), w_in_rec[0].astype(BF16), F32)
    mix = _mixer(proj.reshape(batch, seq, REC_IN_WIDTH), conv_w[0], hgrn_lb, row(hgrn_norm[0]), batch, seq)
    xf = _oproj_ffn(xf, mix.reshape(m, d), w_out_rec[0].astype(BF16), row(ffn_norm[0]),
                    w_ffn_in[0].astype(BF16), w_ffn_out[0].astype(BF16), row(final_norm), False)

    qkv = _norm_matmul(xf, row(mix_norm[1]), w_qkv_attn[0].astype(BF16), F32)
    cos_t, sin_t = _rope_tables(seq)
    att = _attention(qkv.reshape(batch, seq, 3 * D_MODEL), cos_t, sin_t, batch, seq)
    xf = _oproj_ffn(xf, att.reshape(m, d), w_o_attn[0].astype(BF16), row(ffn_norm[1]),
                    w_ffn_in[1].astype(BF16), w_ffn_out[1].astype(BF16), row(final_norm), True)
    return xf.reshape(batch, seq, d)
```

```python
import functools

import jax
import jax.numpy as jnp
import numpy as np
from jax import lax
from jax.experimental import pallas as pl
from jax.experimental.pallas import tpu as pltpu

D_MODEL = 1024
CONV_WIDTH = 3
CONV_CHANNELS = 512
HGRN_HEAD_DIM = 128
HGRN_HEADS = 4
HGRN_WIDTH = 512
ATTN_HEAD_DIM = 64
ATTN_HEADS = 16
ATTN_BLOCK = 128
DILATIONS = (1, 4, 16)
ROPE_THETA = 10000.0
D_FF = 2816
RMS_EPS = 1e-6
REC_IN_WIDTH = 3 * CONV_CHANNELS + 4 * HGRN_WIDTH

LANES = 128
VMEM_LIMIT_BYTES = 56 * 1024 * 1024
ROW_TILE = 512
MIX_TILE = 256
HGRN_CHUNK = 32
MASK_VALUE = -1e30
ATTN_GROUP = 4

F32 = jnp.float32
BF16 = jnp.bfloat16


def _rmsnorm(x, g):
    return x * lax.rsqrt(jnp.mean(x * x, axis=-1, keepdims=True) + RMS_EPS) * g


def _const_spec(shape):
    zeros = (0,) * len(shape)
    return pl.BlockSpec(shape, lambda *_: zeros, pipeline_mode=pl.Buffered(1))


def _params(n_grid):
    return pltpu.CompilerParams(
        dimension_semantics=("arbitrary",) * n_grid, vmem_limit_bytes=VMEM_LIMIT_BYTES)


def _norm_matmul_kernel(x_ref, g_ref, w_ref, o_ref):
    h = _rmsnorm(x_ref[...], g_ref[...]).astype(BF16)
    o_ref[...] = jnp.dot(h, w_ref[...], preferred_element_type=F32).astype(o_ref.dtype)


def _norm_matmul(x, g, w, out_dtype):
    m, d = x.shape
    n = w.shape[1]
    return pl.pallas_call(
        _norm_matmul_kernel,
        grid=(m // ROW_TILE,),
        in_specs=[pl.BlockSpec((ROW_TILE, d), lambda i: (i, 0)),
                  _const_spec((1, d)),
                  _const_spec((d, n))],
        out_specs=pl.BlockSpec((ROW_TILE, n), lambda i: (i, 0)),
        out_shape=jax.ShapeDtypeStruct((m, n), out_dtype),
        compiler_params=_params(1),
        name="norm_matmul",
    )(x, g, w)


def _mixer_kernel(proj_ref, convw_ref, lb_ref, hnorm_ref, out_ref,
                  ubuf, logf_s, k_s, state):
    ts = MIX_TILE
    cc = CONV_CHANNELS
    j = pl.program_id(1)

    @pl.when(j == 0)
    def _():
        state[...] = jnp.zeros_like(state)
        ubuf[0:8, :] = jnp.zeros((8, cc), F32)

    u = proj_ref[0, :, cc:2 * cc] * proj_ref[0, :, 2 * cc:3 * cc]
    ubuf[8:8 + ts, :] = u
    y = (convw_ref[0:1, :] * ubuf[6:6 + ts, :]
         + convw_ref[1:2, :] * ubuf[7:7 + ts, :]
         + convw_ref[2:3, :] * u)
    out_ref[0, :, 0:cc] = (proj_ref[0, :, 0:cc] * y).astype(out_ref.dtype)
    ubuf[0:8, :] = u[ts - 8:ts, :]

    lbp = lb_ref[...]
    e = jnp.exp(lbp - jnp.max(lbp, axis=0, keepdims=True))
    sm = e / jnp.sum(e, axis=0, keepdims=True)
    cum0 = sm[0:1, :]
    cum1 = cum0 + sm[1:2, :]
    lb = cum1 - cum0
    log_lb = jnp.log(lb)
    log_1mlb = jnp.log1p(-lb)
    q0 = 3 * cc
    z = proj_ref[0, :, q0 + HGRN_WIDTH:q0 + 2 * HGRN_WIDTH]
    ez = jnp.exp(-jnp.abs(z))
    log_sig = jnp.minimum(z, 0.0) - jnp.log1p(ez)
    b = log_1mlb + log_sig
    logf_s[...] = jnp.maximum(log_lb, b) + jnp.log1p(jnp.exp(-jnp.abs(log_lb - b)))
    inv = 1.0 / (1.0 + ez)
    k_s[...] = (1.0 - lb) * jnp.where(z >= 0.0, ez * inv, inv)

    c = HGRN_CHUNK
    row = lax.broadcasted_iota(jnp.int32, (c, c), 0)
    col = lax.broadcasted_iota(jnp.int32, (c, c), 1)
    causal = row >= col
    tril = causal.astype(BF16)
    hnorm = hnorm_ref[...]

    def chunk_body(ci, carry):
        r0 = pl.multiple_of(ci * c, c)
        rows = pl.ds(r0, c)
        lf = logf_s[rows, :]
        lf_hi = lf.astype(BF16)
        lf_lo = (lf - lf_hi.astype(F32)).astype(BF16)
        g_all = (jnp.dot(tril, lf_hi, preferred_element_type=F32)
                 + jnp.dot(tril, lf_lo, preferred_element_type=F32))
        for hd in range(HGRN_HEADS):
            lanes = slice(hd * LANES, (hd + 1) * LANES)
            gc = g_all[:, lanes]
            gl = gc[c - 1:c, :]
            rho = 0.5 * gl
            qh = proj_ref[0, rows, q0 + hd * LANES:q0 + (hd + 1) * LANES]
            ih = proj_ref[0, rows, q0 + 2 * HGRN_WIDTH + hd * LANES:q0 + 2 * HGRN_WIDTH + (hd + 1) * LANES]
            gh = proj_ref[0, rows, q0 + 3 * HGRN_WIDTH + hd * LANES:q0 + 3 * HGRN_WIDTH + (hd + 1) * LANES]
            kh = k_s[rows, lanes]
            vh = (ih * (1.0 / (1.0 + jnp.exp(-ih)))).astype(BF16)
            qd = (qh * jnp.exp(gc - rho)).astype(BF16)
            kd = (kh * jnp.exp(rho - gc)).astype(BF16)
            a = lax.dot_general(qd, kd, (((1,), (1,)), ((), ())), preferred_element_type=F32)
            a = jnp.where(causal, a, 0.0).astype(BF16)
            st = state[hd]
            qg = (qh * jnp.exp(gc)).astype(BF16)
            o = (lax.dot_general(qg, st.astype(BF16), (((1,), (1,)), ((), ())), preferred_element_type=F32)
                 + jnp.dot(a, vh, preferred_element_type=F32))
            kd2 = (kh * jnp.exp(gl - gc)).astype(BF16)
            upd = lax.dot_general(vh, kd2, (((0,), (0,)), ((), ())), preferred_element_type=F32)
            state[hd] = st * jnp.exp(gl) + upd
            o = _rmsnorm(o, hnorm)
            o = o * (gh * (1.0 / (1.0 + jnp.exp(-gh))))
            out_ref[0, rows, cc + hd * LANES:cc + (hd + 1) * LANES] = o.astype(out_ref.dtype)
        return carry

    lax.fori_loop(0, ts // c, chunk_body, 0)


def _mixer(proj, conv_w, hgrn_lb, hgrn_norm, batch, seq):
    ts = MIX_TILE
    return pl.pallas_call(
        _mixer_kernel,
        grid=(batch, seq // ts),
        in_specs=[pl.BlockSpec((1, ts, REC_IN_WIDTH), lambda b, j: (b, j, 0)),
                  _const_spec((CONV_WIDTH, CONV_CHANNELS)),
                  _const_spec((3, HGRN_WIDTH)),
                  _const_spec((1, HGRN_HEAD_DIM))],
        out_specs=pl.BlockSpec((1, ts, D_MODEL), lambda b, j: (b, j, 0)),
        out_shape=jax.ShapeDtypeStruct((batch, seq, D_MODEL), BF16),
        scratch_shapes=[pltpu.VMEM((ts + 8, CONV_CHANNELS), F32),
                        pltpu.VMEM((ts, HGRN_WIDTH), F32),
                        pltpu.VMEM((ts, HGRN_WIDTH), F32),
                        pltpu.VMEM((HGRN_HEADS, HGRN_HEAD_DIM, HGRN_HEAD_DIM), F32)],
        compiler_params=_params(2),
        name="conv_hgrn_mixer",
    )(proj, conv_w, hgrn_lb, hgrn_norm)


N_RES = DILATIONS[-1]


def _segments(d):
    count = N_RES // d
    return count, ATTN_BLOCK // count


def _attn_consts():
    blk = ATTN_BLOCK
    e = np.arange(blk)
    none = np.full((blk, blk), MASK_VALUE)
    kaug = []
    for d in DILATIONS:
        count, rows = _segments(d)
        pos = (e % rows) * count + e // rows
        kpos, qpos = pos[:, None], pos[None, :]
        prev_ok = np.where(kpos >= qpos, 0.0, MASK_VALUE)
        cur_ok = np.where(kpos <= qpos, 0.0, MASK_VALUE)
        kaug.append(np.stack([np.concatenate([none, cur_ok]), np.concatenate([prev_ok, cur_ok])]))
    qaug = np.concatenate([np.eye(blk), np.eye(blk)])
    return jnp.asarray(qaug, BF16), jnp.asarray(np.stack(kaug), BF16)


def _attn_kernel(q_ref, k_ref, v_ref, cos_ref, sin_ref, qaug_ref, kaug_ref, o_ref,
                 q0s, q1s, ks, vs, s_buf, m_s, l_s, acc_s, *, seq):
    blk = ATTN_BLOCK
    grp = ATTN_GROUP
    n_groups = (seq // blk) // grp
    assert n_groups % 2 == 0 and seq // N_RES == blk
    lane = lax.broadcasted_iota(jnp.int32, (blk, LANES), 1)
    first_head = lane < ATTN_HEAD_DIM
    first_half = (lane % ATTN_HEAD_DIM) < (ATTN_HEAD_DIM // 2)

    def rope_body(r, carry):
        rows = pl.ds(r, blk, stride=N_RES)
        cos = cos_ref[r]
        sin = sin_ref[r]

        def rot(t):
            partner = jnp.where(first_half,
                                pltpu.roll(t, LANES - ATTN_HEAD_DIM // 2, 1),
                                pltpu.roll(t, ATTN_HEAD_DIM // 2, 1))
            return t * cos + partner * sin

        qr = rot(q_ref[0, rows, :]) * (ATTN_HEAD_DIM ** -0.5)
        q0s[r] = jnp.where(first_head, qr, 0.0)
        q1s[r] = jnp.where(first_head, 0.0, qr)
        ks[r] = rot(k_ref[0, rows, :])
        vs[r] = v_ref[0, rows, :]
        return carry

    lax.fori_loop(0, N_RES, rope_body, 0)

    def pick(top, bot):
        return jnp.where(first_head, top, bot)

    def block_id(d, j, u):
        n_blocks = seq // (d * blk)
        if n_blocks == 1:
            return j * grp + u, 0, None, None
        gpr = n_blocks // grp
        res = 0 if d == 1 else j // gpr
        n = (j % gpr) * grp + u if gpr > 1 else u
        if u > 0:
            return res, n, n - 1, 1
        if isinstance(n, int):
            return res, n, max(n - 1, 0), min(n, 1)
        return res, n, jnp.maximum(n - 1, 0), jnp.minimum(n, 1)

    def seg_rows(d, n):
        _, rows = _segments(d)
        start = n * rows
        return pl.ds(start if isinstance(start, int) else pl.multiple_of(start, rows), rows)

    def load_block(ref, d, res, n):
        count, _ = _segments(d)
        return jnp.concatenate([ref[res + d * i, seg_rows(d, n), :] for i in range(count)], axis=0)

    def store_block(ref, branch, d, res, n, val):
        count, rows = _segments(d)
        for i in range(count):
            ref[branch, res + d * i, seg_rows(d, n), :] = val[i * rows:(i + 1) * rows]

    def scores(branch, d, j, u):
        res, n, n_prev, has_prev = block_id(d, j, u)
        q2 = jnp.concatenate([load_block(q0s, d, res, n), load_block(q1s, d, res, n)], axis=0).astype(BF16)
        q_ext = jnp.concatenate([q2, qaug_ref[...]], axis=1)
        kc = load_block(ks, d, res, n)
        if n_prev is None:
            kk = kc.astype(BF16)
            aug = kaug_ref[branch, 1, blk:2 * blk, :]
        else:
            kk = jnp.concatenate([load_block(ks, d, res, n_prev), kc], axis=0).astype(BF16)
            aug = kaug_ref[branch, has_prev]
        k_ext = jnp.concatenate([kk, aug], axis=1)
        s = lax.dot_general(q_ext, k_ext, (((1,), (1,)), ((), ())), preferred_element_type=F32)
        s_buf[j % 2, u, :, 0:s.shape[1]] = s

    def softmax_pv(branch, d, j, u):
        res, n, n_prev, _ = block_id(d, j, u)
        vc = load_block(vs, d, res, n)
        if n_prev is None:
            vv = vc.astype(BF16)
        else:
            vv = jnp.concatenate([load_block(vs, d, res, n_prev), vc], axis=0).astype(BF16)
        s = s_buf[j % 2, u, :, 0:vv.shape[0]]
        m = jnp.max(s, axis=-1, keepdims=True)
        p = jnp.exp(s - m).astype(BF16)
        v_ext = jnp.concatenate([vv, jnp.ones_like(vv)], axis=1)
        out = jnp.dot(p, v_ext, preferred_element_type=F32)
        mb = jnp.broadcast_to(m, (2 * blk, LANES))
        store_block(m_s, branch, d, res, n, pick(mb[:blk], mb[blk:]))
        store_block(l_s, branch, d, res, n, pick(out[:blk, LANES:], out[blk:, LANES:]))
        store_block(acc_s, branch, d, res, n, pick(out[:blk, :LANES], out[blk:, :LANES]))

    for u in range(grp):
        scores(0, DILATIONS[0], 0, u)
    for branch, d in enumerate(DILATIONS):
        def group_body(j, carry, branch=branch, d=d):
            for u in range(grp):
                softmax_pv(branch, d, j, u)
            for u in range(grp):
                scores(branch, d, j + 1, u)
            return carry

        lax.fori_loop(0, n_groups - 1, group_body, 0)
        for u in range(grp):
            softmax_pv(branch, d, n_groups - 1, u)
        if branch + 1 < len(DILATIONS):
            for u in range(grp):
                scores(branch + 1, DILATIONS[branch + 1], 0, u)

    def merge_body(r, carry):
        ms = [m_s[i, r] for i in range(len(DILATIONS))]
        m_all = functools.reduce(jnp.maximum, ms)
        ws = [jnp.exp(mi - m_all) for mi in ms]
        num = sum(w * acc_s[i, r] for i, w in enumerate(ws))
        den = sum(w * l_s[i, r] for i, w in enumerate(ws))
        o_ref[0, pl.ds(r, blk, stride=N_RES), :] = num / den
        return carry

    lax.fori_loop(0, N_RES, merge_body, 0)


def _attention(qkv, cos_t, sin_t, batch, seq):
    n_pairs = ATTN_HEADS * ATTN_HEAD_DIM // LANES
    blk_spec = lambda off: pl.BlockSpec((1, seq, LANES), lambda b, h: (b, 0, off + h))
    nb = len(DILATIONS)
    qaug, kaug = _attn_consts()
    res_major = (N_RES, seq // N_RES, LANES)
    return pl.pallas_call(
        functools.partial(_attn_kernel, seq=seq),
        grid=(batch, n_pairs),
        in_specs=[blk_spec(0), blk_spec(n_pairs), blk_spec(2 * n_pairs),
                  _const_spec(res_major), _const_spec(res_major),
                  _const_spec(qaug.shape), _const_spec(kaug.shape)],
        out_specs=pl.BlockSpec((1, seq, LANES), lambda b, h: (b, 0, h)),
        out_shape=jax.ShapeDtypeStruct((batch, seq, D_MODEL), F32),
        scratch_shapes=[pltpu.VMEM(res_major, F32),
                        pltpu.VMEM(res_major, F32),
                        pltpu.VMEM(res_major, F32),
                        pltpu.VMEM(res_major, F32),
                        pltpu.VMEM((2, ATTN_GROUP, 2 * ATTN_BLOCK, 2 * ATTN_BLOCK), F32),
                        pltpu.VMEM((nb,) + res_major, F32),
                        pltpu.VMEM((nb,) + res_major, F32),
                        pltpu.VMEM((nb,) + res_major, F32)],
        compiler_params=_params(2),
        name="dilated_attention",
    )(qkv, qkv, qkv, cos_t, sin_t, qaug, kaug)


def _oproj_ffn_kernel(x_ref, a_ref, wo_ref, g_ref, win_ref, wout_ref, gfin_ref, o_ref, *, final_norm):
    x1 = x_ref[...] + jnp.dot(a_ref[...].astype(BF16), wo_ref[...], preferred_element_type=F32)
    h = _rmsnorm(x1, g_ref[...]).astype(BF16)
    acc = x1
    n_split = 2
    fc = D_FF // n_split
    for c in range(n_split):
        gate = jnp.dot(h, win_ref[:, c * fc:(c + 1) * fc], preferred_element_type=F32)
        up = jnp.dot(h, win_ref[:, D_FF + c * fc:D_FF + (c + 1) * fc], preferred_element_type=F32)
        act = (gate * (1.0 / (1.0 + jnp.exp(-gate))) * up).astype(BF16)
        acc = acc + jnp.dot(act, wout_ref[c * fc:(c + 1) * fc, :], preferred_element_type=F32)
    if final_norm:
        acc = _rmsnorm(acc, gfin_ref[...])
    o_ref[...] = acc


def _oproj_ffn(x, a, wo, g, w_in, w_out, g_fin, final_norm):
    m, d = x.shape
    row_spec = pl.BlockSpec((ROW_TILE, d), lambda i: (i, 0))
    return pl.pallas_call(
        functools.partial(_oproj_ffn_kernel, final_norm=final_norm),
        grid=(m // ROW_TILE,),
        in_specs=[row_spec, row_spec,
                  _const_spec(wo.shape), _const_spec((1, d)),
                  _const_spec(w_in.shape), _const_spec(w_out.shape), _const_spec((1, d))],
        out_specs=row_spec,
        out_shape=jax.ShapeDtypeStruct((m, d), F32),
        compiler_params=_params(1),
        name="oproj_ffn",
    )(x, a, wo, g, w_in, w_out, g_fin)


def _rope_tables(seq):
    half = ATTN_HEAD_DIM // 2
    freqs = ROPE_THETA ** (-jnp.arange(half, dtype=F32) / half)
    ang = jnp.arange(seq, dtype=F32)[:, None] * freqs[None, :]
    cos, sin = jnp.cos(ang), jnp.sin(ang)
    reps = LANES // ATTN_HEAD_DIM
    cos_t = jnp.tile(jnp.concatenate([cos, cos], axis=1), (1, reps))
    sin_t = jnp.tile(jnp.concatenate([-sin, sin], axis=1), (1, reps))
    res_major = lambda t: t.reshape(seq // N_RES, N_RES, LANES).transpose(1, 0, 2)
    return res_major(cos_t), res_major(sin_t)


def kernel(x, mix_norm, ffn_norm, w_in_rec, conv_w, hgrn_lb, hgrn_norm, w_out_rec,
           w_qkv_attn, w_o_attn, w_ffn_in, w_ffn_out, final_norm):
    batch, seq, d = x.shape
    m = batch * seq
    xf = x.reshape(m, d)
    row = lambda v: v.reshape(1, -1)

    proj = _norm_matmul(xf, row(mix_norm[0]), w_in_rec[0].astype(BF16), F32)
    mix = _mixer(proj.reshape(batch, seq, REC_IN_WIDTH), conv_w[0], hgrn_lb, row(hgrn_norm[0]), batch, seq)
    xf = _oproj_ffn(xf, mix.reshape(m, d), w_out_rec[0].astype(BF16), row(ffn_norm[0]),
                    w_ffn_in[0].astype(BF16), w_ffn_out[0].astype(BF16), row(final_norm), False)

    qkv = _norm_matmul(xf, row(mix_norm[1]), w_qkv_attn[0].astype(BF16), F32)
    cos_t, sin_t = _rope_tables(seq)
    att = _attention(qkv.reshape(batch, seq, 3 * D_MODEL), cos_t, sin_t, batch, seq)
    xf = _oproj_ffn(xf, att.reshape(m, d), w_o_attn[0].astype(BF16), row(ffn_norm[1]),
                    w_ffn_in[1].astype(BF16), w_ffn_out[1].astype(BF16), row(final_norm), True)
    return xf.reshape(batch, seq, d)
```

```python
import functools

import jax
import jax.numpy as jnp
import numpy as np
from jax import lax
from jax.experimental import pallas as pl
from jax.experimental.pallas import tpu as pltpu

D_MODEL = 1024
CONV_WIDTH = 3
CONV_CHANNELS = 512
HGRN_HEAD_DIM = 128
HGRN_HEADS = 4
HGRN_WIDTH = 512
ATTN_HEAD_DIM = 64
ATTN_HEADS = 16
ATTN_BLOCK = 128
DILATIONS = (1, 4, 16)
ROPE_THETA = 10000.0
D_FF = 2816
RMS_EPS = 1e-6
REC_IN_WIDTH = 3 * CONV_CHANNELS + 4 * HGRN_WIDTH

LANES = 128
VMEM_LIMIT_BYTES = 56 * 1024 * 1024
ROW_TILE = 512
MIX_TILE = 256
HGRN_CHUNK = 32
MASK_VALUE = -1e30
ATTN_GROUP = 4

F32 = jnp.float32
BF16 = jnp.bfloat16


def _rmsnorm(x, g):
    return x * lax.rsqrt(jnp.mean(x * x, axis=-1, keepdims=True) + RMS_EPS) * g


def _const_spec(shape):
    zeros = (0,) * len(shape)
    return pl.BlockSpec(shape, lambda *_: zeros, pipeline_mode=pl.Buffered(1))


def _params(n_grid):
    return pltpu.CompilerParams(
        dimension_semantics=("arbitrary",) * n_grid, vmem_limit_bytes=VMEM_LIMIT_BYTES)


def _norm_matmul_kernel(x_ref, g_ref, w_ref, o_ref):
    h = _rmsnorm(x_ref[...], g_ref[...]).astype(BF16)
    o_ref[...] = jnp.dot(h, w_ref[...], preferred_element_type=F32).astype(o_ref.dtype)


def _norm_matmul(x, g, w, out_dtype):
    m, d = x.shape
    n = w.shape[1]
    return pl.pallas_call(
        _norm_matmul_kernel,
        grid=(m // ROW_TILE,),
        in_specs=[pl.BlockSpec((ROW_TILE, d), lambda i: (i, 0)),
                  _const_spec((1, d)),
                  _const_spec((d, n))],
        out_specs=pl.BlockSpec((ROW_TILE, n), lambda i: (i, 0)),
        out_shape=jax.ShapeDtypeStruct((m, n), out_dtype),
        compiler_params=_params(1),
        name="norm_matmul",
    )(x, g, w)


def _mixer_consts():
    ts, c = MIX_TILE, HGRN_CHUNK
    t = np.arange(ts)
    same = (t[:, None] // c) == (t[None, :] // c)
    tril = same & (t[:, None] >= t[None, :])
    cum = np.concatenate([tril, same]).astype(np.float32)
    ind = (t[:, None] // c == np.arange(LANES)[None, :]).astype(np.float32)
    return jnp.asarray(cum, BF16), jnp.asarray(ind, BF16)


def _mixer_kernel(proj_ref, convw_ref, lb_ref, hnorm_ref, cum_ref, ind_ref, out_ref,
                  ubuf, state, vblk):
    ts = MIX_TILE
    cc = CONV_CHANNELS
    c = HGRN_CHUNK
    n_chunks = ts // c
    j = pl.program_id(1)

    @pl.when(j == 0)
    def _():
        state[...] = jnp.zeros_like(state)
        vblk[...] = jnp.zeros_like(vblk)
        ubuf[0:8, :] = jnp.zeros((8, cc), F32)

    u = proj_ref[0, :, cc:2 * cc] * proj_ref[0, :, 2 * cc:3 * cc]
    ubuf[8:8 + ts, :] = u
    y = (convw_ref[0:1, :] * ubuf[6:6 + ts, :]
         + convw_ref[1:2, :] * ubuf[7:7 + ts, :]
         + convw_ref[2:3, :] * u)
    out_ref[0, :, 0:cc] = (proj_ref[0, :, 0:cc] * y).astype(out_ref.dtype)
    ubuf[0:8, :] = u[ts - 8:ts, :]

    lbp = lb_ref[...]
    e = jnp.exp(lbp - jnp.max(lbp, axis=0, keepdims=True))
    sm = e / jnp.sum(e, axis=0, keepdims=True)
    cum0 = sm[0:1, :]
    cum1 = cum0 + sm[1:2, :]
    lb = cum1 - cum0
    log_lb = jnp.log(lb)
    log_1mlb = jnp.log1p(-lb)
    q0 = 3 * cc
    z = proj_ref[0, :, q0 + HGRN_WIDTH:q0 + 2 * HGRN_WIDTH]
    ez = jnp.exp(-jnp.abs(z))
    log_sig = jnp.minimum(z, 0.0) - jnp.log1p(ez)
    b = log_1mlb + log_sig
    lf = jnp.maximum(log_lb, b) + jnp.log1p(jnp.exp(-jnp.abs(log_lb - b)))
    inv = 1.0 / (1.0 + ez)
    kk = (1.0 - lb) * jnp.where(z >= 0.0, ez * inv, inv)

    lf_hi = lf.astype(BF16)
    lf_lo = (lf - lf_hi.astype(F32)).astype(BF16)
    sums = (jnp.dot(cum_ref[...], lf_hi, preferred_element_type=F32)
            + jnp.dot(cum_ref[...], lf_lo, preferred_element_type=F32))
    g = sums[:ts]
    gl = sums[ts:]
    tn = (((0,), (0,)), ((), ()))
    decay_col = jnp.exp(lax.dot_general(lf_hi, ind_ref[...], tn, preferred_element_type=F32)
                        + lax.dot_general(lf_lo, ind_ref[...], tn, preferred_element_type=F32))

    half = 0.5 * gl
    q = proj_ref[0, :, q0:q0 + HGRN_WIDTH]
    qd = (q * jnp.exp(g - half)).astype(BF16)
    kd = (kk * jnp.exp(half - g)).astype(BF16)
    qg = (q * jnp.exp(g)).astype(BF16)
    kd2 = (kk * jnp.exp(gl - g)).astype(BF16)
    i_in = proj_ref[0, :, q0 + 2 * HGRN_WIDTH:q0 + 3 * HGRN_WIDTH]
    v = (i_in * (1.0 / (1.0 + jnp.exp(-i_in)))).astype(BF16)

    row = lax.broadcasted_iota(jnp.int32, (ts, ts), 0)
    col = lax.broadcasted_iota(jnp.int32, (ts, ts), 1)
    intra = (row // c == col // c) & (row >= col)
    hnorm = hnorm_ref[...]

    for hd in range(HGRN_HEADS):
        lanes = slice(hd * LANES, (hd + 1) * LANES)
        vh = v[:, lanes]
        a = lax.dot_general(qd[:, lanes], kd[:, lanes], (((1,), (1,)), ((), ())), preferred_element_type=F32)
        o = jnp.dot(jnp.where(intra, a, 0.0).astype(BF16), vh, preferred_element_type=F32)
        for ci in range(n_chunks):
            vblk[hd, ci * c:(ci + 1) * c, ci * LANES:(ci + 1) * LANES] = vh[ci * c:(ci + 1) * c]
        upd = lax.dot_general(kd2[:, lanes], vblk[hd], tn, preferred_element_type=F32)
        st = state[hd]
        inter = []
        for ci in range(n_chunks):
            inter.append(jnp.dot(qg[ci * c:(ci + 1) * c, lanes], st.astype(BF16), preferred_element_type=F32))
            st = st * decay_col[lanes, ci:ci + 1] + upd[:, ci * LANES:(ci + 1) * LANES]
        state[hd] = st
        o = o + jnp.concatenate(inter, axis=0)
        o = _rmsnorm(o, hnorm)
        gh = proj_ref[0, :, q0 + 3 * HGRN_WIDTH + hd * LANES:q0 + 3 * HGRN_WIDTH + (hd + 1) * LANES]
        o = o * (gh * (1.0 / (1.0 + jnp.exp(-gh))))
        out_ref[0, :, cc + hd * LANES:cc + (hd + 1) * LANES] = o.astype(out_ref.dtype)


def _mixer(proj, conv_w, hgrn_lb, hgrn_norm, batch, seq):
    ts = MIX_TILE
    cum, ind = _mixer_consts()
    return pl.pallas_call(
        _mixer_kernel,
        grid=(batch, seq // ts),
        in_specs=[pl.BlockSpec((1, ts, REC_IN_WIDTH), lambda b, j: (b, j, 0)),
                  _const_spec((CONV_WIDTH, CONV_CHANNELS)),
                  _const_spec((3, HGRN_WIDTH)),
                  _const_spec((1, HGRN_HEAD_DIM)),
                  _const_spec(cum.shape), _const_spec(ind.shape)],
        out_specs=pl.BlockSpec((1, ts, D_MODEL), lambda b, j: (b, j, 0)),
        out_shape=jax.ShapeDtypeStruct((batch, seq, D_MODEL), BF16),
        scratch_shapes=[pltpu.VMEM((ts + 8, CONV_CHANNELS), F32),
                        pltpu.VMEM((HGRN_HEADS, HGRN_HEAD_DIM, HGRN_HEAD_DIM), F32),
                        pltpu.VMEM((HGRN_HEADS, ts, (ts // HGRN_CHUNK) * LANES), BF16)],
        compiler_params=_params(2),
        name="conv_hgrn_mixer",
    )(proj, conv_w, hgrn_lb, hgrn_norm, cum, ind)


N_RES = DILATIONS[-1]


def _segments(d):
    count = N_RES // d
    return count, ATTN_BLOCK // count


def _attn_consts():
    blk = ATTN_BLOCK
    e = np.arange(blk)
    none = np.full((blk, blk), MASK_VALUE)
    kaug = []
    for d in DILATIONS:
        count, rows = _segments(d)
        pos = (e % rows) * count + e // rows
        kpos, qpos = pos[:, None], pos[None, :]
        prev_ok = np.where(kpos >= qpos, 0.0, MASK_VALUE)
        cur_ok = np.where(kpos <= qpos, 0.0, MASK_VALUE)
        kaug.append(np.stack([np.concatenate([none, cur_ok]), np.concatenate([prev_ok, cur_ok])]))
    qaug = np.concatenate([np.eye(blk), np.eye(blk)])
    return jnp.asarray(qaug, BF16), jnp.asarray(np.stack(kaug), BF16)


def _attn_kernel(q_ref, k_ref, v_ref, cos_ref, sin_ref, qaug_ref, kaug_ref, o_ref,
                 q0s, q1s, ks, vs, s_buf, m_s, l_s, acc_s, *, seq):
    blk = ATTN_BLOCK
    grp = ATTN_GROUP
    n_groups = (seq // blk) // grp
    assert n_groups % 2 == 0 and seq // N_RES == blk
    lane = lax.broadcasted_iota(jnp.int32, (blk, LANES), 1)
    first_head = lane < ATTN_HEAD_DIM
    first_half = (lane % ATTN_HEAD_DIM) < (ATTN_HEAD_DIM // 2)

    def rope_body(r, carry):
        rows = pl.ds(r, blk, stride=N_RES)
        cos = cos_ref[r]
        sin = sin_ref[r]

        def rot(t):
            partner = jnp.where(first_half,
                                pltpu.roll(t, LANES - ATTN_HEAD_DIM // 2, 1),
                                pltpu.roll(t, ATTN_HEAD_DIM // 2, 1))
            return t * cos + partner * sin

        qr = rot(q_ref[0, rows, :]) * (ATTN_HEAD_DIM ** -0.5)
        q0s[r] = jnp.where(first_head, qr, 0.0)
        q1s[r] = jnp.where(first_head, 0.0, qr)
        ks[r] = rot(k_ref[0, rows, :])
        vs[r] = v_ref[0, rows, :]
        return carry

    lax.fori_loop(0, N_RES, rope_body, 0)

    def pick(top, bot):
        return jnp.where(first_head, top, bot)

    def block_id(d, j, u):
        n_blocks = seq // (d * blk)
        if n_blocks == 1:
            return j * grp + u, 0, None, None
        gpr = n_blocks // grp
        res = 0 if d == 1 else j // gpr
        n = (j % gpr) * grp + u if gpr > 1 else u
        if u > 0:
            return res, n, n - 1, 1
        if isinstance(n, int):
            return res, n, max(n - 1, 0), min(n, 1)
        return res, n, jnp.maximum(n - 1, 0), jnp.minimum(n, 1)

    def seg_rows(d, n):
        _, rows = _segments(d)
        start = n * rows
        return pl.ds(start if isinstance(start, int) else pl.multiple_of(start, rows), rows)

    def load_block(ref, d, res, n):
        count, _ = _segments(d)
        return jnp.concatenate([ref[res + d * i, seg_rows(d, n), :] for i in range(count)], axis=0)

    def store_block(ref, branch, d, res, n, val):
        count, rows = _segments(d)
        for i in range(count):
            ref[branch, res + d * i, seg_rows(d, n), :] = val[i * rows:(i + 1) * rows]

    def scores(branch, d, j, u):
        res, n, n_prev, has_prev = block_id(d, j, u)
        q2 = jnp.concatenate([load_block(q0s, d, res, n), load_block(q1s, d, res, n)], axis=0).astype(BF16)
        q_ext = jnp.concatenate([q2, qaug_ref[...]], axis=1)
        kc = load_block(ks, d, res, n)
        if n_prev is None:
            kk = kc.astype(BF16)
            aug = kaug_ref[branch, 1, blk:2 * blk, :]
        else:
            kk = jnp.concatenate([load_block(ks, d, res, n_prev), kc], axis=0).astype(BF16)
            aug = kaug_ref[branch, has_prev]
        k_ext = jnp.concatenate([kk, aug], axis=1)
        s = lax.dot_general(q_ext, k_ext, (((1,), (1,)), ((), ())), preferred_element_type=F32)
        s_buf[j % 2, u, :, 0:s.shape[1]] = s

    def softmax_pv(branch, d, j, u):
        res, n, n_prev, _ = block_id(d, j, u)
        vc = load_block(vs, d, res, n)
        if n_prev is None:
            vv = vc.astype(BF16)
        else:
            vv = jnp.concatenate([load_block(vs, d, res, n_prev), vc], axis=0).astype(BF16)
        s = s_buf[j % 2, u, :, 0:vv.shape[0]]
        m = jnp.max(s, axis=-1, keepdims=True)
        p = jnp.exp(s - m).astype(BF16)
        v_ext = jnp.concatenate([vv, jnp.ones_like(vv)], axis=1)
        out = jnp.dot(p, v_ext, preferred_element_type=F32)
        mb = jnp.broadcast_to(m, (2 * blk, LANES))
        store_block(m_s, branch, d, res, n, pick(mb[:blk], mb[blk:]))
        store_block(l_s, branch, d, res, n, pick(out[:blk, LANES:], out[blk:, LANES:]))
        store_block(acc_s, branch, d, res, n, pick(out[:blk, :LANES], out[blk:, :LANES]))

    for u in range(grp):
        scores(0, DILATIONS[0], 0, u)
    for branch, d in enumerate(DILATIONS):
        def group_body(j, carry, branch=branch, d=d):
            for u in range(grp):
                softmax_pv(branch, d, j, u)
            for u in range(grp):
                scores(branch, d, j + 1, u)
            return carry

        lax.fori_loop(0, n_groups - 1, group_body, 0)
        for u in range(grp):
            softmax_pv(branch, d, n_groups - 1, u)
        if branch + 1 < len(DILATIONS):
            for u in range(grp):
                scores(branch + 1, DILATIONS[branch + 1], 0, u)

    def merge_body(r, carry):
        ms = [m_s[i, r] for i in range(len(DILATIONS))]
        m_all = functools.reduce(jnp.maximum, ms)
        ws = [jnp.exp(mi - m_all) for mi in ms]
        num = sum(w * acc_s[i, r] for i, w in enumerate(ws))
        den = sum(w * l_s[i, r] for i, w in enumerate(ws))
        o_ref[0, pl.ds(r, blk, stride=N_RES), :] = num / den
        return carry

    lax.fori_loop(0, N_RES, merge_body, 0)


def _attention(qkv, cos_t, sin_t, batch, seq):
    n_pairs = ATTN_HEADS * ATTN_HEAD_DIM // LANES
    blk_spec = lambda off: pl.BlockSpec((1, seq, LANES), lambda b, h: (b, 0, off + h))
    nb = len(DILATIONS)
    qaug, kaug = _attn_consts()
    res_major = (N_RES, seq // N_RES, LANES)
    return pl.pallas_call(
        functools.partial(_attn_kernel, seq=seq),
        grid=(batch, n_pairs),
        in_specs=[blk_spec(0), blk_spec(n_pairs), blk_spec(2 * n_pairs),
                  _const_spec(res_major), _const_spec(res_major),
                  _const_spec(qaug.shape), _const_spec(kaug.shape)],
        out_specs=pl.BlockSpec((1, seq, LANES), lambda b, h: (b, 0, h)),
        out_shape=jax.ShapeDtypeStruct((batch, seq, D_MODEL), F32),
        scratch_shapes=[pltpu.VMEM(res_major, F32),
                        pltpu.VMEM(res_major, F32),
                        pltpu.VMEM(res_major, F32),
                        pltpu.VMEM(res_major, F32),
                        pltpu.VMEM((2, ATTN_GROUP, 2 * ATTN_BLOCK, 2 * ATTN_BLOCK), F32),
                        pltpu.VMEM((nb,) + res_major, F32),
                        pltpu.VMEM((nb,) + res_major, F32),
                        pltpu.VMEM((nb,) + res_major, F32)],
        compiler_params=_params(2),
        name="dilated_attention",
    )(qkv, qkv, qkv, cos_t, sin_t, qaug, kaug)


def _oproj_ffn_kernel(x_ref, a_ref, wo_ref, g_ref, win_ref, wout_ref, gfin_ref, o_ref, *, final_norm):
    x1 = x_ref[...] + jnp.dot(a_ref[...].astype(BF16), wo_ref[...], preferred_element_type=F32)
    h = _rmsnorm(x1, g_ref[...]).astype(BF16)
    acc = x1
    n_split = 2
    fc = D_FF // n_split
    for c in range(n_split):
        gate = jnp.dot(h, win_ref[:, c * fc:(c + 1) * fc], preferred_element_type=F32)
        up = jnp.dot(h, win_ref[:, D_FF + c * fc:D_FF + (c + 1) * fc], preferred_element_type=F32)
        act = (gate * (1.0 / (1.0 + jnp.exp(-gate))) * up).astype(BF16)
        acc = acc + jnp.dot(act, wout_ref[c * fc:(c + 1) * fc, :], preferred_element_type=F32)
    if final_norm:
        acc = _rmsnorm(acc, gfin_ref[...])
    o_ref[...] = acc


def _oproj_ffn(x, a, wo, g, w_in, w_out, g_fin, final_norm):
    m, d = x.shape
    row_spec = pl.BlockSpec((ROW_TILE, d), lambda i: (i, 0))
    return pl.pallas_call(
        functools.partial(_oproj_ffn_kernel, final_norm=final_norm),
        grid=(m // ROW_TILE,),
        in_specs=[row_spec, row_spec,
                  _const_spec(wo.shape), _const_spec((1, d)),
                  _const_spec(w_in.shape), _const_spec(w_out.shape), _const_spec((1, d))],
        out_specs=row_spec,
        out_shape=jax.ShapeDtypeStruct((m, d), F32),
        compiler_params=_params(1),
        name="oproj_ffn",
    )(x, a, wo, g, w_in, w_out, g_fin)


def _rope_tables(seq):
    half = ATTN_HEAD_DIM // 2
    freqs = ROPE_THETA ** (-jnp.arange(half, dtype=F32) / half)
    ang = jnp.arange(seq, dtype=F32)[:, None] * freqs[None, :]
    cos, sin = jnp.cos(ang), jnp.sin(ang)
    reps = LANES // ATTN_HEAD_DIM
    cos_t = jnp.tile(jnp.concatenate([cos, cos], axis=1), (1, reps))
    sin_t = jnp.tile(jnp.concatenate([-sin, sin], axis=1), (1, reps))
    res_major = lambda t: t.reshape(seq // N_RES, N_RES, LANES).transpose(1, 0, 2)
    return res_major(cos_t), res_major(sin_t)


def kernel(x, mix_norm, ffn_norm, w_in_rec, conv_w, hgrn_lb, hgrn_norm, w_out_rec,
           w_qkv_attn, w_o_attn, w_ffn_in, w_ffn_out, final_norm):
    batch, seq, d = x.shape
    m = batch * seq
    xf = x.reshape(m, d)
    row = lambda v: v.reshape(1, -1)

    proj = _norm_matmul(xf, row(mix_norm[0]), w_in_rec[0].astype(BF16), F32)
    mix = _mixer(proj.reshape(batch, seq, REC_IN_WIDTH), conv_w[0], hgrn_lb, row(hgrn_norm[0]), batch, seq)
    xf = _oproj_ffn(xf, mix.reshape(m, d), w_out_rec[0].astype(BF16), row(ffn_norm[0]),
                    w_ffn_in[0].astype(BF16), w_ffn_out[0].astype(BF16), row(final_norm), False)

    qkv = _norm_matmul(xf, row(mix_norm[1]), w_qkv_attn[0].astype(BF16), F32)
    cos_t, sin_t = _rope_tables(seq)
    att = _attention(qkv.reshape(batch, seq, 3 * D_MODEL), cos_t, sin_t, batch, seq)
    xf = _oproj_ffn(xf, att.reshape(m, d), w_o_attn[0].astype(BF16), row(ffn_norm[1]),
                    w_ffn_in[1].astype(BF16), w_ffn_out[1].astype(BF16), row(final_norm), True)
    return xf.reshape(batch, seq, d)
```

```python
import functools

import jax
import jax.numpy as jnp
import numpy as np
from jax import lax
from jax.experimental import pallas as pl
from jax.experimental.pallas import tpu as pltpu

D_MODEL = 1024
CONV_WIDTH = 3
CONV_CHANNELS = 512
HGRN_HEAD_DIM = 128
HGRN_HEADS = 4
HGRN_WIDTH = 512
ATTN_HEAD_DIM = 64
ATTN_HEADS = 16
ATTN_BLOCK = 128
DILATIONS = (1, 4, 16)
ROPE_THETA = 10000.0
D_FF = 2816
RMS_EPS = 1e-6
REC_IN_WIDTH = 3 * CONV_CHANNELS + 4 * HGRN_WIDTH

LANES = 128
VMEM_LIMIT_BYTES = 56 * 1024 * 1024
ROW_TILE = 512
MIX_TILE = 256
HGRN_CHUNK = 32
MASK_VALUE = -1e30
ATTN_GROUP = 4

F32 = jnp.float32
BF16 = jnp.bfloat16


def _rmsnorm(x, g):
    return x * lax.rsqrt(jnp.mean(x * x, axis=-1, keepdims=True) + RMS_EPS) * g


def _const_spec(shape):
    zeros = (0,) * len(shape)
    return pl.BlockSpec(shape, lambda *_: zeros, pipeline_mode=pl.Buffered(1))


def _params(n_grid):
    return pltpu.CompilerParams(
        dimension_semantics=("arbitrary",) * n_grid, vmem_limit_bytes=VMEM_LIMIT_BYTES)


def _norm_matmul_kernel(x_ref, g_ref, w_ref, o_ref):
    h = _rmsnorm(x_ref[...], g_ref[...]).astype(BF16)
    o_ref[...] = jnp.dot(h, w_ref[...], preferred_element_type=F32).astype(o_ref.dtype)


def _norm_matmul(x, g, w, out_dtype):
    m, d = x.shape
    n = w.shape[1]
    return pl.pallas_call(
        _norm_matmul_kernel,
        grid=(m // ROW_TILE,),
        in_specs=[pl.BlockSpec((ROW_TILE, d), lambda i: (i, 0)),
                  _const_spec((1, d)),
                  _const_spec((d, n))],
        out_specs=pl.BlockSpec((ROW_TILE, n), lambda i: (i, 0)),
        out_shape=jax.ShapeDtypeStruct((m, n), out_dtype),
        compiler_params=_params(1),
        name="norm_matmul",
    )(x, g, w)


def _mixer_consts():
    ts, c = MIX_TILE, HGRN_CHUNK
    t = np.arange(ts)
    same = (t[:, None] // c) == (t[None, :] // c)
    tril = same & (t[:, None] >= t[None, :])
    cum = np.concatenate([tril, same]).astype(np.float32)
    ind = (t[:, None] // c == np.arange(LANES)[None, :]).astype(np.float32)
    return jnp.asarray(cum, BF16), jnp.asarray(ind, BF16)


def _mixer_kernel(proj_ref, convw_ref, lb_ref, hnorm_ref, cum_ref, ind_ref, out_ref,
                  ubuf, state, vblk):
    ts = MIX_TILE
    cc = CONV_CHANNELS
    c = HGRN_CHUNK
    n_chunks = ts // c
    j = pl.program_id(1)

    @pl.when(j == 0)
    def _():
        state[...] = jnp.zeros_like(state)
        vblk[...] = jnp.zeros_like(vblk)
        ubuf[0:8, :] = jnp.zeros((8, cc), F32)

    u = proj_ref[0, :, cc:2 * cc] * proj_ref[0, :, 2 * cc:3 * cc]
    ubuf[8:8 + ts, :] = u
    y = (convw_ref[0:1, :] * ubuf[6:6 + ts, :]
         + convw_ref[1:2, :] * ubuf[7:7 + ts, :]
         + convw_ref[2:3, :] * u)
    out_ref[0, :, 0:cc] = (proj_ref[0, :, 0:cc] * y).astype(out_ref.dtype)
    ubuf[0:8, :] = u[ts - 8:ts, :]

    lbp = lb_ref[...]
    e = jnp.exp(lbp - jnp.max(lbp, axis=0, keepdims=True))
    sm = e / jnp.sum(e, axis=0, keepdims=True)
    cum0 = sm[0:1, :]
    cum1 = cum0 + sm[1:2, :]
    lb = cum1 - cum0
    log_lb = jnp.log(lb)
    log_1mlb = jnp.log1p(-lb)
    q0 = 3 * cc
    z = proj_ref[0, :, q0 + HGRN_WIDTH:q0 + 2 * HGRN_WIDTH]
    ez = jnp.exp(-jnp.abs(z))
    log_sig = jnp.minimum(z, 0.0) - jnp.log1p(ez)
    b = log_1mlb + log_sig
    lf = jnp.maximum(log_lb, b) + jnp.log1p(jnp.exp(-jnp.abs(log_lb - b)))
    inv = 1.0 / (1.0 + ez)
    kk = (1.0 - lb) * jnp.where(z >= 0.0, ez * inv, inv)

    lf_hi = lf.astype(BF16)
    lf_lo = (lf - lf_hi.astype(F32)).astype(BF16)
    sums = (jnp.dot(cum_ref[...], lf_hi, preferred_element_type=F32)
            + jnp.dot(cum_ref[...], lf_lo, preferred_element_type=F32))
    g = sums[:ts]
    gl = sums[ts:]
    tn = (((0,), (0,)), ((), ()))
    decay_col = jnp.exp(lax.dot_general(lf_hi, ind_ref[...], tn, preferred_element_type=F32)
                        + lax.dot_general(lf_lo, ind_ref[...], tn, preferred_element_type=F32))

    half = 0.5 * gl
    q = proj_ref[0, :, q0:q0 + HGRN_WIDTH]
    qd = (q * jnp.exp(g - half)).astype(BF16)
    kd = (kk * jnp.exp(half - g)).astype(BF16)
    qg = (q * jnp.exp(g)).astype(BF16)
    kd2 = (kk * jnp.exp(gl - g)).astype(BF16)
    i_in = proj_ref[0, :, q0 + 2 * HGRN_WIDTH:q0 + 3 * HGRN_WIDTH]
    v = (i_in * (1.0 / (1.0 + jnp.exp(-i_in)))).astype(BF16)

    row = lax.broadcasted_iota(jnp.int32, (ts, ts), 0)
    col = lax.broadcasted_iota(jnp.int32, (ts, ts), 1)
    intra = (row // c == col // c) & (row >= col)
    hnorm = hnorm_ref[...]

    for hd in range(HGRN_HEADS):
        lanes = slice(hd * LANES, (hd + 1) * LANES)
        vh = v[:, lanes]
        a = lax.dot_general(qd[:, lanes], kd[:, lanes], (((1,), (1,)), ((), ())), preferred_element_type=F32)
        o = jnp.dot(jnp.where(intra, a, 0.0).astype(BF16), vh, preferred_element_type=F32)
        for ci in range(n_chunks):
            vblk[hd, ci * c:(ci + 1) * c, ci * LANES:(ci + 1) * LANES] = vh[ci * c:(ci + 1) * c]
        upd = lax.dot_general(kd2[:, lanes], vblk[hd], tn, preferred_element_type=F32)
        st = state[hd]
        inter = []
        for ci in range(n_chunks):
            inter.append(jnp.dot(qg[ci * c:(ci + 1) * c, lanes], st.astype(BF16), preferred_element_type=F32))
            st = st * decay_col[lanes, ci:ci + 1] + upd[:, ci * LANES:(ci + 1) * LANES]
        state[hd] = st
        o = o + jnp.concatenate(inter, axis=0)
        o = _rmsnorm(o, hnorm)
        gh = proj_ref[0, :, q0 + 3 * HGRN_WIDTH + hd * LANES:q0 + 3 * HGRN_WIDTH + (hd + 1) * LANES]
        o = o * (gh * (1.0 / (1.0 + jnp.exp(-gh))))
        out_ref[0, :, cc + hd * LANES:cc + (hd + 1) * LANES] = o.astype(out_ref.dtype)


def _mixer(proj, conv_w, hgrn_lb, hgrn_norm, batch, seq):
    ts = MIX_TILE
    cum, ind = _mixer_consts()
    return pl.pallas_call(
        _mixer_kernel,
        grid=(batch, seq // ts),
        in_specs=[pl.BlockSpec((1, ts, REC_IN_WIDTH), lambda b, j: (b, j, 0)),
                  _const_spec((CONV_WIDTH, CONV_CHANNELS)),
                  _const_spec((3, HGRN_WIDTH)),
                  _const_spec((1, HGRN_HEAD_DIM)),
                  _const_spec(cum.shape), _const_spec(ind.shape)],
        out_specs=pl.BlockSpec((1, ts, D_MODEL), lambda b, j: (b, j, 0)),
        out_shape=jax.ShapeDtypeStruct((batch, seq, D_MODEL), BF16),
        scratch_shapes=[pltpu.VMEM((ts + 8, CONV_CHANNELS), F32),
                        pltpu.VMEM((HGRN_HEADS, HGRN_HEAD_DIM, HGRN_HEAD_DIM), F32),
                        pltpu.VMEM((HGRN_HEADS, ts, (ts // HGRN_CHUNK) * LANES), BF16)],
        compiler_params=_params(2),
        name="conv_hgrn_mixer",
    )(proj, conv_w, hgrn_lb, hgrn_norm, cum, ind)


N_RES = DILATIONS[-1]


def _segments(d):
    count = N_RES // d
    return count, ATTN_BLOCK // count


def _attn_consts():
    blk = ATTN_BLOCK
    e = np.arange(blk)
    none = np.full((blk, blk), MASK_VALUE)
    kaug = []
    for d in DILATIONS:
        count, rows = _segments(d)
        pos = (e % rows) * count + e // rows
        kpos, qpos = pos[:, None], pos[None, :]
        prev_ok = np.where(kpos >= qpos, 0.0, MASK_VALUE)
        cur_ok = np.where(kpos <= qpos, 0.0, MASK_VALUE)
        kaug.append(np.stack([np.concatenate([none, cur_ok]), np.concatenate([prev_ok, cur_ok])]))
    qaug = np.concatenate([np.eye(blk), np.eye(blk)])
    return jnp.asarray(qaug, BF16), jnp.asarray(np.stack(kaug), BF16)


def _qkv_rope_kernel(*refs):
    x_refs = refs[:N_RES]
    g_ref, w_ref, cos_ref, sin_ref, o_ref = refs[N_RES:]
    rows = ROW_TILE // N_RES
    x = jnp.concatenate([r[...] for r in x_refs], axis=0)
    h = _rmsnorm(x, g_ref[...]).astype(BF16)
    cos = cos_ref[...].reshape(ROW_TILE, LANES)
    sin = sin_ref[...].reshape(ROW_TILE, LANES)
    lane = lax.broadcasted_iota(jnp.int32, (ROW_TILE, LANES), 1)
    first_half = (lane % ATTN_HEAD_DIM) < (ATTN_HEAD_DIM // 2)
    width = ATTN_HEADS * ATTN_HEAD_DIM
    for part in range(3):
        y = jnp.dot(h, w_ref[:, part * width:(part + 1) * width], preferred_element_type=F32)
        if part < 2:
            cols = []
            for cb in range(width // LANES):
                t = y[:, cb * LANES:(cb + 1) * LANES]
                partner = jnp.where(first_half,
                                    pltpu.roll(t, LANES - ATTN_HEAD_DIM // 2, 1),
                                    pltpu.roll(t, ATTN_HEAD_DIM // 2, 1))
                t = t * cos + partner * sin
                cols.append(t * (ATTN_HEAD_DIM ** -0.5) if part == 0 else t)
            y = jnp.concatenate(cols, axis=1)
        o_ref[:, :, part * width:(part + 1) * width] = y.reshape(N_RES, rows, width)


def _qkv_rope(x, g, w, cos_t, sin_t, batch, seq):
    d = x.shape[-1]
    n = w.shape[1]
    rows = ROW_TILE // N_RES
    x_cols = x.reshape(batch, seq // N_RES, N_RES * d)
    x_specs = [pl.BlockSpec((None, rows, d), functools.partial(lambda b, c, r: (b, c, r), r=r)) for r in range(N_RES)]
    table_spec = pl.BlockSpec((N_RES, rows, LANES), lambda b, c: (0, c, 0))
    return pl.pallas_call(
        _qkv_rope_kernel,
        grid=(batch, seq // ROW_TILE),
        in_specs=x_specs + [_const_spec((1, d)), _const_spec((d, n)), table_spec, table_spec],
        out_specs=pl.BlockSpec((None, N_RES, rows, n), lambda b, c: (b, 0, c, 0)),
        out_shape=jax.ShapeDtypeStruct((batch, N_RES, seq // N_RES, n), F32),
        compiler_params=_params(2),
        name="qkv_rope",
    )(*([x_cols] * N_RES), g, w, cos_t, sin_t)


def _attn_kernel(q_ref, k_ref, v_ref, qaug_ref, kaug_ref, o_ref,
                 s_buf, m_s, l_s, acc_s, *, seq):
    blk = ATTN_BLOCK
    grp = ATTN_GROUP
    n_groups = (seq // blk) // grp
    assert n_groups % 2 == 0 and seq // N_RES == blk
    lane = lax.broadcasted_iota(jnp.int32, (blk, LANES), 1)
    first_head = lane < ATTN_HEAD_DIM

    def pick(top, bot):
        return jnp.where(first_head, top, bot)

    def block_id(d, j, u):
        n_blocks = seq // (d * blk)
        if n_blocks == 1:
            return j * grp + u, 0, None, None
        gpr = n_blocks // grp
        res = 0 if d == 1 else j // gpr
        n = (j % gpr) * grp + u if gpr > 1 else u
        if u > 0:
            return res, n, n - 1, 1
        if isinstance(n, int):
            return res, n, max(n - 1, 0), min(n, 1)
        return res, n, jnp.maximum(n - 1, 0), jnp.minimum(n, 1)

    def seg_rows(d, n):
        _, rows = _segments(d)
        start = n * rows
        return pl.ds(start if isinstance(start, int) else pl.multiple_of(start, rows), rows)

    def load_block(ref, d, res, n):
        count, _ = _segments(d)
        return jnp.concatenate([ref[res + d * i, seg_rows(d, n), :] for i in range(count)], axis=0)

    def store_block(ref, branch, d, res, n, val):
        count, rows = _segments(d)
        for i in range(count):
            ref[branch, res + d * i, seg_rows(d, n), :] = val[i * rows:(i + 1) * rows]

    def scores(branch, d, j, u):
        res, n, n_prev, has_prev = block_id(d, j, u)
        qb = load_block(q_ref, d, res, n)
        q2 = jnp.concatenate([jnp.where(first_head, qb, 0.0), jnp.where(first_head, 0.0, qb)], axis=0).astype(BF16)
        q_ext = jnp.concatenate([q2, qaug_ref[...]], axis=1)
        kc = load_block(k_ref, d, res, n)
        if n_prev is None:
            kk = kc.astype(BF16)
            aug = kaug_ref[branch, 1, blk:2 * blk, :]
        else:
            kk = jnp.concatenate([load_block(k_ref, d, res, n_prev), kc], axis=0).astype(BF16)
            aug = kaug_ref[branch, has_prev]
        k_ext = jnp.concatenate([kk, aug], axis=1)
        s = lax.dot_general(q_ext, k_ext, (((1,), (1,)), ((), ())), preferred_element_type=F32)
        s_buf[j % 2, u, :, 0:s.shape[1]] = s

    def softmax_pv(branch, d, j, u):
        res, n, n_prev, _ = block_id(d, j, u)
        vc = load_block(v_ref, d, res, n)
        if n_prev is None:
            vv = vc.astype(BF16)
        else:
            vv = jnp.concatenate([load_block(v_ref, d, res, n_prev), vc], axis=0).astype(BF16)
        s = s_buf[j % 2, u, :, 0:vv.shape[0]]
        m = jnp.max(s, axis=-1, keepdims=True)
        p = jnp.exp(s - m).astype(BF16)
        v_ext = jnp.concatenate([vv, jnp.ones_like(vv)], axis=1)
        out = jnp.dot(p, v_ext, preferred_element_type=F32)
        mb = jnp.broadcast_to(m, (2 * blk, LANES))
        store_block(m_s, branch, d, res, n, pick(mb[:blk], mb[blk:]))
        store_block(l_s, branch, d, res, n, pick(out[:blk, LANES:], out[blk:, LANES:]))
        store_block(acc_s, branch, d, res, n, pick(out[:blk, :LANES], out[blk:, :LANES]))

    for u in range(grp):
        scores(0, DILATIONS[0], 0, u)
    for branch, d in enumerate(DILATIONS):
        def group_body(j, carry, branch=branch, d=d):
            for u in range(grp):
                softmax_pv(branch, d, j, u)
            for u in range(grp):
                scores(branch, d, j + 1, u)
            return carry

        lax.fori_loop(0, n_groups - 1, group_body, 0)
        for u in range(grp):
            softmax_pv(branch, d, n_groups - 1, u)
        if branch + 1 < len(DILATIONS):
            for u in range(grp):
                scores(branch + 1, DILATIONS[branch + 1], 0, u)

    def merge_body(r, carry):
        ms = [m_s[i, r] for i in range(len(DILATIONS))]
        m_all = functools.reduce(jnp.maximum, ms)
        ws = [jnp.exp(mi - m_all) for mi in ms]
        num = sum(w * acc_s[i, r] for i, w in enumerate(ws))
        den = sum(w * l_s[i, r] for i, w in enumerate(ws))
        o_ref[0, pl.ds(r, blk, stride=N_RES), :] = num / den
        return carry

    lax.fori_loop(0, N_RES, merge_body, 0)


def _attention(qkv, batch, seq):
    n_pairs = ATTN_HEADS * ATTN_HEAD_DIM // LANES
    res_major = (N_RES, seq // N_RES, LANES)
    blk_spec = lambda off: pl.BlockSpec((None,) + res_major, lambda b, h: (b, 0, 0, off + h))
    nb = len(DILATIONS)
    qaug, kaug = _attn_consts()
    return pl.pallas_call(
        functools.partial(_attn_kernel, seq=seq),
        grid=(batch, n_pairs),
        in_specs=[blk_spec(0), blk_spec(n_pairs), blk_spec(2 * n_pairs),
                  _const_spec(qaug.shape), _const_spec(kaug.shape)],
        out_specs=pl.BlockSpec((1, seq, LANES), lambda b, h: (b, 0, h)),
        out_shape=jax.ShapeDtypeStruct((batch, seq, D_MODEL), F32),
        scratch_shapes=[pltpu.VMEM((2, ATTN_GROUP, 2 * ATTN_BLOCK, 2 * ATTN_BLOCK), F32),
                        pltpu.VMEM((nb,) + res_major, F32),
                        pltpu.VMEM((nb,) + res_major, F32),
                        pltpu.VMEM((nb,) + res_major, F32)],
        compiler_params=_params(2),
        name="dilated_attention",
    )(qkv, qkv, qkv, qaug, kaug)


def _oproj_ffn_kernel(x_ref, a_ref, wo_ref, g_ref, win_ref, wout_ref, gfin_ref, o_ref, *, final_norm):
    x1 = x_ref[...] + jnp.dot(a_ref[...].astype(BF16), wo_ref[...], preferred_element_type=F32)
    h = _rmsnorm(x1, g_ref[...]).astype(BF16)
    acc = x1
    n_split = 2
    fc = D_FF // n_split
    for c in range(n_split):
        gate = jnp.dot(h, win_ref[:, c * fc:(c + 1) * fc], preferred_element_type=F32)
        up = jnp.dot(h, win_ref[:, D_FF + c * fc:D_FF + (c + 1) * fc], preferred_element_type=F32)
        act = (gate * (1.0 / (1.0 + jnp.exp(-gate))) * up).astype(BF16)
        acc = acc + jnp.dot(act, wout_ref[c * fc:(c + 1) * fc, :], preferred_element_type=F32)
    if final_norm:
        acc = _rmsnorm(acc, gfin_ref[...])
    o_ref[...] = acc


def _oproj_ffn(x, a, wo, g, w_in, w_out, g_fin, final_norm):
    m, d = x.shape
    row_spec = pl.BlockSpec((ROW_TILE, d), lambda i: (i, 0))
    return pl.pallas_call(
        functools.partial(_oproj_ffn_kernel, final_norm=final_norm),
        grid=(m // ROW_TILE,),
        in_specs=[row_spec, row_spec,
                  _const_spec(wo.shape), _const_spec((1, d)),
                  _const_spec(w_in.shape), _const_spec(w_out.shape), _const_spec((1, d))],
        out_specs=row_spec,
        out_shape=jax.ShapeDtypeStruct((m, d), F32),
        compiler_params=_params(1),
        name="oproj_ffn",
    )(x, a, wo, g, w_in, w_out, g_fin)


def _rope_tables(seq):
    half = ATTN_HEAD_DIM // 2
    freqs = ROPE_THETA ** (-jnp.arange(half, dtype=F32) / half)
    ang = jnp.arange(seq, dtype=F32)[:, None] * freqs[None, :]
    cos, sin = jnp.cos(ang), jnp.sin(ang)
    reps = LANES // ATTN_HEAD_DIM
    cos_t = jnp.tile(jnp.concatenate([cos, cos], axis=1), (1, reps))
    sin_t = jnp.tile(jnp.concatenate([-sin, sin], axis=1), (1, reps))
    res_major = lambda t: t.reshape(seq // N_RES, N_RES, LANES).transpose(1, 0, 2)
    return res_major(cos_t), res_major(sin_t)


def kernel(x, mix_norm, ffn_norm, w_in_rec, conv_w, hgrn_lb, hgrn_norm, w_out_rec,
           w_qkv_attn, w_o_attn, w_ffn_in, w_ffn_out, final_norm):
    batch, seq, d = x.shape
    m = batch * seq
    xf = x.reshape(m, d)
    row = lambda v: v.reshape(1, -1)

    proj = _norm_matmul(xf, row(mix_norm[0]), w_in_rec[0].astype(BF16), F32)
    mix = _mixer(proj.reshape(batch, seq, REC_IN_WIDTH), conv_w[0], hgrn_lb, row(hgrn_norm[0]), batch, seq)
    xf = _oproj_ffn(xf, mix.reshape(m, d), w_out_rec[0].astype(BF16), row(ffn_norm[0]),
                    w_ffn_in[0].astype(BF16), w_ffn_out[0].astype(BF16), row(final_norm), False)

    cos_t, sin_t = _rope_tables(seq)
    qkv = _qkv_rope(xf.reshape(batch, seq, d), row(mix_norm[1]), w_qkv_attn[0].astype(BF16), cos_t, sin_t, batch, seq)
    att = _attention(qkv, batch, seq)
    xf = _oproj_ffn(xf, att.reshape(m, d), w_o_attn[0].astype(BF16), row(ffn_norm[1]),
                    w_ffn_in[1].astype(BF16), w_ffn_out[1].astype(BF16), row(final_norm), True)
    return xf.reshape(batch, seq, d)
```

```python
import functools

import jax
import jax.numpy as jnp
import numpy as np
from jax import lax
from jax.experimental import pallas as pl
from jax.experimental.pallas import tpu as pltpu

D_MODEL = 1024
CONV_WIDTH = 3
CONV_CHANNELS = 512
HGRN_HEAD_DIM = 128
HGRN_HEADS = 4
HGRN_WIDTH = 512
ATTN_HEAD_DIM = 64
ATTN_HEADS = 16
ATTN_BLOCK = 128
DILATIONS = (1, 4, 16)
ROPE_THETA = 10000.0
D_FF = 2816
RMS_EPS = 1e-6
REC_IN_WIDTH = 3 * CONV_CHANNELS + 4 * HGRN_WIDTH

LANES = 128
VMEM_LIMIT_BYTES = 56 * 1024 * 1024
ROW_TILE = 512
MIX_TILE = 256
HGRN_CHUNK = 32
MASK_VALUE = -1e30
ATTN_GROUP = 4

F32 = jnp.float32
BF16 = jnp.bfloat16


def _rmsnorm(x, g):
    return x * lax.rsqrt(jnp.mean(x * x, axis=-1, keepdims=True) + RMS_EPS) * g


def _const_spec(shape):
    zeros = (0,) * len(shape)
    return pl.BlockSpec(shape, lambda *_: zeros, pipeline_mode=pl.Buffered(1))


def _params(n_grid):
    return pltpu.CompilerParams(
        dimension_semantics=("arbitrary",) * n_grid, vmem_limit_bytes=VMEM_LIMIT_BYTES)


def _norm_matmul_kernel(x_ref, g_ref, w_ref, o_ref):
    h = _rmsnorm(x_ref[...], g_ref[...]).astype(BF16)
    o_ref[...] = jnp.dot(h, w_ref[...], preferred_element_type=F32).astype(o_ref.dtype)


def _norm_matmul(x, g, w, out_dtype):
    m, d = x.shape
    n = w.shape[1]
    return pl.pallas_call(
        _norm_matmul_kernel,
        grid=(m // ROW_TILE,),
        in_specs=[pl.BlockSpec((ROW_TILE, d), lambda i: (i, 0)),
                  _const_spec((1, d)),
                  _const_spec((d, n))],
        out_specs=pl.BlockSpec((ROW_TILE, n), lambda i: (i, 0)),
        out_shape=jax.ShapeDtypeStruct((m, n), out_dtype),
        compiler_params=_params(1),
        name="norm_matmul",
    )(x, g, w)


def _mixer_consts():
    ts, c = MIX_TILE, HGRN_CHUNK
    t = np.arange(ts)
    same = (t[:, None] // c) == (t[None, :] // c)
    tril = same & (t[:, None] >= t[None, :])
    cum = np.concatenate([tril, same]).astype(np.float32)
    ind = (t[:, None] // c == np.arange(LANES)[None, :]).astype(np.float32)
    return jnp.asarray(cum, BF16), jnp.asarray(ind, BF16)


def _mixer_kernel(proj_ref, convw_ref, lb_ref, hnorm_ref, cum_ref, ind_ref, out_ref,
                  ubuf, state, vblk):
    ts = MIX_TILE
    cc = CONV_CHANNELS
    c = HGRN_CHUNK
    n_chunks = ts // c
    j = pl.program_id(1)

    @pl.when(j == 0)
    def _():
        state[...] = jnp.zeros_like(state)
        vblk[...] = jnp.zeros_like(vblk)
        ubuf[0:8, :] = jnp.zeros((8, cc), F32)

    u = proj_ref[0, :, cc:2 * cc] * proj_ref[0, :, 2 * cc:3 * cc]
    ubuf[8:8 + ts, :] = u
    y = (convw_ref[0:1, :] * ubuf[6:6 + ts, :]
         + convw_ref[1:2, :] * ubuf[7:7 + ts, :]
         + convw_ref[2:3, :] * u)
    out_ref[0, :, 0:cc] = (proj_ref[0, :, 0:cc] * y).astype(out_ref.dtype)
    ubuf[0:8, :] = u[ts - 8:ts, :]

    lbp = lb_ref[...]
    e = jnp.exp(lbp - jnp.max(lbp, axis=0, keepdims=True))
    sm = e / jnp.sum(e, axis=0, keepdims=True)
    cum0 = sm[0:1, :]
    cum1 = cum0 + sm[1:2, :]
    lb = cum1 - cum0
    log_lb = jnp.log(lb)
    log_1mlb = jnp.log1p(-lb)
    q0 = 3 * cc
    z = proj_ref[0, :, q0 + HGRN_WIDTH:q0 + 2 * HGRN_WIDTH]
    ez = jnp.exp(-jnp.abs(z))
    log_sig = jnp.minimum(z, 0.0) - jnp.log1p(ez)
    b = log_1mlb + log_sig
    lf = jnp.maximum(log_lb, b) + jnp.log1p(jnp.exp(-jnp.abs(log_lb - b)))
    inv = 1.0 / (1.0 + ez)
    kk = (1.0 - lb) * jnp.where(z >= 0.0, ez * inv, inv)

    lf_hi = lf.astype(BF16)
    lf_lo = (lf - lf_hi.astype(F32)).astype(BF16)
    sums = (jnp.dot(cum_ref[...], lf_hi, preferred_element_type=F32)
            + jnp.dot(cum_ref[...], lf_lo, preferred_element_type=F32))
    g = sums[:ts]
    gl = sums[ts:]
    tn = (((0,), (0,)), ((), ()))
    decay_col = jnp.exp(lax.dot_general(lf_hi, ind_ref[...], tn, preferred_element_type=F32)
                        + lax.dot_general(lf_lo, ind_ref[...], tn, preferred_element_type=F32))

    half = 0.5 * gl
    q = proj_ref[0, :, q0:q0 + HGRN_WIDTH]
    qd = (q * jnp.exp(g - half)).astype(BF16)
    kd = (kk * jnp.exp(half - g)).astype(BF16)
    qg = (q * jnp.exp(g)).astype(BF16)
    kd2 = (kk * jnp.exp(gl - g)).astype(BF16)
    i_in = proj_ref[0, :, q0 + 2 * HGRN_WIDTH:q0 + 3 * HGRN_WIDTH]
    v = (i_in * (1.0 / (1.0 + jnp.exp(-i_in)))).astype(BF16)

    row = lax.broadcasted_iota(jnp.int32, (ts, ts), 0)
    col = lax.broadcasted_iota(jnp.int32, (ts, ts), 1)
    intra = (row // c == col // c) & (row >= col)
    hnorm = hnorm_ref[...]

    for hd in range(HGRN_HEADS):
        lanes = slice(hd * LANES, (hd + 1) * LANES)
        vh = v[:, lanes]
        a = lax.dot_general(qd[:, lanes], kd[:, lanes], (((1,), (1,)), ((), ())), preferred_element_type=F32)
        o = jnp.dot(jnp.where(intra, a, 0.0).astype(BF16), vh, preferred_element_type=F32)
        for ci in range(n_chunks):
            vblk[hd, ci * c:(ci + 1) * c, ci * LANES:(ci + 1) * LANES] = vh[ci * c:(ci + 1) * c]
        upd = lax.dot_general(kd2[:, lanes], vblk[hd], tn, preferred_element_type=F32)
        st = state[hd]
        inter = []
        for ci in range(n_chunks):
            inter.append(jnp.dot(qg[ci * c:(ci + 1) * c, lanes], st.astype(BF16), preferred_element_type=F32))
            st = st * decay_col[lanes, ci:ci + 1] + upd[:, ci * LANES:(ci + 1) * LANES]
        state[hd] = st
        o = o + jnp.concatenate(inter, axis=0)
        o = _rmsnorm(o, hnorm)
        gh = proj_ref[0, :, q0 + 3 * HGRN_WIDTH + hd * LANES:q0 + 3 * HGRN_WIDTH + (hd + 1) * LANES]
        o = o * (gh * (1.0 / (1.0 + jnp.exp(-gh))))
        out_ref[0, :, cc + hd * LANES:cc + (hd + 1) * LANES] = o.astype(out_ref.dtype)


def _mixer(proj, conv_w, hgrn_lb, hgrn_norm, batch, seq):
    ts = MIX_TILE
    cum, ind = _mixer_consts()
    return pl.pallas_call(
        _mixer_kernel,
        grid=(batch, seq // ts),
        in_specs=[pl.BlockSpec((1, ts, REC_IN_WIDTH), lambda b, j: (b, j, 0)),
                  _const_spec((CONV_WIDTH, CONV_CHANNELS)),
                  _const_spec((3, HGRN_WIDTH)),
                  _const_spec((1, HGRN_HEAD_DIM)),
                  _const_spec(cum.shape), _const_spec(ind.shape)],
        out_specs=pl.BlockSpec((1, ts, D_MODEL), lambda b, j: (b, j, 0)),
        out_shape=jax.ShapeDtypeStruct((batch, seq, D_MODEL), BF16),
        scratch_shapes=[pltpu.VMEM((ts + 8, CONV_CHANNELS), F32),
                        pltpu.VMEM((HGRN_HEADS, HGRN_HEAD_DIM, HGRN_HEAD_DIM), F32),
                        pltpu.VMEM((HGRN_HEADS, ts, (ts // HGRN_CHUNK) * LANES), BF16)],
        compiler_params=_params(2),
        name="conv_hgrn_mixer",
    )(proj, conv_w, hgrn_lb, hgrn_norm, cum, ind)


N_RES = DILATIONS[-1]


def _segments(d):
    count = N_RES // d
    return count, ATTN_BLOCK // count


def _attn_consts():
    blk = ATTN_BLOCK
    e = np.arange(blk)
    none = np.full((blk, blk), MASK_VALUE)
    kaug = []
    for d in DILATIONS:
        count, rows = _segments(d)
        pos = (e % rows) * count + e // rows
        kpos, qpos = pos[:, None], pos[None, :]
        prev_ok = np.where(kpos >= qpos, 0.0, MASK_VALUE)
        cur_ok = np.where(kpos <= qpos, 0.0, MASK_VALUE)
        kaug.append(np.stack([np.concatenate([none, cur_ok]), np.concatenate([prev_ok, cur_ok])]))
    qaug = np.concatenate([np.eye(blk), np.eye(blk)])
    return jnp.asarray(qaug, BF16), jnp.asarray(np.stack(kaug), BF16)


def _qkv_rope_kernel(*refs):
    n_slabs = D_MODEL // LANES
    x_refs = refs[:n_slabs]
    g_ref, w_ref, cos_ref, sin_ref, o_ref = refs[n_slabs:]
    rows = ROW_TILE // N_RES
    x = jnp.concatenate(
        [jnp.concatenate([ref[pl.ds(r, rows, stride=N_RES), :] for r in range(N_RES)], axis=0) for ref in x_refs],
        axis=1)
    h = _rmsnorm(x, g_ref[...]).astype(BF16)
    cos = cos_ref[...].reshape(ROW_TILE, LANES)
    sin = sin_ref[...].reshape(ROW_TILE, LANES)
    lane = lax.broadcasted_iota(jnp.int32, (ROW_TILE, LANES), 1)
    first_half = (lane % ATTN_HEAD_DIM) < (ATTN_HEAD_DIM // 2)
    width = ATTN_HEADS * ATTN_HEAD_DIM
    for part in range(3):
        y = jnp.dot(h, w_ref[:, part * width:(part + 1) * width], preferred_element_type=F32)
        if part < 2:
            cols = []
            for cb in range(width // LANES):
                t = y[:, cb * LANES:(cb + 1) * LANES]
                partner = jnp.where(first_half,
                                    pltpu.roll(t, LANES - ATTN_HEAD_DIM // 2, 1),
                                    pltpu.roll(t, ATTN_HEAD_DIM // 2, 1))
                t = t * cos + partner * sin
                cols.append(t * (ATTN_HEAD_DIM ** -0.5) if part == 0 else t)
            y = jnp.concatenate(cols, axis=1)
        o_ref[:, :, part * width:(part + 1) * width] = y.reshape(N_RES, rows, width)


def _qkv_rope(x, g, w, cos_t, sin_t, batch, seq):
    d = x.shape[-1]
    n = w.shape[1]
    rows = ROW_TILE // N_RES
    tiles = seq // ROW_TILE
    n_slabs = d // LANES
    x_specs = [pl.BlockSpec((ROW_TILE, LANES), functools.partial(lambda b, c, s: (b * tiles + c, s), s=s))
               for s in range(n_slabs)]
    table_spec = pl.BlockSpec((N_RES, rows, LANES), lambda b, c: (0, c, 0))
    return pl.pallas_call(
        _qkv_rope_kernel,
        grid=(batch, tiles),
        in_specs=x_specs + [_const_spec((1, d)), _const_spec((d, n)), table_spec, table_spec],
        out_specs=pl.BlockSpec((None, N_RES, rows, n), lambda b, c: (b, 0, c, 0)),
        out_shape=jax.ShapeDtypeStruct((batch, N_RES, seq // N_RES, n), F32),
        compiler_params=_params(2),
        name="qkv_rope",
    )(*([x] * n_slabs), g, w, cos_t, sin_t)


def _attn_kernel(q_ref, k_ref, v_ref, qaug_ref, kaug_ref, o_ref,
                 s_buf, m_s, l_s, acc_s, *, seq):
    blk = ATTN_BLOCK
    grp = ATTN_GROUP
    n_groups = (seq // blk) // grp
    assert n_groups % 2 == 0 and seq // N_RES == blk
    lane = lax.broadcasted_iota(jnp.int32, (blk, LANES), 1)
    first_head = lane < ATTN_HEAD_DIM

    def pick(top, bot):
        return jnp.where(first_head, top, bot)

    def block_id(d, j, u):
        n_blocks = seq // (d * blk)
        if n_blocks == 1:
            return j * grp + u, 0, None, None
        gpr = n_blocks // grp
        res = 0 if d == 1 else j // gpr
        n = (j % gpr) * grp + u if gpr > 1 else u
        if u > 0:
            return res, n, n - 1, 1
        if isinstance(n, int):
            return res, n, max(n - 1, 0), min(n, 1)
        return res, n, jnp.maximum(n - 1, 0), jnp.minimum(n, 1)

    def seg_rows(d, n):
        _, rows = _segments(d)
        start = n * rows
        return pl.ds(start if isinstance(start, int) else pl.multiple_of(start, rows), rows)

    def load_block(ref, d, res, n):
        count, _ = _segments(d)
        return jnp.concatenate([ref[res + d * i, seg_rows(d, n), :] for i in range(count)], axis=0)

    def store_block(ref, branch, d, res, n, val):
        count, rows = _segments(d)
        for i in range(count):
            ref[branch, res + d * i, seg_rows(d, n), :] = val[i * rows:(i + 1) * rows]

    def scores(branch, d, j, u):
        res, n, n_prev, has_prev = block_id(d, j, u)
        qb = load_block(q_ref, d, res, n)
        q2 = jnp.concatenate([jnp.where(first_head, qb, 0.0), jnp.where(first_head, 0.0, qb)], axis=0).astype(BF16)
        q_ext = jnp.concatenate([q2, qaug_ref[...]], axis=1)
        kc = load_block(k_ref, d, res, n)
        if n_prev is None:
            kk = kc.astype(BF16)
            aug = kaug_ref[branch, 1, blk:2 * blk, :]
        else:
            kk = jnp.concatenate([load_block(k_ref, d, res, n_prev), kc], axis=0).astype(BF16)
            aug = kaug_ref[branch, has_prev]
        k_ext = jnp.concatenate([kk, aug], axis=1)
        s = lax.dot_general(q_ext, k_ext, (((1,), (1,)), ((), ())), preferred_element_type=F32)
        s_buf[j % 2, u, :, 0:s.shape[1]] = s

    def softmax_pv(branch, d, j, u):
        res, n, n_prev, _ = block_id(d, j, u)
        vc = load_block(v_ref, d, res, n)
        if n_prev is None:
            vv = vc.astype(BF16)
        else:
            vv = jnp.concatenate([load_block(v_ref, d, res, n_prev), vc], axis=0).astype(BF16)
        s = s_buf[j % 2, u, :, 0:vv.shape[0]]
        m = jnp.max(s, axis=-1, keepdims=True)
        p = jnp.exp(s - m).astype(BF16)
        v_ext = jnp.concatenate([vv, jnp.ones_like(vv)], axis=1)
        out = jnp.dot(p, v_ext, preferred_element_type=F32)
        mb = jnp.broadcast_to(m, (2 * blk, LANES))
        store_block(m_s, branch, d, res, n, pick(mb[:blk], mb[blk:]))
        store_block(l_s, branch, d, res, n, pick(out[:blk, LANES:], out[blk:, LANES:]))
        store_block(acc_s, branch, d, res, n, pick(out[:blk, :LANES], out[blk:, :LANES]))

    for u in range(grp):
        scores(0, DILATIONS[0], 0, u)
    for branch, d in enumerate(DILATIONS):
        def group_body(j, carry, branch=branch, d=d):
            for u in range(grp):
                softmax_pv(branch, d, j, u)
            for u in range(grp):
                scores(branch, d, j + 1, u)
            return carry

        lax.fori_loop(0, n_groups - 1, group_body, 0)
        for u in range(grp):
            softmax_pv(branch, d, n_groups - 1, u)
        if branch + 1 < len(DILATIONS):
            for u in range(grp):
                scores(branch + 1, DILATIONS[branch + 1], 0, u)

    def merge_body(r, carry):
        ms = [m_s[i, r] for i in range(len(DILATIONS))]
        m_all = functools.reduce(jnp.maximum, ms)
        ws = [jnp.exp(mi - m_all) for mi in ms]
        num = sum(w * acc_s[i, r] for i, w in enumerate(ws))
        den = sum(w * l_s[i, r] for i, w in enumerate(ws))
        o_ref[0, pl.ds(r, blk, stride=N_RES), :] = num / den
        return carry

    lax.fori_loop(0, N_RES, merge_body, 0)


def _attention(qkv, batch, seq):
    n_pairs = ATTN_HEADS * ATTN_HEAD_DIM // LANES
    res_major = (N_RES, seq // N_RES, LANES)
    blk_spec = lambda off: pl.BlockSpec((None,) + res_major, lambda b, h: (b, 0, 0, off + h))
    nb = len(DILATIONS)
    qaug, kaug = _attn_consts()
    return pl.pallas_call(
        functools.partial(_attn_kernel, seq=seq),
        grid=(batch, n_pairs),
        in_specs=[blk_spec(0), blk_spec(n_pairs), blk_spec(2 * n_pairs),
                  _const_spec(qaug.shape), _const_spec(kaug.shape)],
        out_specs=pl.BlockSpec((1, seq, LANES), lambda b, h: (b, 0, h)),
        out_shape=jax.ShapeDtypeStruct((batch, seq, D_MODEL), F32),
        scratch_shapes=[pltpu.VMEM((2, ATTN_GROUP, 2 * ATTN_BLOCK, 2 * ATTN_BLOCK), F32),
                        pltpu.VMEM((nb,) + res_major, F32),
                        pltpu.VMEM((nb,) + res_major, F32),
                        pltpu.VMEM((nb,) + res_major, F32)],
        compiler_params=_params(2),
        name="dilated_attention",
    )(qkv, qkv, qkv, qaug, kaug)


def _oproj_ffn_kernel(x_ref, a_ref, wo_ref, g_ref, win_ref, wout_ref, gfin_ref, o_ref, *, final_norm):
    x1 = x_ref[...] + jnp.dot(a_ref[...].astype(BF16), wo_ref[...], preferred_element_type=F32)
    h = _rmsnorm(x1, g_ref[...]).astype(BF16)
    acc = x1
    n_split = 2
    fc = D_FF // n_split
    for c in range(n_split):
        gate = jnp.dot(h, win_ref[:, c * fc:(c + 1) * fc], preferred_element_type=F32)
        up = jnp.dot(h, win_ref[:, D_FF + c * fc:D_FF + (c + 1) * fc], preferred_element_type=F32)
        act = (gate * (1.0 / (1.0 + jnp.exp(-gate))) * up).astype(BF16)
        acc = acc + jnp.dot(act, wout_ref[c * fc:(c + 1) * fc, :], preferred_element_type=F32)
    if final_norm:
        acc = _rmsnorm(acc, gfin_ref[...])
    o_ref[...] = acc


def _oproj_ffn(x, a, wo, g, w_in, w_out, g_fin, final_norm):
    m, d = x.shape
    row_spec = pl.BlockSpec((ROW_TILE, d), lambda i: (i, 0))
    return pl.pallas_call(
        functools.partial(_oproj_ffn_kernel, final_norm=final_norm),
        grid=(m // ROW_TILE,),
        in_specs=[row_spec, row_spec,
                  _const_spec(wo.shape), _const_spec((1, d)),
                  _const_spec(w_in.shape), _const_spec(w_out.shape), _const_spec((1, d))],
        out_specs=row_spec,
        out_shape=jax.ShapeDtypeStruct((m, d), F32),
        compiler_params=_params(1),
        name="oproj_ffn",
    )(x, a, wo, g, w_in, w_out, g_fin)


def _rope_tables(seq):
    half = ATTN_HEAD_DIM // 2
    freqs = ROPE_THETA ** (-jnp.arange(half, dtype=F32) / half)
    ang = jnp.arange(seq, dtype=F32)[:, None] * freqs[None, :]
    cos, sin = jnp.cos(ang), jnp.sin(ang)
    reps = LANES // ATTN_HEAD_DIM
    cos_t = jnp.tile(jnp.concatenate([cos, cos], axis=1), (1, reps))
    sin_t = jnp.tile(jnp.concatenate([-sin, sin], axis=1), (1, reps))
    res_major = lambda t: t.reshape(seq // N_RES, N_RES, LANES).transpose(1, 0, 2)
    return res_major(cos_t), res_major(sin_t)


def kernel(x, mix_norm, ffn_norm, w_in_rec, conv_w, hgrn_lb, hgrn_norm, w_out_rec,
           w_qkv_attn, w_o_attn, w_ffn_in, w_ffn_out, final_norm):
    batch, seq, d = x.shape
    m = batch * seq
    xf = x.reshape(m, d)
    row = lambda v: v.reshape(1, -1)

    proj = _norm_matmul(xf, row(mix_norm[0]), w_in_rec[0].astype(BF16), F32)
    mix = _mixer(proj.reshape(batch, seq, REC_IN_WIDTH), conv_w[0], hgrn_lb, row(hgrn_norm[0]), batch, seq)
    xf = _oproj_ffn(xf, mix.reshape(m, d), w_out_rec[0].astype(BF16), row(ffn_norm[0]),
                    w_ffn_in[0].astype(BF16), w_ffn_out[0].astype(BF16), row(final_norm), False)

    cos_t, sin_t = _rope_tables(seq)
    qkv = _qkv_rope(xf, row(mix_norm[1]), w_qkv_attn[0].astype(BF16), cos_t, sin_t, batch, seq)
    att = _attention(qkv, batch, seq)
    xf = _oproj_ffn(xf, att.reshape(m, d), w_o_attn[0].astype(BF16), row(ffn_norm[1]),
                    w_ffn_in[1].astype(BF16), w_ffn_out[1].astype(BF16), row(final_norm), True)
    return xf.reshape(batch, seq, d)
```

```python
import functools

import jax
import jax.numpy as jnp
import numpy as np
from jax import lax
from jax.experimental import pallas as pl
from jax.experimental.pallas import tpu as pltpu

D_MODEL = 1024
CONV_WIDTH = 3
CONV_CHANNELS = 512
HGRN_HEAD_DIM = 128
HGRN_HEADS = 4
HGRN_WIDTH = 512
ATTN_HEAD_DIM = 64
ATTN_HEADS = 16
ATTN_BLOCK = 128
DILATIONS = (1, 4, 16)
ROPE_THETA = 10000.0
D_FF = 2816
RMS_EPS = 1e-6
REC_IN_WIDTH = 3 * CONV_CHANNELS + 4 * HGRN_WIDTH

LANES = 128
VMEM_LIMIT_BYTES = 56 * 1024 * 1024
ROW_TILE = 512
MIX_TILE = 256
HGRN_CHUNK = 32
MASK_VALUE = -1e30
ATTN_GROUP = 8

F32 = jnp.float32
BF16 = jnp.bfloat16


def _rmsnorm(x, g):
    return x * lax.rsqrt(jnp.mean(x * x, axis=-1, keepdims=True) + RMS_EPS) * g


def _const_spec(shape):
    zeros = (0,) * len(shape)
    return pl.BlockSpec(shape, lambda *_: zeros, pipeline_mode=pl.Buffered(1))


def _params(n_grid):
    return pltpu.CompilerParams(
        dimension_semantics=("arbitrary",) * n_grid, vmem_limit_bytes=VMEM_LIMIT_BYTES)


def _norm_matmul_kernel(x_ref, g_ref, w_ref, o_ref):
    h = _rmsnorm(x_ref[...], g_ref[...]).astype(BF16)
    o_ref[...] = jnp.dot(h, w_ref[...], preferred_element_type=F32).astype(o_ref.dtype)


def _norm_matmul(x, g, w, out_dtype):
    m, d = x.shape
    n = w.shape[1]
    return pl.pallas_call(
        _norm_matmul_kernel,
        grid=(m // ROW_TILE,),
        in_specs=[pl.BlockSpec((ROW_TILE, d), lambda i: (i, 0)),
                  _const_spec((1, d)),
                  _const_spec((d, n))],
        out_specs=pl.BlockSpec((ROW_TILE, n), lambda i: (i, 0)),
        out_shape=jax.ShapeDtypeStruct((m, n), out_dtype),
        compiler_params=_params(1),
        name="norm_matmul",
    )(x, g, w)


def _mixer_consts():
    ts, c = MIX_TILE, HGRN_CHUNK
    t = np.arange(ts)
    same = (t[:, None] // c) == (t[None, :] // c)
    tril = same & (t[:, None] >= t[None, :])
    cum = np.concatenate([tril, same]).astype(np.float32)
    ind = (t[:, None] // c == np.arange(LANES)[None, :]).astype(np.float32)
    return jnp.asarray(cum, BF16), jnp.asarray(ind, BF16)


def _mixer_kernel(proj_ref, convw_ref, lb_ref, hnorm_ref, cum_ref, ind_ref, out_ref,
                  ubuf, state, vblk):
    ts = MIX_TILE
    cc = CONV_CHANNELS
    c = HGRN_CHUNK
    n_chunks = ts // c
    j = pl.program_id(1)

    @pl.when(j == 0)
    def _():
        state[...] = jnp.zeros_like(state)
        vblk[...] = jnp.zeros_like(vblk)
        ubuf[0:8, :] = jnp.zeros((8, cc), F32)

    u = proj_ref[0, :, cc:2 * cc] * proj_ref[0, :, 2 * cc:3 * cc]
    ubuf[8:8 + ts, :] = u
    y = (convw_ref[0:1, :] * ubuf[6:6 + ts, :]
         + convw_ref[1:2, :] * ubuf[7:7 + ts, :]
         + convw_ref[2:3, :] * u)
    out_ref[0, :, 0:cc] = (proj_ref[0, :, 0:cc] * y).astype(out_ref.dtype)
    ubuf[0:8, :] = u[ts - 8:ts, :]

    lbp = lb_ref[...]
    e = jnp.exp(lbp - jnp.max(lbp, axis=0, keepdims=True))
    sm = e / jnp.sum(e, axis=0, keepdims=True)
    cum0 = sm[0:1, :]
    cum1 = cum0 + sm[1:2, :]
    lb = cum1 - cum0
    log_lb = jnp.log(lb)
    log_1mlb = jnp.log1p(-lb)
    q0 = 3 * cc
    z = proj_ref[0, :, q0 + HGRN_WIDTH:q0 + 2 * HGRN_WIDTH]
    ez = jnp.exp(-jnp.abs(z))
    log_sig = jnp.minimum(z, 0.0) - jnp.log1p(ez)
    b = log_1mlb + log_sig
    lf = jnp.maximum(log_lb, b) + jnp.log1p(jnp.exp(-jnp.abs(log_lb - b)))
    inv = 1.0 / (1.0 + ez)
    kk = (1.0 - lb) * jnp.where(z >= 0.0, ez * inv, inv)

    lf_hi = lf.astype(BF16)
    lf_lo = (lf - lf_hi.astype(F32)).astype(BF16)
    sums = (jnp.dot(cum_ref[...], lf_hi, preferred_element_type=F32)
            + jnp.dot(cum_ref[...], lf_lo, preferred_element_type=F32))
    g = sums[:ts]
    gl = sums[ts:]
    tn = (((0,), (0,)), ((), ()))
    decay_col = jnp.exp(lax.dot_general(lf_hi, ind_ref[...], tn, preferred_element_type=F32)
                        + lax.dot_general(lf_lo, ind_ref[...], tn, preferred_element_type=F32))

    half = 0.5 * gl
    q = proj_ref[0, :, q0:q0 + HGRN_WIDTH]
    qd = (q * jnp.exp(g - half)).astype(BF16)
    kd = (kk * jnp.exp(half - g)).astype(BF16)
    qg = (q * jnp.exp(g)).astype(BF16)
    kd2 = (kk * jnp.exp(gl - g)).astype(BF16)
    i_in = proj_ref[0, :, q0 + 2 * HGRN_WIDTH:q0 + 3 * HGRN_WIDTH]
    v = (i_in * (1.0 / (1.0 + jnp.exp(-i_in)))).astype(BF16)

    row = lax.broadcasted_iota(jnp.int32, (ts, ts), 0)
    col = lax.broadcasted_iota(jnp.int32, (ts, ts), 1)
    intra = (row // c == col // c) & (row >= col)
    hnorm = hnorm_ref[...]

    for hd in range(HGRN_HEADS):
        lanes = slice(hd * LANES, (hd + 1) * LANES)
        vh = v[:, lanes]
        a = lax.dot_general(qd[:, lanes], kd[:, lanes], (((1,), (1,)), ((), ())), preferred_element_type=F32)
        o = jnp.dot(jnp.where(intra, a, 0.0).astype(BF16), vh, preferred_element_type=F32)
        for ci in range(n_chunks):
            vblk[hd, ci * c:(ci + 1) * c, ci * LANES:(ci + 1) * LANES] = vh[ci * c:(ci + 1) * c]
        upd = lax.dot_general(kd2[:, lanes], vblk[hd], tn, preferred_element_type=F32)
        st = state[hd]
        inter = []
        for ci in range(n_chunks):
            inter.append(jnp.dot(qg[ci * c:(ci + 1) * c, lanes], st.astype(BF16), preferred_element_type=F32))
            st = st * decay_col[lanes, ci:ci + 1] + upd[:, ci * LANES:(ci + 1) * LANES]
        state[hd] = st
        o = o + jnp.concatenate(inter, axis=0)
        o = _rmsnorm(o, hnorm)
        gh = proj_ref[0, :, q0 + 3 * HGRN_WIDTH + hd * LANES:q0 + 3 * HGRN_WIDTH + (hd + 1) * LANES]
        o = o * (gh * (1.0 / (1.0 + jnp.exp(-gh))))
        out_ref[0, :, cc + hd * LANES:cc + (hd + 1) * LANES] = o.astype(out_ref.dtype)


def _mixer(proj, conv_w, hgrn_lb, hgrn_norm, batch, seq):
    ts = MIX_TILE
    cum, ind = _mixer_consts()
    return pl.pallas_call(
        _mixer_kernel,
        grid=(batch, seq // ts),
        in_specs=[pl.BlockSpec((1, ts, REC_IN_WIDTH), lambda b, j: (b, j, 0)),
                  _const_spec((CONV_WIDTH, CONV_CHANNELS)),
                  _const_spec((3, HGRN_WIDTH)),
                  _const_spec((1, HGRN_HEAD_DIM)),
                  _const_spec(cum.shape), _const_spec(ind.shape)],
        out_specs=pl.BlockSpec((1, ts, D_MODEL), lambda b, j: (b, j, 0)),
        out_shape=jax.ShapeDtypeStruct((batch, seq, D_MODEL), BF16),
        scratch_shapes=[pltpu.VMEM((ts + 8, CONV_CHANNELS), F32),
                        pltpu.VMEM((HGRN_HEADS, HGRN_HEAD_DIM, HGRN_HEAD_DIM), F32),
                        pltpu.VMEM((HGRN_HEADS, ts, (ts // HGRN_CHUNK) * LANES), BF16)],
        compiler_params=_params(2),
        name="conv_hgrn_mixer",
    )(proj, conv_w, hgrn_lb, hgrn_norm, cum, ind)


N_RES = DILATIONS[-1]


def _segments(d):
    count = N_RES // d
    return count, ATTN_BLOCK // count


def _attn_consts():
    blk = ATTN_BLOCK
    e = np.arange(blk)
    none = np.full((blk, blk), MASK_VALUE)
    kaug = []
    for d in DILATIONS:
        count, rows = _segments(d)
        pos = (e % rows) * count + e // rows
        kpos, qpos = pos[:, None], pos[None, :]
        prev_ok = np.where(kpos >= qpos, 0.0, MASK_VALUE)
        cur_ok = np.where(kpos <= qpos, 0.0, MASK_VALUE)
        kaug.append(np.stack([np.concatenate([none, cur_ok]), np.concatenate([prev_ok, cur_ok])]))
    qaug = np.concatenate([np.eye(blk), np.eye(blk)])
    return jnp.asarray(qaug, BF16), jnp.asarray(np.stack(kaug), BF16)


def _qkv_rope_kernel(*refs):
    n_slabs = D_MODEL // LANES
    x_refs = refs[:n_slabs]
    g_ref, w_ref, cos_ref, sin_ref, o_ref = refs[n_slabs:]
    rows = ROW_TILE // N_RES
    x = jnp.concatenate(
        [jnp.concatenate([ref[pl.ds(r, rows, stride=N_RES), :] for r in range(N_RES)], axis=0) for ref in x_refs],
        axis=1)
    h = _rmsnorm(x, g_ref[...]).astype(BF16)
    cos = cos_ref[...].reshape(ROW_TILE, LANES)
    sin = sin_ref[...].reshape(ROW_TILE, LANES)
    lane = lax.broadcasted_iota(jnp.int32, (ROW_TILE, LANES), 1)
    first_half = (lane % ATTN_HEAD_DIM) < (ATTN_HEAD_DIM // 2)
    width = ATTN_HEADS * ATTN_HEAD_DIM
    for part in range(3):
        y = jnp.dot(h, w_ref[:, part * width:(part + 1) * width], preferred_element_type=F32)
        if part < 2:
            cols = []
            for cb in range(width // LANES):
                t = y[:, cb * LANES:(cb + 1) * LANES]
                partner = jnp.where(first_half,
                                    pltpu.roll(t, LANES - ATTN_HEAD_DIM // 2, 1),
                                    pltpu.roll(t, ATTN_HEAD_DIM // 2, 1))
                t = t * cos + partner * sin
                cols.append(t * (ATTN_HEAD_DIM ** -0.5) if part == 0 else t)
            y = jnp.concatenate(cols, axis=1)
        o_ref[:, :, part * width:(part + 1) * width] = y.reshape(N_RES, rows, width)


def _qkv_rope(x, g, w, cos_t, sin_t, batch, seq):
    d = x.shape[-1]
    n = w.shape[1]
    rows = ROW_TILE // N_RES
    tiles = seq // ROW_TILE
    n_slabs = d // LANES
    x_specs = [pl.BlockSpec((ROW_TILE, LANES), functools.partial(lambda b, c, s: (b * tiles + c, s), s=s))
               for s in range(n_slabs)]
    table_spec = pl.BlockSpec((N_RES, rows, LANES), lambda b, c: (0, c, 0))
    return pl.pallas_call(
        _qkv_rope_kernel,
        grid=(batch, tiles),
        in_specs=x_specs + [_const_spec((1, d)), _const_spec((d, n)), table_spec, table_spec],
        out_specs=pl.BlockSpec((None, N_RES, rows, n), lambda b, c: (b, 0, c, 0)),
        out_shape=jax.ShapeDtypeStruct((batch, N_RES, seq // N_RES, n), F32),
        compiler_params=_params(2),
        name="qkv_rope",
    )(*([x] * n_slabs), g, w, cos_t, sin_t)


def _attn_kernel(q_ref, k_ref, v_ref, qaug_ref, kaug_ref, o_ref,
                 s_buf, m_s, l_s, acc_s, *, seq):
    blk = ATTN_BLOCK
    grp = ATTN_GROUP
    n_groups = (seq // blk) // grp
    assert n_groups % 2 == 0 and seq // N_RES == blk
    last = len(DILATIONS) - 1
    lane = lax.broadcasted_iota(jnp.int32, (blk, LANES), 1)
    first_head = lane < ATTN_HEAD_DIM

    def pick(top, bot):
        return jnp.where(first_head, top, bot)

    def block_id(d, j, u):
        n_blocks = seq // (d * blk)
        if n_blocks == 1:
            return j * grp + u, 0, None, None
        if n_blocks >= grp:
            gpr = n_blocks // grp
            res = 0 if d == 1 else j // gpr
            n = (j % gpr) * grp + u if gpr > 1 else u
        else:
            res = j * (grp // n_blocks) + u // n_blocks
            n = u % n_blocks
        if isinstance(n, int):
            return res, n, max(n - 1, 0), min(n, 1)
        if u > 0:
            return res, n, n - 1, 1
        return res, n, jnp.maximum(n - 1, 0), jnp.minimum(n, 1)

    def seg_rows(d, n):
        _, rows = _segments(d)
        start = n * rows
        return pl.ds(start if isinstance(start, int) else pl.multiple_of(start, rows), rows)

    def load_block(ref, d, res, n):
        count, _ = _segments(d)
        return jnp.concatenate([ref[res + d * i, seg_rows(d, n), :] for i in range(count)], axis=0)

    def store_block(ref, branch, d, res, n, val):
        count, rows = _segments(d)
        for i in range(count):
            ref[branch, res + d * i, seg_rows(d, n), :] = val[i * rows:(i + 1) * rows]

    def scores(branch, d, j, u):
        res, n, n_prev, has_prev = block_id(d, j, u)
        qb = load_block(q_ref, d, res, n)
        q2 = jnp.concatenate([jnp.where(first_head, qb, 0.0), jnp.where(first_head, 0.0, qb)], axis=0).astype(BF16)
        q_ext = jnp.concatenate([q2, qaug_ref[...]], axis=1)
        kc = load_block(k_ref, d, res, n)
        if n_prev is None:
            kk = kc.astype(BF16)
            aug = kaug_ref[branch, 1, blk:2 * blk, :]
        else:
            kk = jnp.concatenate([load_block(k_ref, d, res, n_prev), kc], axis=0).astype(BF16)
            aug = kaug_ref[branch, has_prev]
        k_ext = jnp.concatenate([kk, aug], axis=1)
        s = lax.dot_general(q_ext, k_ext, (((1,), (1,)), ((), ())), preferred_element_type=F32)
        s_buf[j % 2, u, :, 0:s.shape[1]] = s

    def softmax_pv(branch, d, j, u):
        res, n, n_prev, _ = block_id(d, j, u)
        vc = load_block(v_ref, d, res, n)
        if n_prev is None:
            vv = vc.astype(BF16)
        else:
            vv = jnp.concatenate([load_block(v_ref, d, res, n_prev), vc], axis=0).astype(BF16)
        s = s_buf[j % 2, u, :, 0:vv.shape[0]]
        m = jnp.max(s, axis=-1, keepdims=True)
        p = jnp.exp(s - m).astype(BF16)
        v_ext = jnp.concatenate([vv, jnp.ones_like(vv)], axis=1)
        out = jnp.dot(p, v_ext, preferred_element_type=F32)
        mb = jnp.broadcast_to(m, (2 * blk, LANES))
        m_b = pick(mb[:blk], mb[blk:])
        l_b = pick(out[:blk, LANES:], out[blk:, LANES:])
        acc_b = pick(out[:blk, :LANES], out[blk:, :LANES])
        if branch < last:
            store_block(m_s, branch, d, res, n, m_b)
            store_block(l_s, branch, d, res, n, l_b)
            store_block(acc_s, branch, d, res, n, acc_b)
        else:
            ms = [m_s[i, res] for i in range(last)] + [m_b]
            ls = [l_s[i, res] for i in range(last)] + [l_b]
            accs = [acc_s[i, res] for i in range(last)] + [acc_b]
            m_all = functools.reduce(jnp.maximum, ms)
            ws = [jnp.exp(mi - m_all) for mi in ms]
            num = sum(w * a for w, a in zip(ws, accs))
            den = sum(w * l for w, l in zip(ws, ls))
            o_ref[0, pl.ds(res, blk, stride=N_RES), :] = num / den

    for u in range(grp):
        scores(0, DILATIONS[0], 0, u)
    for branch, d in enumerate(DILATIONS):
        def group_body(j, carry, branch=branch, d=d):
            for u in range(grp):
                softmax_pv(branch, d, j, u)
            for u in range(grp):
                scores(branch, d, j + 1, u)
            return carry

        lax.fori_loop(0, n_groups - 1, group_body, 0)
        for u in range(grp):
            softmax_pv(branch, d, n_groups - 1, u)
        if branch + 1 < len(DILATIONS):
            for u in range(grp):
                scores(branch + 1, DILATIONS[branch + 1], 0, u)


def _attention(qkv, batch, seq):
    n_pairs = ATTN_HEADS * ATTN_HEAD_DIM // LANES
    res_major = (N_RES, seq // N_RES, LANES)
    blk_spec = lambda off: pl.BlockSpec((None,) + res_major, lambda b, h: (b, 0, 0, off + h))
    nb = len(DILATIONS) - 1
    qaug, kaug = _attn_consts()
    return pl.pallas_call(
        functools.partial(_attn_kernel, seq=seq),
        grid=(batch, n_pairs),
        in_specs=[blk_spec(0), blk_spec(n_pairs), blk_spec(2 * n_pairs),
                  _const_spec(qaug.shape), _const_spec(kaug.shape)],
        out_specs=pl.BlockSpec((1, seq, LANES), lambda b, h: (b, 0, h)),
        out_shape=jax.ShapeDtypeStruct((batch, seq, D_MODEL), F32),
        scratch_shapes=[pltpu.VMEM((2, ATTN_GROUP, 2 * ATTN_BLOCK, 2 * ATTN_BLOCK), F32),
                        pltpu.VMEM((nb,) + res_major, F32),
                        pltpu.VMEM((nb,) + res_major, F32),
                        pltpu.VMEM((nb,) + res_major, F32)],
        compiler_params=_params(2),
        name="dilated_attention",
    )(qkv, qkv, qkv, qaug, kaug)


def _oproj_ffn_kernel(x_ref, a_ref, wo_ref, g_ref, win_ref, wout_ref, gfin_ref, o_ref, *, final_norm):
    x1 = x_ref[...] + jnp.dot(a_ref[...].astype(BF16), wo_ref[...], preferred_element_type=F32)
    h = _rmsnorm(x1, g_ref[...]).astype(BF16)
    acc = x1
    n_split = 2
    fc = D_FF // n_split
    for c in range(n_split):
        gate = jnp.dot(h, win_ref[:, c * fc:(c + 1) * fc], preferred_element_type=F32)
        up = jnp.dot(h, win_ref[:, D_FF + c * fc:D_FF + (c + 1) * fc], preferred_element_type=F32)
        act = (gate * (1.0 / (1.0 + jnp.exp(-gate))) * up).astype(BF16)
        acc = acc + jnp.dot(act, wout_ref[c * fc:(c + 1) * fc, :], preferred_element_type=F32)
    if final_norm:
        acc = _rmsnorm(acc, gfin_ref[...])
    o_ref[...] = acc


def _oproj_ffn(x, a, wo, g, w_in, w_out, g_fin, final_norm):
    m, d = x.shape
    row_spec = pl.BlockSpec((ROW_TILE, d), lambda i: (i, 0))
    return pl.pallas_call(
        functools.partial(_oproj_ffn_kernel, final_norm=final_norm),
        grid=(m // ROW_TILE,),
        in_specs=[row_spec, row_spec,
                  _const_spec(wo.shape), _const_spec((1, d)),
                  _const_spec(w_in.shape), _const_spec(w_out.shape), _const_spec((1, d))],
        out_specs=row_spec,
        out_shape=jax.ShapeDtypeStruct((m, d), F32),
        compiler_params=_params(1),
        name="oproj_ffn",
    )(x, a, wo, g, w_in, w_out, g_fin)


def _rope_tables(seq):
    half = ATTN_HEAD_DIM // 2
    freqs = ROPE_THETA ** (-jnp.arange(half, dtype=F32) / half)
    ang = jnp.arange(seq, dtype=F32)[:, None] * freqs[None, :]
    cos, sin = jnp.cos(ang), jnp.sin(ang)
    reps = LANES // ATTN_HEAD_DIM
    cos_t = jnp.tile(jnp.concatenate([cos, cos], axis=1), (1, reps))
    sin_t = jnp.tile(jnp.concatenate([-sin, sin], axis=1), (1, reps))
    res_major = lambda t: t.reshape(seq // N_RES, N_RES, LANES).transpose(1, 0, 2)
    return res_major(cos_t), res_major(sin_t)


def kernel(x, mix_norm, ffn_norm, w_in_rec, conv_w, hgrn_lb, hgrn_norm, w_out_rec,
           w_qkv_attn, w_o_attn, w_ffn_in, w_ffn_out, final_norm):
    batch, seq, d = x.shape
    m = batch * seq
    xf = x.reshape(m, d)
    row = lambda v: v.reshape(1, -1)

    proj = _norm_matmul(xf, row(mix_norm[0]), w_in_rec[0].astype(BF16), F32)
    mix = _mixer(proj.reshape(batch, seq, REC_IN_WIDTH), conv_w[0], hgrn_lb, row(hgrn_norm[0]), batch, seq)
    xf = _oproj_ffn(xf, mix.reshape(m, d), w_out_rec[0].astype(BF16), row(ffn_norm[0]),
                    w_ffn_in[0].astype(BF16), w_ffn_out[0].astype(BF16), row(final_norm), False)

    cos_t, sin_t = _rope_tables(seq)
    qkv = _qkv_rope(xf, row(mix_norm[1]), w_qkv_attn[0].astype(BF16), cos_t, sin_t, batch, seq)
    att = _attention(qkv, batch, seq)
    xf = _oproj_ffn(xf, att.reshape(m, d), w_o_attn[0].astype(BF16), row(ffn_norm[1]),
                    w_ffn_in[1].astype(BF16), w_ffn_out[1].astype(BF16), row(final_norm), True)
    return xf.reshape(batch, seq, d)
```

```python
import functools

import jax
import jax.numpy as jnp
import numpy as np
from jax import lax
from jax.experimental import pallas as pl
from jax.experimental.pallas import tpu as pltpu

D_MODEL = 1024
CONV_WIDTH = 3
CONV_CHANNELS = 512
HGRN_HEAD_DIM = 128
HGRN_HEADS = 4
HGRN_WIDTH = 512
ATTN_HEAD_DIM = 64
ATTN_HEADS = 16
ATTN_BLOCK = 128
DILATIONS = (1, 4, 16)
ROPE_THETA = 10000.0
D_FF = 2816
RMS_EPS = 1e-6
REC_IN_WIDTH = 3 * CONV_CHANNELS + 4 * HGRN_WIDTH

LANES = 128
VMEM_LIMIT_BYTES = 56 * 1024 * 1024
ROW_TILE = 512
MIX_TILE = 256
MIX_ROWS = 2
HGRN_CHUNK = 32
MASK_VALUE = -1e30
ATTN_GROUP = 8

F32 = jnp.float32
BF16 = jnp.bfloat16


def _rmsnorm(x, g):
    return x * lax.rsqrt(jnp.mean(x * x, axis=-1, keepdims=True) + RMS_EPS) * g


def _const_spec(shape):
    zeros = (0,) * len(shape)
    return pl.BlockSpec(shape, lambda *_: zeros, pipeline_mode=pl.Buffered(1))


def _params(n_grid):
    return pltpu.CompilerParams(
        dimension_semantics=("arbitrary",) * n_grid, vmem_limit_bytes=VMEM_LIMIT_BYTES)


def _norm_matmul_kernel(x_ref, g_ref, w_ref, o_ref):
    h = _rmsnorm(x_ref[...], g_ref[...]).astype(BF16)
    o_ref[...] = jnp.dot(h, w_ref[...], preferred_element_type=F32).astype(o_ref.dtype)


def _norm_matmul(x, g, w, out_dtype):
    m, d = x.shape
    n = w.shape[1]
    return pl.pallas_call(
        _norm_matmul_kernel,
        grid=(m // ROW_TILE,),
        in_specs=[pl.BlockSpec((ROW_TILE, d), lambda i: (i, 0)),
                  _const_spec((1, d)),
                  _const_spec((d, n))],
        out_specs=pl.BlockSpec((ROW_TILE, n), lambda i: (i, 0)),
        out_shape=jax.ShapeDtypeStruct((m, n), out_dtype),
        compiler_params=_params(1),
        name="norm_matmul",
    )(x, g, w)


def _mixer_consts():
    ts, c = MIX_TILE, HGRN_CHUNK
    t = np.arange(ts)
    same = (t[:, None] // c) == (t[None, :] // c)
    tril = same & (t[:, None] >= t[None, :])
    cum = np.concatenate([tril, same]).astype(np.float32)
    ind = (t[:, None] // c == np.arange(LANES)[None, :]).astype(np.float32)
    return jnp.asarray(cum, BF16), jnp.asarray(ind, BF16)


def _mixer_kernel(proj_ref, convw_ref, lb_ref, hnorm_ref, cum_ref, ind_ref, out_ref,
                  ubuf, state, vblk):
    @pl.when(pl.program_id(1) == 0)
    def _():
        state[...] = jnp.zeros_like(state)
        vblk[...] = jnp.zeros_like(vblk)
        ubuf[:, 0:8, :] = jnp.zeros((MIX_ROWS, 8, CONV_CHANNELS), F32)

    for bb in range(MIX_ROWS):
        _mixer_tile(proj_ref.at[bb], convw_ref, lb_ref, hnorm_ref, cum_ref, ind_ref, out_ref.at[bb],
                    ubuf.at[bb], state.at[bb], vblk.at[bb])


def _mixer_tile(proj_ref, convw_ref, lb_ref, hnorm_ref, cum_ref, ind_ref, out_ref, ubuf, state, vblk):
    ts = MIX_TILE
    cc = CONV_CHANNELS
    c = HGRN_CHUNK
    n_chunks = ts // c

    u = proj_ref[:, cc:2 * cc] * proj_ref[:, 2 * cc:3 * cc]
    ubuf[8:8 + ts, :] = u
    y = (convw_ref[0:1, :] * ubuf[6:6 + ts, :]
         + convw_ref[1:2, :] * ubuf[7:7 + ts, :]
         + convw_ref[2:3, :] * u)
    out_ref[:, 0:cc] = (proj_ref[:, 0:cc] * y).astype(out_ref.dtype)
    ubuf[0:8, :] = u[ts - 8:ts, :]

    lbp = lb_ref[...]
    e = jnp.exp(lbp - jnp.max(lbp, axis=0, keepdims=True))
    sm = e / jnp.sum(e, axis=0, keepdims=True)
    cum0 = sm[0:1, :]
    cum1 = cum0 + sm[1:2, :]
    lb = cum1 - cum0
    q0 = 3 * cc
    z = proj_ref[:, q0 + HGRN_WIDTH:q0 + 2 * HGRN_WIDTH]
    ez = jnp.exp(-jnp.abs(z))
    inv = 1.0 / (1.0 + ez)
    small = ez * inv
    pos = z >= 0.0
    lf = jnp.log(lb + (1.0 - lb) * jnp.where(pos, inv, small))
    kk = (1.0 - lb) * jnp.where(pos, small, inv)

    lf_hi = lf.astype(BF16)
    lf_lo = (lf - lf_hi.astype(F32)).astype(BF16)
    sums = (jnp.dot(cum_ref[...], lf_hi, preferred_element_type=F32)
            + jnp.dot(cum_ref[...], lf_lo, preferred_element_type=F32))
    g = sums[:ts]
    gl = sums[ts:]
    tn = (((0,), (0,)), ((), ()))
    decay_col = jnp.exp(lax.dot_general(lf_hi, ind_ref[...], tn, preferred_element_type=F32)
                        + lax.dot_general(lf_lo, ind_ref[...], tn, preferred_element_type=F32))

    half = 0.5 * gl
    q = proj_ref[:, q0:q0 + HGRN_WIDTH]
    qd = (q * jnp.exp(g - half)).astype(BF16)
    kd = (kk * jnp.exp(half - g)).astype(BF16)
    qg = (q * jnp.exp(g)).astype(BF16)
    kd2 = (kk * jnp.exp(gl - g)).astype(BF16)
    i_in = proj_ref[:, q0 + 2 * HGRN_WIDTH:q0 + 3 * HGRN_WIDTH]
    v = (i_in * (1.0 / (1.0 + jnp.exp(-i_in)))).astype(BF16)

    row = lax.broadcasted_iota(jnp.int32, (ts, ts), 0)
    col = lax.broadcasted_iota(jnp.int32, (ts, ts), 1)
    intra = (row // c == col // c) & (row >= col)
    hnorm = hnorm_ref[...]

    for hd in range(HGRN_HEADS):
        lanes = slice(hd * LANES, (hd + 1) * LANES)
        vh = v[:, lanes]
        a = lax.dot_general(qd[:, lanes], kd[:, lanes], (((1,), (1,)), ((), ())), preferred_element_type=F32)
        o = jnp.dot(jnp.where(intra, a, 0.0).astype(BF16), vh, preferred_element_type=F32)
        for ci in range(n_chunks):
            vblk[hd, ci * c:(ci + 1) * c, ci * LANES:(ci + 1) * LANES] = vh[ci * c:(ci + 1) * c]
        upd = lax.dot_general(kd2[:, lanes], vblk[hd], tn, preferred_element_type=F32)
        st = state[hd]
        inter = []
        for ci in range(n_chunks):
            inter.append(jnp.dot(qg[ci * c:(ci + 1) * c, lanes], st.astype(BF16), preferred_element_type=F32))
            st = st * decay_col[lanes, ci:ci + 1] + upd[:, ci * LANES:(ci + 1) * LANES]
        state[hd] = st
        o = o + jnp.concatenate(inter, axis=0)
        o = _rmsnorm(o, hnorm)
        gh = proj_ref[:, q0 + 3 * HGRN_WIDTH + hd * LANES:q0 + 3 * HGRN_WIDTH + (hd + 1) * LANES]
        o = o * (gh * (1.0 / (1.0 + jnp.exp(-gh))))
        out_ref[:, cc + hd * LANES:cc + (hd + 1) * LANES] = o.astype(out_ref.dtype)


def _mixer(proj, conv_w, hgrn_lb, hgrn_norm, batch, seq):
    ts = MIX_TILE
    cum, ind = _mixer_consts()
    return pl.pallas_call(
        _mixer_kernel,
        grid=(batch // MIX_ROWS, seq // ts),
        in_specs=[pl.BlockSpec((MIX_ROWS, ts, REC_IN_WIDTH), lambda b, j: (b, j, 0)),
                  _const_spec((CONV_WIDTH, CONV_CHANNELS)),
                  _const_spec((3, HGRN_WIDTH)),
                  _const_spec((1, HGRN_HEAD_DIM)),
                  _const_spec(cum.shape), _const_spec(ind.shape)],
        out_specs=pl.BlockSpec((MIX_ROWS, ts, D_MODEL), lambda b, j: (b, j, 0)),
        out_shape=jax.ShapeDtypeStruct((batch, seq, D_MODEL), BF16),
        scratch_shapes=[pltpu.VMEM((MIX_ROWS, ts + 8, CONV_CHANNELS), F32),
                        pltpu.VMEM((MIX_ROWS, HGRN_HEADS, HGRN_HEAD_DIM, HGRN_HEAD_DIM), F32),
                        pltpu.VMEM((MIX_ROWS, HGRN_HEADS, ts, (ts // HGRN_CHUNK) * LANES), BF16)],
        compiler_params=_params(2),
        name="conv_hgrn_mixer",
    )(proj, conv_w, hgrn_lb, hgrn_norm, cum, ind)


N_RES = DILATIONS[-1]


def _segments(d):
    count = N_RES // d
    return count, ATTN_BLOCK // count


def _attn_consts():
    blk = ATTN_BLOCK
    e = np.arange(blk)
    none = np.full((blk, blk), MASK_VALUE)
    kaug = []
    for d in DILATIONS:
        count, rows = _segments(d)
        pos = (e % rows) * count + e // rows
        kpos, qpos = pos[:, None], pos[None, :]
        prev_ok = np.where(kpos >= qpos, 0.0, MASK_VALUE)
        cur_ok = np.where(kpos <= qpos, 0.0, MASK_VALUE)
        kaug.append(np.stack([np.concatenate([none, cur_ok]), np.concatenate([prev_ok, cur_ok])]))
    qaug = np.concatenate([np.eye(blk), np.eye(blk)])
    return jnp.asarray(qaug, BF16), jnp.asarray(np.stack(kaug), BF16)


def _qkv_rope_kernel(*refs):
    n_slabs = D_MODEL // LANES
    x_refs = refs[:n_slabs]
    g_ref, w_ref, cos_ref, sin_ref, o_ref = refs[n_slabs:]
    rows = ROW_TILE // N_RES
    x = jnp.concatenate(
        [jnp.concatenate([ref[pl.ds(r, rows, stride=N_RES), :] for r in range(N_RES)], axis=0) for ref in x_refs],
        axis=1)
    h = _rmsnorm(x, g_ref[...]).astype(BF16)
    cos = cos_ref[...].reshape(ROW_TILE, LANES)
    sin = sin_ref[...].reshape(ROW_TILE, LANES)
    lane = lax.broadcasted_iota(jnp.int32, (ROW_TILE, LANES), 1)
    first_half = (lane % ATTN_HEAD_DIM) < (ATTN_HEAD_DIM // 2)
    width = ATTN_HEADS * ATTN_HEAD_DIM
    for part in range(3):
        y = jnp.dot(h, w_ref[:, part * width:(part + 1) * width], preferred_element_type=F32)
        if part < 2:
            cols = []
            for cb in range(width // LANES):
                t = y[:, cb * LANES:(cb + 1) * LANES]
                partner = jnp.where(first_half,
                                    pltpu.roll(t, LANES - ATTN_HEAD_DIM // 2, 1),
                                    pltpu.roll(t, ATTN_HEAD_DIM // 2, 1))
                t = t * cos + partner * sin
                cols.append(t * (ATTN_HEAD_DIM ** -0.5) if part == 0 else t)
            y = jnp.concatenate(cols, axis=1)
        o_ref[:, :, part * width:(part + 1) * width] = y.reshape(N_RES, rows, width)


def _qkv_rope(x, g, w, cos_t, sin_t, batch, seq):
    d = x.shape[-1]
    n = w.shape[1]
    rows = ROW_TILE // N_RES
    tiles = seq // ROW_TILE
    n_slabs = d // LANES
    x_specs = [pl.BlockSpec((ROW_TILE, LANES), functools.partial(lambda b, c, s: (b * tiles + c, s), s=s))
               for s in range(n_slabs)]
    table_spec = pl.BlockSpec((N_RES, rows, LANES), lambda b, c: (0, c, 0))
    return pl.pallas_call(
        _qkv_rope_kernel,
        grid=(batch, tiles),
        in_specs=x_specs + [_const_spec((1, d)), _const_spec((d, n)), table_spec, table_spec],
        out_specs=pl.BlockSpec((None, N_RES, rows, n), lambda b, c: (b, 0, c, 0)),
        out_shape=jax.ShapeDtypeStruct((batch, N_RES, seq // N_RES, n), F32),
        compiler_params=_params(2),
        name="qkv_rope",
    )(*([x] * n_slabs), g, w, cos_t, sin_t)


def _attn_kernel(q_ref, k_ref, v_ref, qaug_ref, kaug_ref, o_ref,
                 s_buf, m_s, l_s, acc_s, *, seq):
    blk = ATTN_BLOCK
    grp = ATTN_GROUP
    n_groups = (seq // blk) // grp
    assert n_groups % 2 == 0 and seq // N_RES == blk
    last = len(DILATIONS) - 1
    lane = lax.broadcasted_iota(jnp.int32, (blk, LANES), 1)
    first_head = lane < ATTN_HEAD_DIM

    def pick(top, bot):
        return jnp.where(first_head, top, bot)

    def block_id(d, j, u):
        n_blocks = seq // (d * blk)
        if n_blocks == 1:
            return j * grp + u, 0, None, None
        if n_blocks >= grp:
            gpr = n_blocks // grp
            res = 0 if d == 1 else j // gpr
            n = (j % gpr) * grp + u if gpr > 1 else u
        else:
            res = j * (grp // n_blocks) + u // n_blocks
            n = u % n_blocks
        if isinstance(n, int):
            return res, n, max(n - 1, 0), min(n, 1)
        if u > 0:
            return res, n, n - 1, 1
        return res, n, jnp.maximum(n - 1, 0), jnp.minimum(n, 1)

    def seg_rows(d, n):
        _, rows = _segments(d)
        start = n * rows
        return pl.ds(start if isinstance(start, int) else pl.multiple_of(start, rows), rows)

    def load_block(ref, d, res, n):
        count, _ = _segments(d)
        return jnp.concatenate([ref[res + d * i, seg_rows(d, n), :] for i in range(count)], axis=0)

    def store_block(ref, branch, d, res, n, val):
        count, rows = _segments(d)
        for i in range(count):
            ref[branch, res + d * i, seg_rows(d, n), :] = val[i * rows:(i + 1) * rows]

    def scores(branch, d, j, u):
        res, n, n_prev, has_prev = block_id(d, j, u)
        qb = load_block(q_ref, d, res, n)
        q2 = jnp.concatenate([jnp.where(first_head, qb, 0.0), jnp.where(first_head, 0.0, qb)], axis=0).astype(BF16)
        q_ext = jnp.concatenate([q2, qaug_ref[...]], axis=1)
        kc = load_block(k_ref, d, res, n)
        if n_prev is None:
            kk = kc.astype(BF16)
            aug = kaug_ref[branch, 1, blk:2 * blk, :]
        else:
            kk = jnp.concatenate([load_block(k_ref, d, res, n_prev), kc], axis=0).astype(BF16)
            aug = kaug_ref[branch, has_prev]
        k_ext = jnp.concatenate([kk, aug], axis=1)
        s = lax.dot_general(q_ext, k_ext, (((1,), (1,)), ((), ())), preferred_element_type=F32)
        s_buf[j % 2, u, :, 0:s.shape[1]] = s

    def softmax_pv(branch, d, j, u):
        res, n, n_prev, _ = block_id(d, j, u)
        vc = load_block(v_ref, d, res, n)
        if n_prev is None:
            vv = vc.astype(BF16)
        else:
            vv = jnp.concatenate([load_block(v_ref, d, res, n_prev), vc], axis=0).astype(BF16)
        s = s_buf[j % 2, u, :, 0:vv.shape[0]]
        m = jnp.max(s, axis=-1, keepdims=True)
        p = jnp.exp(s - m).astype(BF16)
        v_ext = jnp.concatenate([vv, jnp.ones_like(vv)], axis=1)
        out = jnp.dot(p, v_ext, preferred_element_type=F32)
        mb = jnp.broadcast_to(m, (2 * blk, LANES))
        m_b = pick(mb[:blk], mb[blk:])
        l_b = pick(out[:blk, LANES:], out[blk:, LANES:])
        acc_b = pick(out[:blk, :LANES], out[blk:, :LANES])
        if branch < last:
            store_block(m_s, branch, d, res, n, m_b)
            store_block(l_s, branch, d, res, n, l_b)
            store_block(acc_s, branch, d, res, n, acc_b)
        else:
            ms = [m_s[i, res] for i in range(last)] + [m_b]
            ls = [l_s[i, res] for i in range(last)] + [l_b]
            accs = [acc_s[i, res] for i in range(last)] + [acc_b]
            m_all = functools.reduce(jnp.maximum, ms)
            ws = [jnp.exp(mi - m_all) for mi in ms]
            num = sum(w * a for w, a in zip(ws, accs))
            den = sum(w * l for w, l in zip(ws, ls))
            o_ref[0, pl.ds(res, blk, stride=N_RES), :] = num / den

    for u in range(grp):
        scores(0, DILATIONS[0], 0, u)
    for branch, d in enumerate(DILATIONS):
        def group_body(j, carry, branch=branch, d=d):
            for u in range(grp):
                softmax_pv(branch, d, j, u)
            for u in range(grp):
                scores(branch, d, j + 1, u)
            return carry

        lax.fori_loop(0, n_groups - 1, group_body, 0)
        for u in range(grp):
            softmax_pv(branch, d, n_groups - 1, u)
        if branch + 1 < len(DILATIONS):
            for u in range(grp):
                scores(branch + 1, DILATIONS[branch + 1], 0, u)


def _attention(qkv, batch, seq):
    n_pairs = ATTN_HEADS * ATTN_HEAD_DIM // LANES
    res_major = (N_RES, seq // N_RES, LANES)
    blk_spec = lambda off: pl.BlockSpec((None,) + res_major, lambda b, h: (b, 0, 0, off + h))
    nb = len(DILATIONS) - 1
    qaug, kaug = _attn_consts()
    return pl.pallas_call(
        functools.partial(_attn_kernel, seq=seq),
        grid=(batch, n_pairs),
        in_specs=[blk_spec(0), blk_spec(n_pairs), blk_spec(2 * n_pairs),
                  _const_spec(qaug.shape), _const_spec(kaug.shape)],
        out_specs=pl.BlockSpec((1, seq, LANES), lambda b, h: (b, 0, h)),
        out_shape=jax.ShapeDtypeStruct((batch, seq, D_MODEL), F32),
        scratch_shapes=[pltpu.VMEM((2, ATTN_GROUP, 2 * ATTN_BLOCK, 2 * ATTN_BLOCK), F32),
                        pltpu.VMEM((nb,) + res_major, F32),
                        pltpu.VMEM((nb,) + res_major, F32),
                        pltpu.VMEM((nb,) + res_major, F32)],
        compiler_params=_params(2),
        name="dilated_attention",
    )(qkv, qkv, qkv, qaug, kaug)


def _oproj_ffn_kernel(x_ref, a_ref, wo_ref, g_ref, win_ref, wout_ref, gfin_ref, o_ref, *, final_norm):
    x1 = x_ref[...] + jnp.dot(a_ref[...].astype(BF16), wo_ref[...], preferred_element_type=F32)
    h = _rmsnorm(x1, g_ref[...]).astype(BF16)
    acc = x1
    n_split = 2
    fc = D_FF // n_split
    for c in range(n_split):
        gate = jnp.dot(h, win_ref[:, c * fc:(c + 1) * fc], preferred_element_type=F32)
        up = jnp.dot(h, win_ref[:, D_FF + c * fc:D_FF + (c + 1) * fc], preferred_element_type=F32)
        act = (gate * (1.0 / (1.0 + jnp.exp(-gate))) * up).astype(BF16)
        acc = acc + jnp.dot(act, wout_ref[c * fc:(c + 1) * fc, :], preferred_element_type=F32)
    if final_norm:
        acc = _rmsnorm(acc, gfin_ref[...])
    o_ref[...] = acc


def _oproj_ffn(x, a, wo, g, w_in, w_out, g_fin, final_norm):
    m, d = x.shape
    row_spec = pl.BlockSpec((ROW_TILE, d), lambda i: (i, 0))
    return pl.pallas_call(
        functools.partial(_oproj_ffn_kernel, final_norm=final_norm),
        grid=(m // ROW_TILE,),
        in_specs=[row_spec, row_spec,
                  _const_spec(wo.shape), _const_spec((1, d)),
                  _const_spec(w_in.shape), _const_spec(w_out.shape), _const_spec((1, d))],
        out_specs=row_spec,
        out_shape=jax.ShapeDtypeStruct((m, d), F32),
        compiler_params=_params(1),
        name="oproj_ffn",
    )(x, a, wo, g, w_in, w_out, g_fin)


def _rope_tables(seq):
    half = ATTN_HEAD_DIM // 2
    freqs = ROPE_THETA ** (-jnp.arange(half, dtype=F32) / half)
    ang = jnp.arange(seq, dtype=F32)[:, None] * freqs[None, :]
    cos, sin = jnp.cos(ang), jnp.sin(ang)
    reps = LANES // ATTN_HEAD_DIM
    cos_t = jnp.tile(jnp.concatenate([cos, cos], axis=1), (1, reps))
    sin_t = jnp.tile(jnp.concatenate([-sin, sin], axis=1), (1, reps))
    res_major = lambda t: t.reshape(seq // N_RES, N_RES, LANES).transpose(1, 0, 2)
    return res_major(cos_t), res_major(sin_t)


def kernel(x, mix_norm, ffn_norm, w_in_rec, conv_w, hgrn_lb, hgrn_norm, w_out_rec,
           w_qkv_attn, w_o_attn, w_ffn_in, w_ffn_out, final_norm):
    batch, seq, d = x.shape
    m = batch * seq
    xf = x.reshape(m, d)
    row = lambda v: v.reshape(1, -1)

    proj = _norm_matmul(xf, row(mix_norm[0]), w_in_rec[0].astype(BF16), F32)
    mix = _mixer(proj.reshape(batch, seq, REC_IN_WIDTH), conv_w[0], hgrn_lb, row(hgrn_norm[0]), batch, seq)
    xf = _oproj_ffn(xf, mix.reshape(m, d), w_out_rec[0].astype(BF16), row(ffn_norm[0]),
                    w_ffn_in[0].astype(BF16), w_ffn_out[0].astype(BF16), row(final_norm), False)

    cos_t, sin_t = _rope_tables(seq)
    qkv = _qkv_rope(xf, row(mix_norm[1]), w_qkv_attn[0].astype(BF16), cos_t, sin_t, batch, seq)
    att = _attention(qkv, batch, seq)
    xf = _oproj_ffn(xf, att.reshape(m, d), w_o_attn[0].astype(BF16), row(ffn_norm[1]),
                    w_ffn_in[1].astype(BF16), w_ffn_out[1].astype(BF16), row(final_norm), True)
    return xf.reshape(batch, seq, d)
```

```python
import functools

import jax
import jax.numpy as jnp
import numpy as np
from jax import lax
from jax.experimental import pallas as pl
from jax.experimental.pallas import tpu as pltpu

D_MODEL = 1024
CONV_WIDTH = 3
CONV_CHANNELS = 512
HGRN_HEAD_DIM = 128
HGRN_HEADS = 4
HGRN_WIDTH = 512
ATTN_HEAD_DIM = 64
ATTN_HEADS = 16
ATTN_BLOCK = 128
DILATIONS = (1, 4, 16)
ROPE_THETA = 10000.0
D_FF = 2816
RMS_EPS = 1e-6
REC_IN_WIDTH = 3 * CONV_CHANNELS + 4 * HGRN_WIDTH

LANES = 128
VMEM_LIMIT_BYTES = 56 * 1024 * 1024
ROW_TILE = 512
MIX_TILE = 256
MIX_ROWS = 2
HGRN_CHUNK = 32
MASK_VALUE = -1e30
ATTN_GROUP = 8

F32 = jnp.float32
BF16 = jnp.bfloat16


def _rmsnorm(x, g):
    return x * lax.rsqrt(jnp.mean(x * x, axis=-1, keepdims=True) + RMS_EPS) * g


def _const_spec(shape):
    zeros = (0,) * len(shape)
    return pl.BlockSpec(shape, lambda *_: zeros, pipeline_mode=pl.Buffered(1))


def _params(n_grid):
    return pltpu.CompilerParams(
        dimension_semantics=("arbitrary",) * n_grid, vmem_limit_bytes=VMEM_LIMIT_BYTES)


def _norm_matmul_kernel(x_ref, g_ref, w_ref, o_ref):
    h = _rmsnorm(x_ref[...], g_ref[...]).astype(BF16)
    o_ref[...] = jnp.dot(h, w_ref[...], preferred_element_type=F32).astype(o_ref.dtype)


def _norm_matmul(x, g, w, out_dtype):
    m, d = x.shape
    n = w.shape[1]
    return pl.pallas_call(
        _norm_matmul_kernel,
        grid=(m // ROW_TILE,),
        in_specs=[pl.BlockSpec((ROW_TILE, d), lambda i: (i, 0)),
                  _const_spec((1, d)),
                  _const_spec((d, n))],
        out_specs=pl.BlockSpec((ROW_TILE, n), lambda i: (i, 0)),
        out_shape=jax.ShapeDtypeStruct((m, n), out_dtype),
        compiler_params=_params(1),
        name="norm_matmul",
    )(x, g, w)


def _mixer_consts():
    ts, c = MIX_TILE, HGRN_CHUNK
    t = np.arange(ts)
    same = (t[:, None] // c) == (t[None, :] // c)
    tril = same & (t[:, None] >= t[None, :])
    cum = np.concatenate([tril, same]).astype(np.float32)
    ind = (t[:, None] // c == np.arange(LANES)[None, :]).astype(np.float32)
    return jnp.asarray(cum, BF16), jnp.asarray(ind, BF16)


def _mixer_kernel(proj_ref, convw_ref, lb_ref, hnorm_ref, cum_ref, ind_ref, out_ref,
                  ubuf, state, vblk):
    @pl.when(pl.program_id(1) == 0)
    def _():
        state[...] = jnp.zeros_like(state)
        vblk[...] = jnp.zeros_like(vblk)
        ubuf[:, 0:8, :] = jnp.zeros((MIX_ROWS, 8, CONV_CHANNELS), F32)

    for bb in range(MIX_ROWS):
        _mixer_tile(proj_ref.at[bb], convw_ref, lb_ref, hnorm_ref, cum_ref, ind_ref, out_ref.at[bb],
                    ubuf.at[bb], state.at[bb], vblk.at[bb])


def _mixer_tile(proj_ref, convw_ref, lb_ref, hnorm_ref, cum_ref, ind_ref, out_ref, ubuf, state, vblk):
    ts = MIX_TILE
    cc = CONV_CHANNELS
    c = HGRN_CHUNK
    n_chunks = ts // c

    u = proj_ref[:, cc:2 * cc] * proj_ref[:, 2 * cc:3 * cc]
    ubuf[8:8 + ts, :] = u
    y = (convw_ref[0:1, :] * ubuf[6:6 + ts, :]
         + convw_ref[1:2, :] * ubuf[7:7 + ts, :]
         + convw_ref[2:3, :] * u)
    out_ref[:, 0:cc] = (proj_ref[:, 0:cc] * y).astype(out_ref.dtype)
    ubuf[0:8, :] = u[ts - 8:ts, :]

    lbp = lb_ref[...]
    e = jnp.exp(lbp - jnp.max(lbp, axis=0, keepdims=True))
    sm = e / jnp.sum(e, axis=0, keepdims=True)
    cum0 = sm[0:1, :]
    cum1 = cum0 + sm[1:2, :]
    lb = cum1 - cum0
    q0 = 3 * cc
    z = proj_ref[:, q0 + HGRN_WIDTH:q0 + 2 * HGRN_WIDTH]
    ez = jnp.exp(-jnp.abs(z))
    inv = 1.0 / (1.0 + ez)
    small = ez * inv
    pos = z >= 0.0
    lf = jnp.log(lb + (1.0 - lb) * jnp.where(pos, inv, small))
    kk = (1.0 - lb) * jnp.where(pos, small, inv)

    lf_hi = lf.astype(BF16)
    lf_lo = (lf - lf_hi.astype(F32)).astype(BF16)
    sums = (jnp.dot(cum_ref[...], lf_hi, preferred_element_type=F32)
            + jnp.dot(cum_ref[...], lf_lo, preferred_element_type=F32))
    g = sums[:ts]
    gl = sums[ts:]
    tn = (((0,), (0,)), ((), ()))
    decay_col = jnp.exp(lax.dot_general(lf_hi, ind_ref[...], tn, preferred_element_type=F32)
                        + lax.dot_general(lf_lo, ind_ref[...], tn, preferred_element_type=F32))

    half = 0.5 * gl
    q = proj_ref[:, q0:q0 + HGRN_WIDTH]
    qd = (q * jnp.exp(g - half)).astype(BF16)
    kd = (kk * jnp.exp(half - g)).astype(BF16)
    qg = (q * jnp.exp(g)).astype(BF16)
    kd2 = (kk * jnp.exp(gl - g)).astype(BF16)
    i_in = proj_ref[:, q0 + 2 * HGRN_WIDTH:q0 + 3 * HGRN_WIDTH]
    v = (i_in * (1.0 / (1.0 + jnp.exp(-i_in)))).astype(BF16)

    row = lax.broadcasted_iota(jnp.int32, (ts, ts), 0)
    col = lax.broadcasted_iota(jnp.int32, (ts, ts), 1)
    intra = (row // c == col // c) & (row >= col)
    hnorm = hnorm_ref[...]

    for hd in range(HGRN_HEADS):
        lanes = slice(hd * LANES, (hd + 1) * LANES)
        vh = v[:, lanes]
        a = lax.dot_general(qd[:, lanes], kd[:, lanes], (((1,), (1,)), ((), ())), preferred_element_type=F32)
        o = jnp.dot(jnp.where(intra, a, 0.0).astype(BF16), vh, preferred_element_type=F32)
        for ci in range(n_chunks):
            vblk[hd, ci * c:(ci + 1) * c, ci * LANES:(ci + 1) * LANES] = vh[ci * c:(ci + 1) * c]
        upd = lax.dot_general(kd2[:, lanes], vblk[hd], tn, preferred_element_type=F32)
        st = state[hd]
        inter = []
        for ci in range(n_chunks):
            inter.append(jnp.dot(qg[ci * c:(ci + 1) * c, lanes], st.astype(BF16), preferred_element_type=F32))
            st = st * decay_col[lanes, ci:ci + 1] + upd[:, ci * LANES:(ci + 1) * LANES]
        state[hd] = st
        o = o + jnp.concatenate(inter, axis=0)
        o = _rmsnorm(o, hnorm)
        gh = proj_ref[:, q0 + 3 * HGRN_WIDTH + hd * LANES:q0 + 3 * HGRN_WIDTH + (hd + 1) * LANES]
        o = o * (gh * (1.0 / (1.0 + jnp.exp(-gh))))
        out_ref[:, cc + hd * LANES:cc + (hd + 1) * LANES] = o.astype(out_ref.dtype)


def _mixer(proj, conv_w, hgrn_lb, hgrn_norm, batch, seq):
    ts = MIX_TILE
    cum, ind = _mixer_consts()
    return pl.pallas_call(
        _mixer_kernel,
        grid=(batch // MIX_ROWS, seq // ts),
        in_specs=[pl.BlockSpec((MIX_ROWS, ts, REC_IN_WIDTH), lambda b, j: (b, j, 0)),
                  _const_spec((CONV_WIDTH, CONV_CHANNELS)),
                  _const_spec((3, HGRN_WIDTH)),
                  _const_spec((1, HGRN_HEAD_DIM)),
                  _const_spec(cum.shape), _const_spec(ind.shape)],
        out_specs=pl.BlockSpec((MIX_ROWS, ts, D_MODEL), lambda b, j: (b, j, 0)),
        out_shape=jax.ShapeDtypeStruct((batch, seq, D_MODEL), BF16),
        scratch_shapes=[pltpu.VMEM((MIX_ROWS, ts + 8, CONV_CHANNELS), F32),
                        pltpu.VMEM((MIX_ROWS, HGRN_HEADS, HGRN_HEAD_DIM, HGRN_HEAD_DIM), F32),
                        pltpu.VMEM((MIX_ROWS, HGRN_HEADS, ts, (ts // HGRN_CHUNK) * LANES), BF16)],
        compiler_params=_params(2),
        name="conv_hgrn_mixer",
    )(proj, conv_w, hgrn_lb, hgrn_norm, cum, ind)


N_RES = DILATIONS[-1]
Q_SCALE = ATTN_HEAD_DIM ** -0.5 * float(np.log2(np.e))


def _segments(d):
    count = N_RES // d
    return count, ATTN_BLOCK // count


def _attn_consts():
    blk = ATTN_BLOCK
    e = np.arange(blk)
    none = np.full((blk, blk), MASK_VALUE)
    kaug = []
    for d in DILATIONS:
        count, rows = _segments(d)
        pos = (e % rows) * count + e // rows
        kpos, qpos = pos[:, None], pos[None, :]
        prev_ok = np.where(kpos >= qpos, 0.0, MASK_VALUE)
        cur_ok = np.where(kpos <= qpos, 0.0, MASK_VALUE)
        kaug.append(np.stack([np.concatenate([none, cur_ok]), np.concatenate([prev_ok, cur_ok])]))
    qaug = np.concatenate([np.eye(blk), np.eye(blk)])
    return jnp.asarray(qaug, BF16), jnp.asarray(np.stack(kaug), BF16)


def _qkv_rope_kernel(*refs):
    n_slabs = D_MODEL // LANES
    x_refs = refs[:n_slabs]
    g_ref, w_ref, cos_ref, sin_ref, o_ref = refs[n_slabs:]
    rows = ROW_TILE // N_RES
    x = jnp.concatenate(
        [jnp.concatenate([ref[pl.ds(r, rows, stride=N_RES), :] for r in range(N_RES)], axis=0) for ref in x_refs],
        axis=1)
    h = _rmsnorm(x, g_ref[...]).astype(BF16)
    cos = cos_ref[...].reshape(ROW_TILE, LANES)
    sin = sin_ref[...].reshape(ROW_TILE, LANES)
    lane = lax.broadcasted_iota(jnp.int32, (ROW_TILE, LANES), 1)
    first_half = (lane % ATTN_HEAD_DIM) < (ATTN_HEAD_DIM // 2)
    width = ATTN_HEADS * ATTN_HEAD_DIM
    for part in range(3):
        y = jnp.dot(h, w_ref[:, part * width:(part + 1) * width], preferred_element_type=F32)
        if part < 2:
            cols = []
            for cb in range(width // LANES):
                t = y[:, cb * LANES:(cb + 1) * LANES]
                partner = jnp.where(first_half,
                                    pltpu.roll(t, LANES - ATTN_HEAD_DIM // 2, 1),
                                    pltpu.roll(t, ATTN_HEAD_DIM // 2, 1))
                t = t * cos + partner * sin
                cols.append(t * Q_SCALE if part == 0 else t)
            y = jnp.concatenate(cols, axis=1)
        o_ref[:, :, part * width:(part + 1) * width] = y.reshape(N_RES, rows, width)


def _qkv_rope(x, g, w, cos_t, sin_t, batch, seq):
    d = x.shape[-1]
    n = w.shape[1]
    rows = ROW_TILE // N_RES
    tiles = seq // ROW_TILE
    n_slabs = d // LANES
    x_specs = [pl.BlockSpec((ROW_TILE, LANES), functools.partial(lambda b, c, s: (b * tiles + c, s), s=s))
               for s in range(n_slabs)]
    table_spec = pl.BlockSpec((N_RES, rows, LANES), lambda b, c: (0, c, 0))
    return pl.pallas_call(
        _qkv_rope_kernel,
        grid=(batch, tiles),
        in_specs=x_specs + [_const_spec((1, d)), _const_spec((d, n)), table_spec, table_spec],
        out_specs=pl.BlockSpec((None, N_RES, rows, n), lambda b, c: (b, 0, c, 0)),
        out_shape=jax.ShapeDtypeStruct((batch, N_RES, seq // N_RES, n), F32),
        compiler_params=_params(2),
        name="qkv_rope",
    )(*([x] * n_slabs), g, w, cos_t, sin_t)


def _attn_kernel(q_ref, k_ref, v_ref, qaug_ref, kaug_ref, o_ref,
                 s_buf, m_s, l_s, acc_s, *, seq):
    blk = ATTN_BLOCK
    grp = ATTN_GROUP
    n_groups = (seq // blk) // grp
    assert n_groups % 2 == 0 and seq // N_RES == blk
    last = len(DILATIONS) - 1
    lane = lax.broadcasted_iota(jnp.int32, (blk, LANES), 1)
    first_head = lane < ATTN_HEAD_DIM

    def pick(top, bot):
        return jnp.where(first_head, top, bot)

    def block_id(d, j, u):
        n_blocks = seq // (d * blk)
        if n_blocks == 1:
            return j * grp + u, 0, None, None
        if n_blocks >= grp:
            gpr = n_blocks // grp
            res = 0 if d == 1 else j // gpr
            n = (j % gpr) * grp + u if gpr > 1 else u
        else:
            res = j * (grp // n_blocks) + u // n_blocks
            n = u % n_blocks
        if isinstance(n, int):
            return res, n, max(n - 1, 0), min(n, 1)
        if u > 0:
            return res, n, n - 1, 1
        return res, n, jnp.maximum(n - 1, 0), jnp.minimum(n, 1)

    def seg_rows(d, n):
        _, rows = _segments(d)
        start = n * rows
        return pl.ds(start if isinstance(start, int) else pl.multiple_of(start, rows), rows)

    def load_block(ref, d, res, n):
        count, _ = _segments(d)
        return jnp.concatenate([ref[res + d * i, seg_rows(d, n), :] for i in range(count)], axis=0)

    def store_block(ref, branch, d, res, n, val):
        count, rows = _segments(d)
        for i in range(count):
            ref[branch, res + d * i, seg_rows(d, n), :] = val[i * rows:(i + 1) * rows]

    def scores(branch, d, j, u):
        res, n, n_prev, has_prev = block_id(d, j, u)
        qb = load_block(q_ref, d, res, n)
        q2 = jnp.concatenate([jnp.where(first_head, qb, 0.0), jnp.where(first_head, 0.0, qb)], axis=0).astype(BF16)
        q_ext = jnp.concatenate([q2, qaug_ref[...]], axis=1)
        kc = load_block(k_ref, d, res, n)
        if n_prev is None:
            kk = kc.astype(BF16)
            aug = kaug_ref[branch, 1, blk:2 * blk, :]
        else:
            kk = jnp.concatenate([load_block(k_ref, d, res, n_prev), kc], axis=0).astype(BF16)
            aug = kaug_ref[branch, has_prev]
        k_ext = jnp.concatenate([kk, aug], axis=1)
        s = lax.dot_general(q_ext, k_ext, (((1,), (1,)), ((), ())), preferred_element_type=F32)
        s_buf[j % 2, u, :, 0:s.shape[1]] = s

    def softmax_pv(branch, d, j, u):
        res, n, n_prev, _ = block_id(d, j, u)
        vc = load_block(v_ref, d, res, n)
        if n_prev is None:
            vv = vc.astype(BF16)
        else:
            vv = jnp.concatenate([load_block(v_ref, d, res, n_prev), vc], axis=0).astype(BF16)
        s = s_buf[j % 2, u, :, 0:vv.shape[0]]
        m = jnp.max(s, axis=-1, keepdims=True)
        p = jnp.exp2((s - m).astype(BF16))
        key_first_head = lax.broadcasted_iota(jnp.int32, vv.shape, 1) < ATTN_HEAD_DIM
        one = jnp.ones_like(vv)
        out0 = jnp.dot(p[:blk], jnp.where(key_first_head, vv, one), preferred_element_type=F32)
        out1 = jnp.dot(p[blk:], jnp.where(key_first_head, one, vv), preferred_element_type=F32)
        mb = jnp.broadcast_to(m, (2 * blk, LANES))
        m_b = pick(mb[:blk], mb[blk:])
        acc_b = pick(out0, out1)
        l_b = pltpu.roll(pick(out1, out0), ATTN_HEAD_DIM, 1)
        if branch < last:
            store_block(m_s, branch, d, res, n, m_b)
            store_block(l_s, branch, d, res, n, l_b)
            store_block(acc_s, branch, d, res, n, acc_b)
        else:
            ms = [m_s[i, res] for i in range(last)] + [m_b]
            ls = [l_s[i, res] for i in range(last)] + [l_b]
            accs = [acc_s[i, res] for i in range(last)] + [acc_b]
            m_all = functools.reduce(jnp.maximum, ms)
            ws = [jnp.exp2(mi - m_all) for mi in ms]
            num = sum(w * a for w, a in zip(ws, accs))
            den = sum(w * l for w, l in zip(ws, ls))
            o_ref[0, pl.ds(res, blk, stride=N_RES), :] = num / den

    for u in range(grp):
        scores(0, DILATIONS[0], 0, u)
    for branch, d in enumerate(DILATIONS):
        def group_body(j, carry, branch=branch, d=d):
            for u in range(grp):
                softmax_pv(branch, d, j, u)
            for u in range(grp):
                scores(branch, d, j + 1, u)
            return carry

        lax.fori_loop(0, n_groups - 1, group_body, 0)
        for u in range(grp):
            softmax_pv(branch, d, n_groups - 1, u)
        if branch + 1 < len(DILATIONS):
            for u in range(grp):
                scores(branch + 1, DILATIONS[branch + 1], 0, u)


def _attention(qkv, batch, seq):
    n_pairs = ATTN_HEADS * ATTN_HEAD_DIM // LANES
    res_major = (N_RES, seq // N_RES, LANES)
    blk_spec = lambda off: pl.BlockSpec((None,) + res_major, lambda b, h: (b, 0, 0, off + h))
    nb = len(DILATIONS) - 1
    qaug, kaug = _attn_consts()
    return pl.pallas_call(
        functools.partial(_attn_kernel, seq=seq),
        grid=(batch, n_pairs),
        in_specs=[blk_spec(0), blk_spec(n_pairs), blk_spec(2 * n_pairs),
                  _const_spec(qaug.shape), _const_spec(kaug.shape)],
        out_specs=pl.BlockSpec((1, seq, LANES), lambda b, h: (b, 0, h)),
        out_shape=jax.ShapeDtypeStruct((batch, seq, D_MODEL), F32),
        scratch_shapes=[pltpu.VMEM((2, ATTN_GROUP, 2 * ATTN_BLOCK, 2 * ATTN_BLOCK), F32),
                        pltpu.VMEM((nb,) + res_major, F32),
                        pltpu.VMEM((nb,) + res_major, F32),
                        pltpu.VMEM((nb,) + res_major, F32)],
        compiler_params=_params(2),
        name="dilated_attention",
    )(qkv, qkv, qkv, qaug, kaug)


def _oproj_ffn_kernel(x_ref, a_ref, wo_ref, g_ref, win_ref, wout_ref, gfin_ref, o_ref, *, final_norm):
    x1 = x_ref[...] + jnp.dot(a_ref[...].astype(BF16), wo_ref[...], preferred_element_type=F32)
    h = _rmsnorm(x1, g_ref[...]).astype(BF16)
    acc = x1
    n_split = 2
    fc = D_FF // n_split
    for c in range(n_split):
        gate = jnp.dot(h, win_ref[:, c * fc:(c + 1) * fc], preferred_element_type=F32)
        up = jnp.dot(h, win_ref[:, D_FF + c * fc:D_FF + (c + 1) * fc], preferred_element_type=F32)
        act = (gate * (1.0 / (1.0 + jnp.exp(-gate))) * up).astype(BF16)
        acc = acc + jnp.dot(act, wout_ref[c * fc:(c + 1) * fc, :], preferred_element_type=F32)
    if final_norm:
        acc = _rmsnorm(acc, gfin_ref[...])
    o_ref[...] = acc


def _oproj_ffn(x, a, wo, g, w_in, w_out, g_fin, final_norm):
    m, d = x.shape
    row_spec = pl.BlockSpec((ROW_TILE, d), lambda i: (i, 0))
    return pl.pallas_call(
        functools.partial(_oproj_ffn_kernel, final_norm=final_norm),
        grid=(m // ROW_TILE,),
        in_specs=[row_spec, row_spec,
                  _const_spec(wo.shape), _const_spec((1, d)),
                  _const_spec(w_in.shape), _const_spec(w_out.shape), _const_spec((1, d))],
        out_specs=row_spec,
        out_shape=jax.ShapeDtypeStruct((m, d), F32),
        compiler_params=_params(1),
        name="oproj_ffn",
    )(x, a, wo, g, w_in, w_out, g_fin)


def _rope_tables(seq):
    half = ATTN_HEAD_DIM // 2
    freqs = ROPE_THETA ** (-jnp.arange(half, dtype=F32) / half)
    ang = jnp.arange(seq, dtype=F32)[:, None] * freqs[None, :]
    cos, sin = jnp.cos(ang), jnp.sin(ang)
    reps = LANES // ATTN_HEAD_DIM
    cos_t = jnp.tile(jnp.concatenate([cos, cos], axis=1), (1, reps))
    sin_t = jnp.tile(jnp.concatenate([-sin, sin], axis=1), (1, reps))
    res_major = lambda t: t.reshape(seq // N_RES, N_RES, LANES).transpose(1, 0, 2)
    return res_major(cos_t), res_major(sin_t)


def kernel(x, mix_norm, ffn_norm, w_in_rec, conv_w, hgrn_lb, hgrn_norm, w_out_rec,
           w_qkv_attn, w_o_attn, w_ffn_in, w_ffn_out, final_norm):
    batch, seq, d = x.shape
    m = batch * seq
    xf = x.reshape(m, d)
    row = lambda v: v.reshape(1, -1)

    proj = _norm_matmul(xf, row(mix_norm[0]), w_in_rec[0].astype(BF16), F32)
    mix = _mixer(proj.reshape(batch, seq, REC_IN_WIDTH), conv_w[0], hgrn_lb, row(hgrn_norm[0]), batch, seq)
    xf = _oproj_ffn(xf, mix.reshape(m, d), w_out_rec[0].astype(BF16), row(ffn_norm[0]),
                    w_ffn_in[0].astype(BF16), w_ffn_out[0].astype(BF16), row(final_norm), False)

    cos_t, sin_t = _rope_tables(seq)
    qkv = _qkv_rope(xf, row(mix_norm[1]), w_qkv_attn[0].astype(BF16), cos_t, sin_t, batch, seq)
    att = _attention(qkv, batch, seq)
    xf = _oproj_ffn(xf, att.reshape(m, d), w_o_attn[0].astype(BF16), row(ffn_norm[1]),
                    w_ffn_in[1].astype(BF16), w_ffn_out[1].astype(BF16), row(final_norm), True)
    return xf.reshape(batch, seq, d)
```

```python
import functools

import jax
import jax.numpy as jnp
import numpy as np
from jax import lax
from jax.experimental import pallas as pl
from jax.experimental.pallas import tpu as pltpu

D_MODEL = 1024
CONV_WIDTH = 3
CONV_CHANNELS = 512
HGRN_HEAD_DIM = 128
HGRN_HEADS = 4
HGRN_WIDTH = 512
ATTN_HEAD_DIM = 64
ATTN_HEADS = 16
ATTN_BLOCK = 128
DILATIONS = (1, 4, 16)
ROPE_THETA = 10000.0
D_FF = 2816
RMS_EPS = 1e-6
REC_IN_WIDTH = 3 * CONV_CHANNELS + 4 * HGRN_WIDTH

LANES = 128
VMEM_LIMIT_BYTES = 56 * 1024 * 1024
ROW_TILE = 512
MIX_TILE = 256
MIX_ROWS = 2
HGRN_CHUNK = 32
MASK_VALUE = -1e30
ATTN_GROUP = 8

F32 = jnp.float32
BF16 = jnp.bfloat16


def _rmsnorm(x, g):
    return x * lax.rsqrt(jnp.mean(x * x, axis=-1, keepdims=True) + RMS_EPS) * g


def _const_spec(shape):
    zeros = (0,) * len(shape)
    return pl.BlockSpec(shape, lambda *_: zeros, pipeline_mode=pl.Buffered(1))


def _params(n_grid):
    return pltpu.CompilerParams(
        dimension_semantics=("arbitrary",) * n_grid, vmem_limit_bytes=VMEM_LIMIT_BYTES)


BF16_SUBLANES = 16


def _cast_jobs(jobs, n_steps, step_of):
    in_specs, out_specs, out_shapes, operands = [], [], [], []
    for w, layer in jobs:
        _, rows, cols = w.shape
        n = n_steps
        while rows % (n * BF16_SUBLANES):
            n //= 2
        per = n_steps // n
        in_specs.append(pl.BlockSpec((None, rows // n, cols),
                                     lambda *ids, layer=layer, per=per: (layer, step_of(*ids) // per, 0)))
        out_specs.append(pl.BlockSpec((rows // n, cols), lambda *ids, per=per: (step_of(*ids) // per, 0)))
        out_shapes.append(jax.ShapeDtypeStruct((rows, cols), BF16))
        operands.append(w)
    return in_specs, out_specs, out_shapes, operands


def _split_refs(refs, n_in, n_cast, n_out):
    bounds = np.cumsum([0, n_in, n_cast, n_out, n_cast])
    return [refs[a:b] for a, b in zip(bounds[:-1], bounds[1:])] + [refs[bounds[-1]:]]


def _run_casts(cast_in, cast_out):
    for src, dst in zip(cast_in, cast_out):
        dst[...] = src[...].astype(BF16)


def _norm_matmul_kernel(*refs, n_cast):
    (x_ref, g_ref, w_ref), cast_in, (o_ref,), cast_out, (w_bf,) = _split_refs(refs, 3, n_cast, 1)

    @pl.when(pl.program_id(0) == 0)
    def _():
        w_bf[...] = w_ref[...].astype(BF16)

    h = _rmsnorm(x_ref[...], g_ref[...]).astype(BF16)
    o_ref[...] = jnp.dot(h, w_bf[...], preferred_element_type=F32)
    _run_casts(cast_in, cast_out)


def _norm_matmul(x, g, w, layer, cast):
    m, d = x.shape
    n = w.shape[2]
    steps = m // ROW_TILE
    c_in, c_out, c_shapes, c_ops = _cast_jobs(cast, steps, lambda i: i)
    outs = pl.pallas_call(
        functools.partial(_norm_matmul_kernel, n_cast=len(cast)),
        grid=(steps,),
        in_specs=[pl.BlockSpec((ROW_TILE, d), lambda i: (i, 0)),
                  _const_spec((1, d)),
                  pl.BlockSpec((None, d, n), lambda i: (layer, 0, 0), pipeline_mode=pl.Buffered(1))] + c_in,
        out_specs=[pl.BlockSpec((ROW_TILE, n), lambda i: (i, 0))] + c_out,
        out_shape=[jax.ShapeDtypeStruct((m, n), F32)] + c_shapes,
        scratch_shapes=[pltpu.VMEM((d, n), BF16)],
        compiler_params=_params(1),
        name="norm_matmul",
    )(x, g, w, *c_ops)
    return outs[0], outs[1:]


def _mixer_consts():
    ts, c = MIX_TILE, HGRN_CHUNK
    t = np.arange(ts)
    same = (t[:, None] // c) == (t[None, :] // c)
    tril = same & (t[:, None] >= t[None, :])
    cum = np.concatenate([tril, same]).astype(np.float32)
    ind = (t[:, None] // c == np.arange(LANES)[None, :]).astype(np.float32)
    return jnp.asarray(cum, BF16), jnp.asarray(ind, BF16)


def _mixer_kernel(proj_ref, convw_ref, lb_ref, hnorm_ref, cum_ref, ind_ref, out_ref,
                  ubuf, state, vblk):
    @pl.when(pl.program_id(1) == 0)
    def _():
        state[...] = jnp.zeros_like(state)
        vblk[...] = jnp.zeros_like(vblk)
        ubuf[:, 0:8, :] = jnp.zeros((MIX_ROWS, 8, CONV_CHANNELS), F32)

    for bb in range(MIX_ROWS):
        _mixer_tile(proj_ref.at[bb], convw_ref, lb_ref, hnorm_ref, cum_ref, ind_ref, out_ref.at[bb],
                    ubuf.at[bb], state.at[bb], vblk.at[bb])


def _mixer_tile(proj_ref, convw_ref, lb_ref, hnorm_ref, cum_ref, ind_ref, out_ref, ubuf, state, vblk):
    ts = MIX_TILE
    cc = CONV_CHANNELS
    c = HGRN_CHUNK
    n_chunks = ts // c

    u = proj_ref[:, cc:2 * cc] * proj_ref[:, 2 * cc:3 * cc]
    ubuf[8:8 + ts, :] = u
    y = (convw_ref[0:1, :] * ubuf[6:6 + ts, :]
         + convw_ref[1:2, :] * ubuf[7:7 + ts, :]
         + convw_ref[2:3, :] * u)
    out_ref[:, 0:cc] = (proj_ref[:, 0:cc] * y).astype(out_ref.dtype)
    ubuf[0:8, :] = u[ts - 8:ts, :]

    lbp = lb_ref[...]
    e = jnp.exp(lbp - jnp.max(lbp, axis=0, keepdims=True))
    sm = e / jnp.sum(e, axis=0, keepdims=True)
    cum0 = sm[0:1, :]
    cum1 = cum0 + sm[1:2, :]
    lb = cum1 - cum0
    q0 = 3 * cc
    z = proj_ref[:, q0 + HGRN_WIDTH:q0 + 2 * HGRN_WIDTH]
    ez = jnp.exp(-jnp.abs(z))
    inv = 1.0 / (1.0 + ez)
    small = ez * inv
    pos = z >= 0.0
    lf = jnp.log(lb + (1.0 - lb) * jnp.where(pos, inv, small))
    kk = (1.0 - lb) * jnp.where(pos, small, inv)

    lf_hi = lf.astype(BF16)
    lf_lo = (lf - lf_hi.astype(F32)).astype(BF16)
    sums = (jnp.dot(cum_ref[...], lf_hi, preferred_element_type=F32)
            + jnp.dot(cum_ref[...], lf_lo, preferred_element_type=F32))
    g = sums[:ts]
    gl = sums[ts:]
    tn = (((0,), (0,)), ((), ()))
    decay_col = jnp.exp(lax.dot_general(lf_hi, ind_ref[...], tn, preferred_element_type=F32)
                        + lax.dot_general(lf_lo, ind_ref[...], tn, preferred_element_type=F32))

    half = 0.5 * gl
    q = proj_ref[:, q0:q0 + HGRN_WIDTH]
    qd = (q * jnp.exp(g - half)).astype(BF16)
    kd = (kk * jnp.exp(half - g)).astype(BF16)
    qg = (q * jnp.exp(g)).astype(BF16)
    kd2 = (kk * jnp.exp(gl - g)).astype(BF16)
    i_in = proj_ref[:, q0 + 2 * HGRN_WIDTH:q0 + 3 * HGRN_WIDTH]
    v = (i_in * (1.0 / (1.0 + jnp.exp(-i_in)))).astype(BF16)

    row = lax.broadcasted_iota(jnp.int32, (ts, ts), 0)
    col = lax.broadcasted_iota(jnp.int32, (ts, ts), 1)
    intra = (row // c == col // c) & (row >= col)
    hnorm = hnorm_ref[...]

    for hd in range(HGRN_HEADS):
        lanes = slice(hd * LANES, (hd + 1) * LANES)
        vh = v[:, lanes]
        a = lax.dot_general(qd[:, lanes], kd[:, lanes], (((1,), (1,)), ((), ())), preferred_element_type=F32)
        o = jnp.dot(jnp.where(intra, a, 0.0).astype(BF16), vh, preferred_element_type=F32)
        for ci in range(n_chunks):
            vblk[hd, ci * c:(ci + 1) * c, ci * LANES:(ci + 1) * LANES] = vh[ci * c:(ci + 1) * c]
        upd = lax.dot_general(kd2[:, lanes], vblk[hd], tn, preferred_element_type=F32)
        st = state[hd]
        inter = []
        for ci in range(n_chunks):
            inter.append(jnp.dot(qg[ci * c:(ci + 1) * c, lanes], st.astype(BF16), preferred_element_type=F32))
            st = st * decay_col[lanes, ci:ci + 1] + upd[:, ci * LANES:(ci + 1) * LANES]
        state[hd] = st
        o = o + jnp.concatenate(inter, axis=0)
        o = _rmsnorm(o, hnorm)
        gh = proj_ref[:, q0 + 3 * HGRN_WIDTH + hd * LANES:q0 + 3 * HGRN_WIDTH + (hd + 1) * LANES]
        o = o * (gh * (1.0 / (1.0 + jnp.exp(-gh))))
        out_ref[:, cc + hd * LANES:cc + (hd + 1) * LANES] = o.astype(out_ref.dtype)


def _mixer(proj, conv_w, hgrn_lb, hgrn_norm, batch, seq):
    ts = MIX_TILE
    cum, ind = _mixer_consts()
    return pl.pallas_call(
        _mixer_kernel,
        grid=(batch // MIX_ROWS, seq // ts),
        in_specs=[pl.BlockSpec((MIX_ROWS, ts, REC_IN_WIDTH), lambda b, j: (b, j, 0)),
                  _const_spec((CONV_WIDTH, CONV_CHANNELS)),
                  _const_spec((3, HGRN_WIDTH)),
                  _const_spec((1, HGRN_HEAD_DIM)),
                  _const_spec(cum.shape), _const_spec(ind.shape)],
        out_specs=pl.BlockSpec((MIX_ROWS, ts, D_MODEL), lambda b, j: (b, j, 0)),
        out_shape=jax.ShapeDtypeStruct((batch, seq, D_MODEL), BF16),
        scratch_shapes=[pltpu.VMEM((MIX_ROWS, ts + 8, CONV_CHANNELS), F32),
                        pltpu.VMEM((MIX_ROWS, HGRN_HEADS, HGRN_HEAD_DIM, HGRN_HEAD_DIM), F32),
                        pltpu.VMEM((MIX_ROWS, HGRN_HEADS, ts, (ts // HGRN_CHUNK) * LANES), BF16)],
        compiler_params=_params(2),
        name="conv_hgrn_mixer",
    )(proj, conv_w, hgrn_lb, hgrn_norm, cum, ind)


N_RES = DILATIONS[-1]
Q_SCALE = ATTN_HEAD_DIM ** -0.5 * float(np.log2(np.e))


def _segments(d):
    count = N_RES // d
    return count, ATTN_BLOCK // count


def _attn_consts():
    blk = ATTN_BLOCK
    e = np.arange(blk)
    none = np.full((blk, blk), MASK_VALUE)
    kaug = []
    for d in DILATIONS:
        count, rows = _segments(d)
        pos = (e % rows) * count + e // rows
        kpos, qpos = pos[:, None], pos[None, :]
        prev_ok = np.where(kpos >= qpos, 0.0, MASK_VALUE)
        cur_ok = np.where(kpos <= qpos, 0.0, MASK_VALUE)
        kaug.append(np.stack([np.concatenate([none, cur_ok]), np.concatenate([prev_ok, cur_ok])]))
    qaug = np.concatenate([np.eye(blk), np.eye(blk)])
    return jnp.asarray(qaug, BF16), jnp.asarray(np.stack(kaug), BF16)


def _qkv_rope_kernel(*refs, n_cast):
    n_slabs = D_MODEL // LANES
    ins, cast_in, (o_ref,), cast_out, _ = _split_refs(refs, n_slabs + 4, n_cast, 1)
    x_refs = ins[:n_slabs]
    g_ref, w_ref, cos_ref, sin_ref = ins[n_slabs:]
    _run_casts(cast_in, cast_out)
    rows = ROW_TILE // N_RES
    x = jnp.concatenate(
        [jnp.concatenate([ref[pl.ds(r, rows, stride=N_RES), :] for r in range(N_RES)], axis=0) for ref in x_refs],
        axis=1)
    h = _rmsnorm(x, g_ref[...]).astype(BF16)
    cos = cos_ref[...].reshape(ROW_TILE, LANES)
    sin = sin_ref[...].reshape(ROW_TILE, LANES)
    lane = lax.broadcasted_iota(jnp.int32, (ROW_TILE, LANES), 1)
    first_half = (lane % ATTN_HEAD_DIM) < (ATTN_HEAD_DIM // 2)
    width = ATTN_HEADS * ATTN_HEAD_DIM
    for part in range(3):
        y = jnp.dot(h, w_ref[:, part * width:(part + 1) * width], preferred_element_type=F32)
        if part < 2:
            cols = []
            for cb in range(width // LANES):
                t = y[:, cb * LANES:(cb + 1) * LANES]
                partner = jnp.where(first_half,
                                    pltpu.roll(t, LANES - ATTN_HEAD_DIM // 2, 1),
                                    pltpu.roll(t, ATTN_HEAD_DIM // 2, 1))
                t = t * cos + partner * sin
                cols.append(t * Q_SCALE if part == 0 else t)
            y = jnp.concatenate(cols, axis=1)
        o_ref[:, :, part * width:(part + 1) * width] = y.reshape(N_RES, rows, width)


def _qkv_rope(x, g, w, cos_t, sin_t, batch, seq, cast):
    d = x.shape[-1]
    n = w.shape[1]
    rows = ROW_TILE // N_RES
    tiles = seq // ROW_TILE
    n_slabs = d // LANES
    x_specs = [pl.BlockSpec((ROW_TILE, LANES), functools.partial(lambda b, c, s: (b * tiles + c, s), s=s))
               for s in range(n_slabs)]
    table_spec = pl.BlockSpec((N_RES, rows, LANES), lambda b, c: (0, c, 0))
    c_in, c_out, c_shapes, c_ops = _cast_jobs(cast, batch * tiles, lambda b, c: b * tiles + c)
    outs = pl.pallas_call(
        functools.partial(_qkv_rope_kernel, n_cast=len(cast)),
        grid=(batch, tiles),
        in_specs=x_specs + [_const_spec((1, d)), _const_spec((d, n)), table_spec, table_spec] + c_in,
        out_specs=[pl.BlockSpec((None, N_RES, rows, n), lambda b, c: (b, 0, c, 0))] + c_out,
        out_shape=[jax.ShapeDtypeStruct((batch, N_RES, seq // N_RES, n), F32)] + c_shapes,
        compiler_params=_params(2),
        name="qkv_rope",
    )(*([x] * n_slabs), g, w, cos_t, sin_t, *c_ops)
    return outs[0], outs[1:]


def _attn_kernel(q_ref, k_ref, v_ref, qaug_ref, kaug_ref, o_ref,
                 s_buf, m_s, l_s, acc_s, *, seq):
    blk = ATTN_BLOCK
    grp = ATTN_GROUP
    n_groups = (seq // blk) // grp
    assert n_groups % 2 == 0 and seq // N_RES == blk
    last = len(DILATIONS) - 1
    lane = lax.broadcasted_iota(jnp.int32, (blk, LANES), 1)
    first_head = lane < ATTN_HEAD_DIM

    def pick(top, bot):
        return jnp.where(first_head, top, bot)

    def block_id(d, j, u):
        n_blocks = seq // (d * blk)
        if n_blocks == 1:
            return j * grp + u, 0, None, None
        if n_blocks >= grp:
            gpr = n_blocks // grp
            res = 0 if d == 1 else j // gpr
            n = (j % gpr) * grp + u if gpr > 1 else u
        else:
            res = j * (grp // n_blocks) + u // n_blocks
            n = u % n_blocks
        if isinstance(n, int):
            return res, n, max(n - 1, 0), min(n, 1)
        if u > 0:
            return res, n, n - 1, 1
        return res, n, jnp.maximum(n - 1, 0), jnp.minimum(n, 1)

    def seg_rows(d, n):
        _, rows = _segments(d)
        start = n * rows
        return pl.ds(start if isinstance(start, int) else pl.multiple_of(start, rows), rows)

    def load_block(ref, d, res, n):
        count, _ = _segments(d)
        return jnp.concatenate([ref[res + d * i, seg_rows(d, n), :] for i in range(count)], axis=0)

    def store_block(ref, branch, d, res, n, val):
        count, rows = _segments(d)
        for i in range(count):
            ref[branch, res + d * i, seg_rows(d, n), :] = val[i * rows:(i + 1) * rows]

    def scores(branch, d, j, u):
        res, n, n_prev, has_prev = block_id(d, j, u)
        qb = load_block(q_ref, d, res, n)
        q2 = jnp.concatenate([jnp.where(first_head, qb, 0.0), jnp.where(first_head, 0.0, qb)], axis=0).astype(BF16)
        q_ext = jnp.concatenate([q2, qaug_ref[...]], axis=1)
        kc = load_block(k_ref, d, res, n)
        if n_prev is None:
            kk = kc.astype(BF16)
            aug = kaug_ref[branch, 1, blk:2 * blk, :]
        else:
            kk = jnp.concatenate([load_block(k_ref, d, res, n_prev), kc], axis=0).astype(BF16)
            aug = kaug_ref[branch, has_prev]
        k_ext = jnp.concatenate([kk, aug], axis=1)
        s = lax.dot_general(q_ext, k_ext, (((1,), (1,)), ((), ())), preferred_element_type=F32)
        s_buf[j % 2, u, :, 0:s.shape[1]] = s

    def softmax_pv(branch, d, j, u):
        res, n, n_prev, _ = block_id(d, j, u)
        vc = load_block(v_ref, d, res, n)
        if n_prev is None:
            vv = vc.astype(BF16)
        else:
            vv = jnp.concatenate([load_block(v_ref, d, res, n_prev), vc], axis=0).astype(BF16)
        s = s_buf[j % 2, u, :, 0:vv.shape[0]]
        m = jnp.max(s, axis=-1, keepdims=True)
        p = jnp.exp2((s - m).astype(BF16))
        key_first_head = lax.broadcasted_iota(jnp.int32, vv.shape, 1) < ATTN_HEAD_DIM
        one = jnp.ones_like(vv)
        out0 = jnp.dot(p[:blk], jnp.where(key_first_head, vv, one), preferred_element_type=F32)
        out1 = jnp.dot(p[blk:], jnp.where(key_first_head, one, vv), preferred_element_type=F32)
        mb = jnp.broadcast_to(m, (2 * blk, LANES))
        m_b = pick(mb[:blk], mb[blk:])
        acc_b = pick(out0, out1)
        l_b = pltpu.roll(pick(out1, out0), ATTN_HEAD_DIM, 1)
        if branch < last:
            store_block(m_s, branch, d, res, n, m_b)
            store_block(l_s, branch, d, res, n, l_b)
            store_block(acc_s, branch, d, res, n, acc_b)
        else:
            ms = [m_s[i, res] for i in range(last)] + [m_b]
            ls = [l_s[i, res] for i in range(last)] + [l_b]
            accs = [acc_s[i, res] for i in range(last)] + [acc_b]
            m_all = functools.reduce(jnp.maximum, ms)
            ws = [jnp.exp2(mi - m_all) for mi in ms]
            num = sum(w * a for w, a in zip(ws, accs))
            den = sum(w * l for w, l in zip(ws, ls))
            o_ref[0, pl.ds(res, blk, stride=N_RES), :] = num / den

    for u in range(grp):
        scores(0, DILATIONS[0], 0, u)
    for branch, d in enumerate(DILATIONS):
        def group_body(j, carry, branch=branch, d=d):
            for u in range(grp):
                softmax_pv(branch, d, j, u)
            for u in range(grp):
                scores(branch, d, j + 1, u)
            return carry

        lax.fori_loop(0, n_groups - 1, group_body, 0)
        for u in range(grp):
            softmax_pv(branch, d, n_groups - 1, u)
        if branch + 1 < len(DILATIONS):
            for u in range(grp):
                scores(branch + 1, DILATIONS[branch + 1], 0, u)


def _attention(qkv, batch, seq):
    n_pairs = ATTN_HEADS * ATTN_HEAD_DIM // LANES
    res_major = (N_RES, seq // N_RES, LANES)
    blk_spec = lambda off: pl.BlockSpec((None,) + res_major, lambda b, h: (b, 0, 0, off + h))
    nb = len(DILATIONS) - 1
    qaug, kaug = _attn_consts()
    return pl.pallas_call(
        functools.partial(_attn_kernel, seq=seq),
        grid=(batch, n_pairs),
        in_specs=[blk_spec(0), blk_spec(n_pairs), blk_spec(2 * n_pairs),
                  _const_spec(qaug.shape), _const_spec(kaug.shape)],
        out_specs=pl.BlockSpec((1, seq, LANES), lambda b, h: (b, 0, h)),
        out_shape=jax.ShapeDtypeStruct((batch, seq, D_MODEL), F32),
        scratch_shapes=[pltpu.VMEM((2, ATTN_GROUP, 2 * ATTN_BLOCK, 2 * ATTN_BLOCK), F32),
                        pltpu.VMEM((nb,) + res_major, F32),
                        pltpu.VMEM((nb,) + res_major, F32),
                        pltpu.VMEM((nb,) + res_major, F32)],
        compiler_params=_params(2),
        name="dilated_attention",
    )(qkv, qkv, qkv, qaug, kaug)


def _oproj_ffn_kernel(*refs, final_norm, n_cast):
    ins, cast_in, (o_ref,), cast_out, _ = _split_refs(refs, 7, n_cast, 1)
    x_ref, a_ref, wo_ref, g_ref, win_ref, wout_ref, gfin_ref = ins
    _run_casts(cast_in, cast_out)
    x1 = x_ref[...] + jnp.dot(a_ref[...].astype(BF16), wo_ref[...], preferred_element_type=F32)
    h = _rmsnorm(x1, g_ref[...]).astype(BF16)
    acc = x1
    n_split = 2
    fc = D_FF // n_split
    for c in range(n_split):
        gate = jnp.dot(h, win_ref[:, c * fc:(c + 1) * fc], preferred_element_type=F32)
        up = jnp.dot(h, win_ref[:, D_FF + c * fc:D_FF + (c + 1) * fc], preferred_element_type=F32)
        act = (gate * (1.0 / (1.0 + jnp.exp(-gate))) * up).astype(BF16)
        acc = acc + jnp.dot(act, wout_ref[c * fc:(c + 1) * fc, :], preferred_element_type=F32)
    if final_norm:
        acc = _rmsnorm(acc, gfin_ref[...])
    o_ref[...] = acc


def _oproj_ffn(x, a, wo, g, w_in, w_out, g_fin, final_norm, cast=()):
    m, d = x.shape
    steps = m // ROW_TILE
    row_spec = pl.BlockSpec((ROW_TILE, d), lambda i: (i, 0))
    c_in, c_out, c_shapes, c_ops = _cast_jobs(cast, steps, lambda i: i)
    outs = pl.pallas_call(
        functools.partial(_oproj_ffn_kernel, final_norm=final_norm, n_cast=len(cast)),
        grid=(steps,),
        in_specs=[row_spec, row_spec,
                  _const_spec(wo.shape), _const_spec((1, d)),
                  _const_spec(w_in.shape), _const_spec(w_out.shape), _const_spec((1, d))] + c_in,
        out_specs=[row_spec] + c_out,
        out_shape=[jax.ShapeDtypeStruct((m, d), F32)] + c_shapes,
        compiler_params=_params(1),
        name="oproj_ffn",
    )(x, a, wo, g, w_in, w_out, g_fin, *c_ops)
    return outs[0], outs[1:]


def _rope_tables(seq):
    half = ATTN_HEAD_DIM // 2
    freqs = ROPE_THETA ** (-jnp.arange(half, dtype=F32) / half)
    ang = jnp.arange(seq, dtype=F32)[:, None] * freqs[None, :]
    cos, sin = jnp.cos(ang), jnp.sin(ang)
    reps = LANES // ATTN_HEAD_DIM
    cos_t = jnp.tile(jnp.concatenate([cos, cos], axis=1), (1, reps))
    sin_t = jnp.tile(jnp.concatenate([-sin, sin], axis=1), (1, reps))
    res_major = lambda t: t.reshape(seq // N_RES, N_RES, LANES).transpose(1, 0, 2)
    return res_major(cos_t), res_major(sin_t)


def kernel(x, mix_norm, ffn_norm, w_in_rec, conv_w, hgrn_lb, hgrn_norm, w_out_rec,
           w_qkv_attn, w_o_attn, w_ffn_in, w_ffn_out, final_norm):
    batch, seq, d = x.shape
    m = batch * seq
    xf = x.reshape(m, d)
    row = lambda v: v.reshape(1, -1)

    proj, (wo0, win0, wout0) = _norm_matmul(xf, row(mix_norm[0]), w_in_rec, 0,
                                            cast=[(w_out_rec, 0), (w_ffn_in, 0), (w_ffn_out, 0)])
    mix = _mixer(proj.reshape(batch, seq, REC_IN_WIDTH), conv_w[0], hgrn_lb, row(hgrn_norm[0]), batch, seq)
    xf, (wqkv, wo1) = _oproj_ffn(xf, mix.reshape(m, d), wo0, row(ffn_norm[0]), win0, wout0, row(final_norm), False,
                                 cast=[(w_qkv_attn, 0), (w_o_attn, 0)])

    cos_t, sin_t = _rope_tables(seq)
    qkv, (win1, wout1) = _qkv_rope(xf, row(mix_norm[1]), wqkv, cos_t, sin_t, batch, seq,
                                   cast=[(w_ffn_in, 1), (w_ffn_out, 1)])
    att = _attention(qkv, batch, seq)
    xf, _ = _oproj_ffn(xf, att.reshape(m, d), wo1, row(ffn_norm[1]), win1, wout1, row(final_norm), True)
    return xf.reshape(batch, seq, d)
```

```python
import functools

import jax
import jax.numpy as jnp
import numpy as np
from jax import lax
from jax.experimental import pallas as pl
from jax.experimental.pallas import tpu as pltpu

D_MODEL = 1024
CONV_WIDTH = 3
CONV_CHANNELS = 512
HGRN_HEAD_DIM = 128
HGRN_HEADS = 4
HGRN_WIDTH = 512
ATTN_HEAD_DIM = 64
ATTN_HEADS = 16
ATTN_BLOCK = 128
DILATIONS = (1, 4, 16)
ROPE_THETA = 10000.0
D_FF = 2816
RMS_EPS = 1e-6
REC_IN_WIDTH = 3 * CONV_CHANNELS + 4 * HGRN_WIDTH

LANES = 128
VMEM_LIMIT_BYTES = 56 * 1024 * 1024
ROW_TILE = 512
MIX_TILE = 256
HGRN_CHUNK = 32
MASK_VALUE = -1e30
ATTN_GROUP = 8

F32 = jnp.float32
BF16 = jnp.bfloat16


def _rmsnorm(x, g):
    return x * lax.rsqrt(jnp.mean(x * x, axis=-1, keepdims=True) + RMS_EPS) * g


def _const_spec(shape):
    zeros = (0,) * len(shape)
    return pl.BlockSpec(shape, lambda *_: zeros, pipeline_mode=pl.Buffered(1))


def _params(n_grid):
    return pltpu.CompilerParams(
        dimension_semantics=("arbitrary",) * n_grid, vmem_limit_bytes=VMEM_LIMIT_BYTES)


BF16_SUBLANES = 16


def _cast_jobs(jobs, n_steps, step_of):
    in_specs, out_specs, out_shapes, operands = [], [], [], []
    for w, layer in jobs:
        _, rows, cols = w.shape
        n = n_steps
        while rows % (n * BF16_SUBLANES):
            n //= 2
        per = n_steps // n
        in_specs.append(pl.BlockSpec((None, rows // n, cols),
                                     lambda *ids, layer=layer, per=per: (layer, step_of(*ids) // per, 0)))
        out_specs.append(pl.BlockSpec((rows // n, cols), lambda *ids, per=per: (step_of(*ids) // per, 0)))
        out_shapes.append(jax.ShapeDtypeStruct((rows, cols), BF16))
        operands.append(w)
    return in_specs, out_specs, out_shapes, operands


def _split_refs(refs, n_in, n_cast, n_out):
    bounds = np.cumsum([0, n_in, n_cast, n_out, n_cast])
    return [refs[a:b] for a, b in zip(bounds[:-1], bounds[1:])] + [refs[bounds[-1]:]]


def _run_casts(cast_in, cast_out):
    for src, dst in zip(cast_in, cast_out):
        dst[...] = src[...].astype(BF16)


def _mixer_consts():
    ts, c = MIX_TILE, HGRN_CHUNK
    t = np.arange(ts)
    same = (t[:, None] // c) == (t[None, :] // c)
    tril = same & (t[:, None] >= t[None, :])
    cum = np.concatenate([tril, same]).astype(np.float32)
    ind = (t[:, None] // c == np.arange(LANES)[None, :]).astype(np.float32)
    return jnp.asarray(cum, BF16), jnp.asarray(ind, BF16)


def _proj_mixer_kernel(*refs, n_cast, tiles_per_row):
    ins, cast_in, (out_ref,), cast_out, scratch = _split_refs(refs, 10, n_cast, 1)
    x_first, x_odd, x_even, g_ref, w_ref, convw_ref, lb_ref, hnorm_ref, cum_ref, ind_ref = ins
    w_bf, proj_a, proj_b, ubuf, state, vblk = scratch
    step = pl.program_id(0)
    ts = MIX_TILE

    def project(x_ref, dst):
        h = _rmsnorm(x_ref[...], g_ref[...]).astype(BF16)
        dst[...] = jnp.dot(h, w_bf[...], preferred_element_type=F32)

    @pl.when(step == 0)
    def _():
        w_bf[...] = w_ref[...].astype(BF16)
        project(x_first, proj_a)

    @pl.when(step % (tiles_per_row // 2) == 0)
    def _():
        state[...] = jnp.zeros_like(state)
        vblk[...] = jnp.zeros_like(vblk)
        ubuf[0:8, :] = jnp.zeros((8, CONV_CHANNELS), F32)

    mixer_refs = (convw_ref, lb_ref, hnorm_ref, cum_ref, ind_ref, out_ref)
    project(x_odd, proj_b)
    _mixer_tile(proj_a, *mixer_refs, slice(0, ts), ubuf, state, vblk)
    project(x_even, proj_a)
    _mixer_tile(proj_b, *mixer_refs, slice(ts, 2 * ts), ubuf, state, vblk)
    _run_casts(cast_in, cast_out)


def _mixer_tile(proj_ref, convw_ref, lb_ref, hnorm_ref, cum_ref, ind_ref, out_ref, rows, ubuf, state, vblk):
    ts = MIX_TILE
    cc = CONV_CHANNELS
    c = HGRN_CHUNK
    n_chunks = ts // c

    u = proj_ref[:, cc:2 * cc] * proj_ref[:, 2 * cc:3 * cc]
    ubuf[8:8 + ts, :] = u
    y = (convw_ref[0:1, :] * ubuf[6:6 + ts, :]
         + convw_ref[1:2, :] * ubuf[7:7 + ts, :]
         + convw_ref[2:3, :] * u)
    out_ref[rows, 0:cc] = (proj_ref[:, 0:cc] * y).astype(out_ref.dtype)
    ubuf[0:8, :] = u[ts - 8:ts, :]

    lbp = lb_ref[...]
    e = jnp.exp(lbp - jnp.max(lbp, axis=0, keepdims=True))
    sm = e / jnp.sum(e, axis=0, keepdims=True)
    cum0 = sm[0:1, :]
    cum1 = cum0 + sm[1:2, :]
    lb = cum1 - cum0
    q0 = 3 * cc
    z = proj_ref[:, q0 + HGRN_WIDTH:q0 + 2 * HGRN_WIDTH]
    ez = jnp.exp(-jnp.abs(z))
    inv = 1.0 / (1.0 + ez)
    small = ez * inv
    pos = z >= 0.0
    lf = jnp.log(lb + (1.0 - lb) * jnp.where(pos, inv, small))
    kk = (1.0 - lb) * jnp.where(pos, small, inv)

    lf_hi = lf.astype(BF16)
    lf_lo = (lf - lf_hi.astype(F32)).astype(BF16)
    sums = (jnp.dot(cum_ref[...], lf_hi, preferred_element_type=F32)
            + jnp.dot(cum_ref[...], lf_lo, preferred_element_type=F32))
    g = sums[:ts]
    gl = sums[ts:]
    tn = (((0,), (0,)), ((), ()))
    decay_col = jnp.exp(lax.dot_general(lf_hi, ind_ref[...], tn, preferred_element_type=F32)
                        + lax.dot_general(lf_lo, ind_ref[...], tn, preferred_element_type=F32))

    half = 0.5 * gl
    q = proj_ref[:, q0:q0 + HGRN_WIDTH]
    qd = (q * jnp.exp(g - half)).astype(BF16)
    kd = (kk * jnp.exp(half - g)).astype(BF16)
    qg = (q * jnp.exp(g)).astype(BF16)
    kd2 = (kk * jnp.exp(gl - g)).astype(BF16)
    i_in = proj_ref[:, q0 + 2 * HGRN_WIDTH:q0 + 3 * HGRN_WIDTH]
    v = (i_in * (1.0 / (1.0 + jnp.exp(-i_in)))).astype(BF16)

    row = lax.broadcasted_iota(jnp.int32, (ts, ts), 0)
    col = lax.broadcasted_iota(jnp.int32, (ts, ts), 1)
    intra = (row // c == col // c) & (row >= col)
    hnorm = hnorm_ref[...]

    for hd in range(HGRN_HEADS):
        lanes = slice(hd * LANES, (hd + 1) * LANES)
        vh = v[:, lanes]
        a = lax.dot_general(qd[:, lanes], kd[:, lanes], (((1,), (1,)), ((), ())), preferred_element_type=F32)
        o = jnp.dot(jnp.where(intra, a, 0.0).astype(BF16), vh, preferred_element_type=F32)
        for ci in range(n_chunks):
            vblk[hd, ci * c:(ci + 1) * c, ci * LANES:(ci + 1) * LANES] = vh[ci * c:(ci + 1) * c]
        upd = lax.dot_general(kd2[:, lanes], vblk[hd], tn, preferred_element_type=F32)
        st = state[hd]
        inter = []
        for ci in range(n_chunks):
            inter.append(jnp.dot(qg[ci * c:(ci + 1) * c, lanes], st.astype(BF16), preferred_element_type=F32))
            st = st * decay_col[lanes, ci:ci + 1] + upd[:, ci * LANES:(ci + 1) * LANES]
        state[hd] = st
        o = o + jnp.concatenate(inter, axis=0)
        o = _rmsnorm(o, hnorm)
        gh = proj_ref[:, q0 + 3 * HGRN_WIDTH + hd * LANES:q0 + 3 * HGRN_WIDTH + (hd + 1) * LANES]
        o = o * (gh * (1.0 / (1.0 + jnp.exp(-gh))))
        out_ref[rows, cc + hd * LANES:cc + (hd + 1) * LANES] = o.astype(out_ref.dtype)


def _proj_mixer(x, g, w, layer, conv_w, hgrn_lb, hgrn_norm, batch, seq, cast):
    ts = MIX_TILE
    d = x.shape[1]
    n = w.shape[2]
    tiles_per_row = seq // ts
    n_tiles = batch * tiles_per_row
    steps = n_tiles // 2
    assert tiles_per_row % 2 == 0
    cum, ind = _mixer_consts()
    c_in, c_out, c_shapes, c_ops = _cast_jobs(cast, steps, lambda i: i)
    tile_spec = lambda imap, **kw: pl.BlockSpec((ts, d), imap, **kw)
    outs = pl.pallas_call(
        functools.partial(_proj_mixer_kernel, n_cast=len(cast), tiles_per_row=tiles_per_row),
        grid=(steps,),
        in_specs=[tile_spec(lambda i: (0, 0), pipeline_mode=pl.Buffered(1)),
                  tile_spec(lambda i: (2 * i + 1, 0)),
                  tile_spec(lambda i: (jnp.minimum(2 * i + 2, n_tiles - 1), 0)),
                  _const_spec((1, d)),
                  pl.BlockSpec((None, d, n), lambda i: (layer, 0, 0), pipeline_mode=pl.Buffered(1)),
                  _const_spec((CONV_WIDTH, CONV_CHANNELS)),
                  _const_spec((3, HGRN_WIDTH)),
                  _const_spec((1, HGRN_HEAD_DIM)),
                  _const_spec(cum.shape), _const_spec(ind.shape)] + c_in,
        out_specs=[pl.BlockSpec((2 * ts, D_MODEL), lambda i: (i, 0))] + c_out,
        out_shape=[jax.ShapeDtypeStruct((batch * seq, D_MODEL), BF16)] + c_shapes,
        scratch_shapes=[pltpu.VMEM((d, n), BF16),
                        pltpu.VMEM((ts, n), F32),
                        pltpu.VMEM((ts, n), F32),
                        pltpu.VMEM((ts + 8, CONV_CHANNELS), F32),
                        pltpu.VMEM((HGRN_HEADS, HGRN_HEAD_DIM, HGRN_HEAD_DIM), F32),
                        pltpu.VMEM((HGRN_HEADS, ts, (ts // HGRN_CHUNK) * LANES), BF16)],
        compiler_params=_params(1),
        name="proj_conv_hgrn_mixer",
    )(x, x, x, g, w, conv_w, hgrn_lb, hgrn_norm, cum, ind, *c_ops)
    return outs[0], outs[1:]


N_RES = DILATIONS[-1]
Q_SCALE = ATTN_HEAD_DIM ** -0.5 * float(np.log2(np.e))


def _segments(d):
    count = N_RES // d
    return count, ATTN_BLOCK // count


def _attn_consts():
    blk = ATTN_BLOCK
    e = np.arange(blk)
    none = np.full((blk, blk), MASK_VALUE)
    kaug = []
    for d in DILATIONS:
        count, rows = _segments(d)
        pos = (e % rows) * count + e // rows
        kpos, qpos = pos[:, None], pos[None, :]
        prev_ok = np.where(kpos >= qpos, 0.0, MASK_VALUE)
        cur_ok = np.where(kpos <= qpos, 0.0, MASK_VALUE)
        kaug.append(np.stack([np.concatenate([none, cur_ok]), np.concatenate([prev_ok, cur_ok])]))
    qaug = np.concatenate([np.eye(blk), np.eye(blk)])
    return jnp.asarray(qaug, BF16), jnp.asarray(np.stack(kaug), BF16)


def _qkv_rope_kernel(*refs, n_cast):
    n_slabs = D_MODEL // LANES
    ins, cast_in, (o_ref,), cast_out, _ = _split_refs(refs, n_slabs + 4, n_cast, 1)
    x_refs = ins[:n_slabs]
    g_ref, w_ref, cos_ref, sin_ref = ins[n_slabs:]
    _run_casts(cast_in, cast_out)
    rows = ROW_TILE // N_RES
    x = jnp.concatenate(
        [jnp.concatenate([ref[pl.ds(r, rows, stride=N_RES), :] for r in range(N_RES)], axis=0) for ref in x_refs],
        axis=1)
    h = _rmsnorm(x, g_ref[...]).astype(BF16)
    cos = cos_ref[...].reshape(ROW_TILE, LANES)
    sin = sin_ref[...].reshape(ROW_TILE, LANES)
    lane = lax.broadcasted_iota(jnp.int32, (ROW_TILE, LANES), 1)
    first_half = (lane % ATTN_HEAD_DIM) < (ATTN_HEAD_DIM // 2)
    width = ATTN_HEADS * ATTN_HEAD_DIM
    for part in range(3):
        y = jnp.dot(h, w_ref[:, part * width:(part + 1) * width], preferred_element_type=F32)
        if part < 2:
            cols = []
            for cb in range(width // LANES):
                t = y[:, cb * LANES:(cb + 1) * LANES]
                partner = jnp.where(first_half,
                                    pltpu.roll(t, LANES - ATTN_HEAD_DIM // 2, 1),
                                    pltpu.roll(t, ATTN_HEAD_DIM // 2, 1))
                t = t * cos + partner * sin
                cols.append(t * Q_SCALE if part == 0 else t)
            y = jnp.concatenate(cols, axis=1)
        o_ref[:, :, part * width:(part + 1) * width] = y.reshape(N_RES, rows, width)


def _qkv_rope(x, g, w, cos_t, sin_t, batch, seq, cast):
    d = x.shape[-1]
    n = w.shape[1]
    rows = ROW_TILE // N_RES
    tiles = seq // ROW_TILE
    n_slabs = d // LANES
    x_specs = [pl.BlockSpec((ROW_TILE, LANES), functools.partial(lambda b, c, s: (b * tiles + c, s), s=s))
               for s in range(n_slabs)]
    table_spec = pl.BlockSpec((N_RES, rows, LANES), lambda b, c: (0, c, 0))
    c_in, c_out, c_shapes, c_ops = _cast_jobs(cast, batch * tiles, lambda b, c: b * tiles + c)
    outs = pl.pallas_call(
        functools.partial(_qkv_rope_kernel, n_cast=len(cast)),
        grid=(batch, tiles),
        in_specs=x_specs + [_const_spec((1, d)), _const_spec((d, n)), table_spec, table_spec] + c_in,
        out_specs=[pl.BlockSpec((None, N_RES, rows, n), lambda b, c: (b, 0, c, 0))] + c_out,
        out_shape=[jax.ShapeDtypeStruct((batch, N_RES, seq // N_RES, n), F32)] + c_shapes,
        compiler_params=_params(2),
        name="qkv_rope",
    )(*([x] * n_slabs), g, w, cos_t, sin_t, *c_ops)
    return outs[0], outs[1:]


def _attn_kernel(q_ref, k_ref, v_ref, qaug_ref, kaug_ref, o_ref,
                 s_buf, m_s, l_s, acc_s, *, seq):
    blk = ATTN_BLOCK
    grp = ATTN_GROUP
    n_groups = (seq // blk) // grp
    assert n_groups % 2 == 0 and seq // N_RES == blk
    last = len(DILATIONS) - 1
    lane = lax.broadcasted_iota(jnp.int32, (blk, LANES), 1)
    first_head = lane < ATTN_HEAD_DIM

    def pick(top, bot):
        return jnp.where(first_head, top, bot)

    def block_id(d, j, u):
        n_blocks = seq // (d * blk)
        if n_blocks == 1:
            return j * grp + u, 0, None, None
        if n_blocks >= grp:
            gpr = n_blocks // grp
            res = 0 if d == 1 else j // gpr
            n = (j % gpr) * grp + u if gpr > 1 else u
        else:
            res = j * (grp // n_blocks) + u // n_blocks
            n = u % n_blocks
        if isinstance(n, int):
            return res, n, max(n - 1, 0), min(n, 1)
        if u > 0:
            return res, n, n - 1, 1
        return res, n, jnp.maximum(n - 1, 0), jnp.minimum(n, 1)

    def seg_rows(d, n):
        _, rows = _segments(d)
        start = n * rows
        return pl.ds(start if isinstance(start, int) else pl.multiple_of(start, rows), rows)

    def load_block(ref, d, res, n):
        count, _ = _segments(d)
        return jnp.concatenate([ref[res + d * i, seg_rows(d, n), :] for i in range(count)], axis=0)

    def store_block(ref, branch, d, res, n, val):
        count, rows = _segments(d)
        for i in range(count):
            ref[branch, res + d * i, seg_rows(d, n), :] = val[i * rows:(i + 1) * rows]

    def scores(branch, d, j, u):
        res, n, n_prev, has_prev = block_id(d, j, u)
        qb = load_block(q_ref, d, res, n)
        q2 = jnp.concatenate([jnp.where(first_head, qb, 0.0), jnp.where(first_head, 0.0, qb)], axis=0).astype(BF16)
        q_ext = jnp.concatenate([q2, qaug_ref[...]], axis=1)
        kc = load_block(k_ref, d, res, n)
        if n_prev is None:
            kk = kc.astype(BF16)
            aug = kaug_ref[branch, 1, blk:2 * blk, :]
        else:
            kk = jnp.concatenate([load_block(k_ref, d, res, n_prev), kc], axis=0).astype(BF16)
            aug = kaug_ref[branch, has_prev]
        k_ext = jnp.concatenate([kk, aug], axis=1)
        s = lax.dot_general(q_ext, k_ext, (((1,), (1,)), ((), ())), preferred_element_type=F32)
        s_buf[j % 2, u, :, 0:s.shape[1]] = s

    def softmax_pv(branch, d, j, u):
        res, n, n_prev, _ = block_id(d, j, u)
        vc = load_block(v_ref, d, res, n)
        if n_prev is None:
            vv = vc.astype(BF16)
        else:
            vv = jnp.concatenate([load_block(v_ref, d, res, n_prev), vc], axis=0).astype(BF16)
        s = s_buf[j % 2, u, :, 0:vv.shape[0]]
        m = jnp.max(s, axis=-1, keepdims=True)
        p = jnp.exp2((s - m).astype(BF16))
        key_first_head = lax.broadcasted_iota(jnp.int32, vv.shape, 1) < ATTN_HEAD_DIM
        one = jnp.ones_like(vv)
        out0 = jnp.dot(p[:blk], jnp.where(key_first_head, vv, one), preferred_element_type=F32)
        out1 = jnp.dot(p[blk:], jnp.where(key_first_head, one, vv), preferred_element_type=F32)
        mb = jnp.broadcast_to(m, (2 * blk, LANES))
        m_b = pick(mb[:blk], mb[blk:])
        acc_b = pick(out0, out1)
        l_b = pltpu.roll(pick(out1, out0), ATTN_HEAD_DIM, 1)
        if branch < last:
            store_block(m_s, branch, d, res, n, m_b)
            store_block(l_s, branch, d, res, n, l_b)
            store_block(acc_s, branch, d, res, n, acc_b)
        else:
            ms = [m_s[i, res] for i in range(last)] + [m_b]
            ls = [l_s[i, res] for i in range(last)] + [l_b]
            accs = [acc_s[i, res] for i in range(last)] + [acc_b]
            m_all = functools.reduce(jnp.maximum, ms)
            ws = [jnp.exp2(mi - m_all) for mi in ms]
            num = sum(w * a for w, a in zip(ws, accs))
            den = sum(w * l for w, l in zip(ws, ls))
            o_ref[0, pl.ds(res, blk, stride=N_RES), :] = num / den

    for u in range(grp):
        scores(0, DILATIONS[0], 0, u)
    for branch, d in enumerate(DILATIONS):
        def group_body(j, carry, branch=branch, d=d):
            for u in range(grp):
                softmax_pv(branch, d, j, u)
            for u in range(grp):
                scores(branch, d, j + 1, u)
            return carry

        lax.fori_loop(0, n_groups - 1, group_body, 0)
        for u in range(grp):
            softmax_pv(branch, d, n_groups - 1, u)
        if branch + 1 < len(DILATIONS):
            for u in range(grp):
                scores(branch + 1, DILATIONS[branch + 1], 0, u)


def _attention(qkv, batch, seq):
    n_pairs = ATTN_HEADS * ATTN_HEAD_DIM // LANES
    res_major = (N_RES, seq // N_RES, LANES)
    blk_spec = lambda off: pl.BlockSpec((None,) + res_major, lambda b, h: (b, 0, 0, off + h))
    nb = len(DILATIONS) - 1
    qaug, kaug = _attn_consts()
    return pl.pallas_call(
        functools.partial(_attn_kernel, seq=seq),
        grid=(batch, n_pairs),
        in_specs=[blk_spec(0), blk_spec(n_pairs), blk_spec(2 * n_pairs),
                  _const_spec(qaug.shape), _const_spec(kaug.shape)],
        out_specs=pl.BlockSpec((1, seq, LANES), lambda b, h: (b, 0, h)),
        out_shape=jax.ShapeDtypeStruct((batch, seq, D_MODEL), F32),
        scratch_shapes=[pltpu.VMEM((2, ATTN_GROUP, 2 * ATTN_BLOCK, 2 * ATTN_BLOCK), F32),
                        pltpu.VMEM((nb,) + res_major, F32),
                        pltpu.VMEM((nb,) + res_major, F32),
                        pltpu.VMEM((nb,) + res_major, F32)],
        compiler_params=_params(2),
        name="dilated_attention",
    )(qkv, qkv, qkv, qaug, kaug)


def _oproj_ffn_kernel(*refs, final_norm, n_cast):
    ins, cast_in, (o_ref,), cast_out, _ = _split_refs(refs, 7, n_cast, 1)
    x_ref, a_ref, wo_ref, g_ref, win_ref, wout_ref, gfin_ref = ins
    _run_casts(cast_in, cast_out)
    x1 = x_ref[...] + jnp.dot(a_ref[...].astype(BF16), wo_ref[...], preferred_element_type=F32)
    h = _rmsnorm(x1, g_ref[...]).astype(BF16)
    acc = x1
    n_split = 2
    fc = D_FF // n_split
    for c in range(n_split):
        gate = jnp.dot(h, win_ref[:, c * fc:(c + 1) * fc], preferred_element_type=F32)
        up = jnp.dot(h, win_ref[:, D_FF + c * fc:D_FF + (c + 1) * fc], preferred_element_type=F32)
        act = (gate * (1.0 / (1.0 + jnp.exp(-gate))) * up).astype(BF16)
        acc = acc + jnp.dot(act, wout_ref[c * fc:(c + 1) * fc, :], preferred_element_type=F32)
    if final_norm:
        acc = _rmsnorm(acc, gfin_ref[...])
    o_ref[...] = acc


def _oproj_ffn(x, a, wo, g, w_in, w_out, g_fin, final_norm, cast=()):
    m, d = x.shape
    steps = m // ROW_TILE
    row_spec = pl.BlockSpec((ROW_TILE, d), lambda i: (i, 0))
    c_in, c_out, c_shapes, c_ops = _cast_jobs(cast, steps, lambda i: i)
    outs = pl.pallas_call(
        functools.partial(_oproj_ffn_kernel, final_norm=final_norm, n_cast=len(cast)),
        grid=(steps,),
        in_specs=[row_spec, row_spec,
                  _const_spec(wo.shape), _const_spec((1, d)),
                  _const_spec(w_in.shape), _const_spec(w_out.shape), _const_spec((1, d))] + c_in,
        out_specs=[row_spec] + c_out,
        out_shape=[jax.ShapeDtypeStruct((m, d), F32)] + c_shapes,
        compiler_params=_params(1),
        name="oproj_ffn",
    )(x, a, wo, g, w_in, w_out, g_fin, *c_ops)
    return outs[0], outs[1:]


def _rope_tables(seq):
    half = ATTN_HEAD_DIM // 2
    freqs = ROPE_THETA ** (-jnp.arange(half, dtype=F32) / half)
    ang = jnp.arange(seq, dtype=F32)[:, None] * freqs[None, :]
    cos, sin = jnp.cos(ang), jnp.sin(ang)
    reps = LANES // ATTN_HEAD_DIM
    cos_t = jnp.tile(jnp.concatenate([cos, cos], axis=1), (1, reps))
    sin_t = jnp.tile(jnp.concatenate([-sin, sin], axis=1), (1, reps))
    res_major = lambda t: t.reshape(seq // N_RES, N_RES, LANES).transpose(1, 0, 2)
    return res_major(cos_t), res_major(sin_t)


def kernel(x, mix_norm, ffn_norm, w_in_rec, conv_w, hgrn_lb, hgrn_norm, w_out_rec,
           w_qkv_attn, w_o_attn, w_ffn_in, w_ffn_out, final_norm):
    batch, seq, d = x.shape
    m = batch * seq
    xf = x.reshape(m, d)
    row = lambda v: v.reshape(1, -1)

    mix, (wo0, win0, wout0) = _proj_mixer(xf, row(mix_norm[0]), w_in_rec, 0, conv_w[0], hgrn_lb, row(hgrn_norm[0]),
                                          batch, seq, cast=[(w_out_rec, 0), (w_ffn_in, 0), (w_ffn_out, 0)])
    xf, (wqkv, wo1) = _oproj_ffn(xf, mix, wo0, row(ffn_norm[0]), win0, wout0, row(final_norm), False,
                                 cast=[(w_qkv_attn, 0), (w_o_attn, 0)])

    cos_t, sin_t = _rope_tables(seq)
    qkv, (win1, wout1) = _qkv_rope(xf, row(mix_norm[1]), wqkv, cos_t, sin_t, batch, seq,
                                   cast=[(w_ffn_in, 1), (w_ffn_out, 1)])
    att = _attention(qkv, batch, seq)
    xf, _ = _oproj_ffn(xf, att.reshape(m, d), wo1, row(ffn_norm[1]), win1, wout1, row(final_norm), True)
    return xf.reshape(batch, seq, d)
```

```python
import functools

import jax
import jax.numpy as jnp
import numpy as np
from jax import lax
from jax.experimental import pallas as pl
from jax.experimental.pallas import tpu as pltpu

D_MODEL = 1024
CONV_WIDTH = 3
CONV_CHANNELS = 512
HGRN_HEAD_DIM = 128
HGRN_HEADS = 4
HGRN_WIDTH = 512
ATTN_HEAD_DIM = 64
ATTN_HEADS = 16
ATTN_BLOCK = 128
DILATIONS = (1, 4, 16)
ROPE_THETA = 10000.0
D_FF = 2816
RMS_EPS = 1e-6
REC_IN_WIDTH = 3 * CONV_CHANNELS + 4 * HGRN_WIDTH

LANES = 128
VMEM_LIMIT_BYTES = 56 * 1024 * 1024
ROW_TILE = 512
MIX_TILE = 256
HGRN_CHUNK = 32
MXU_TILE = 256
FFN_CHUNKS = ((0, D_FF),)
MASK_VALUE = -1e30
ATTN_GROUP = 8

F32 = jnp.float32
BF16 = jnp.bfloat16


def _rmsnorm(x, g):
    return x * lax.rsqrt(jnp.mean(x * x, axis=-1, keepdims=True) + RMS_EPS) * g


def _const_spec(shape):
    zeros = (0,) * len(shape)
    return pl.BlockSpec(shape, lambda *_: zeros, pipeline_mode=pl.Buffered(1))


def _params(n_grid):
    return pltpu.CompilerParams(
        dimension_semantics=("arbitrary",) * n_grid, vmem_limit_bytes=VMEM_LIMIT_BYTES)


BF16_SUBLANES = 16


def _cast_jobs(jobs, n_steps, step_of):
    in_specs, out_specs, out_shapes, operands = [], [], [], []
    for w, layer in jobs:
        _, rows, cols = w.shape
        n = n_steps
        while rows % (n * BF16_SUBLANES):
            n //= 2
        per = n_steps // n
        in_specs.append(pl.BlockSpec((None, rows // n, cols),
                                     lambda *ids, layer=layer, per=per: (layer, step_of(*ids) // per, 0)))
        out_specs.append(pl.BlockSpec((rows // n, cols), lambda *ids, per=per: (step_of(*ids) // per, 0)))
        out_shapes.append(jax.ShapeDtypeStruct((rows, cols), BF16))
        operands.append(w)
    return in_specs, out_specs, out_shapes, operands


def _split_refs(refs, n_in, n_cast, n_out):
    bounds = np.cumsum([0, n_in, n_cast, n_out, n_cast])
    return [refs[a:b] for a, b in zip(bounds[:-1], bounds[1:])] + [refs[bounds[-1]:]]


def _run_casts(cast_in, cast_out):
    for src, dst in zip(cast_in, cast_out):
        dst[...] = src[...].astype(BF16)


def _mixer_consts():
    ts, c = MIX_TILE, HGRN_CHUNK
    t = np.arange(ts)
    same = (t[:, None] // c) == (t[None, :] // c)
    tril = same & (t[:, None] >= t[None, :])
    cum = np.concatenate([tril, same]).astype(np.float32)
    ind = (t[:, None] // c == np.arange(LANES)[None, :]).astype(np.float32)
    return jnp.asarray(cum, BF16), jnp.asarray(ind, BF16)


def _proj_mixer_kernel(*refs, n_cast, tiles_per_row):
    ins, cast_in, (out_ref,), cast_out, scratch = _split_refs(refs, 10, n_cast, 1)
    x_first, x_odd, x_even, g_ref, w_ref, convw_ref, lb_ref, hnorm_ref, cum_ref, ind_ref = ins
    w_bf, proj_a, proj_b, ubuf, state, vblk = scratch
    step = pl.program_id(0)
    ts = MIX_TILE

    def project(x_ref, dst):
        h = _rmsnorm(x_ref[...], g_ref[...]).astype(BF16)
        dst[...] = jnp.dot(h, w_bf[...], preferred_element_type=F32)

    @pl.when(step == 0)
    def _():
        w_bf[...] = w_ref[...].astype(BF16)
        project(x_first, proj_a)

    @pl.when(step % (tiles_per_row // 2) == 0)
    def _():
        state[...] = jnp.zeros_like(state)
        vblk[...] = jnp.zeros_like(vblk)
        ubuf[0:8, :] = jnp.zeros((8, CONV_CHANNELS), F32)

    mixer_refs = (convw_ref, lb_ref, hnorm_ref, cum_ref, ind_ref, out_ref)
    project(x_odd, proj_b)
    _mixer_tile(proj_a, *mixer_refs, slice(0, ts), ubuf, state, vblk)
    project(x_even, proj_a)
    _mixer_tile(proj_b, *mixer_refs, slice(ts, 2 * ts), ubuf, state, vblk)
    _run_casts(cast_in, cast_out)


def _mixer_tile(proj_ref, convw_ref, lb_ref, hnorm_ref, cum_ref, ind_ref, out_ref, rows, ubuf, state, vblk):
    ts = MIX_TILE
    cc = CONV_CHANNELS
    c = HGRN_CHUNK
    n_chunks = ts // c

    u = proj_ref[:, cc:2 * cc] * proj_ref[:, 2 * cc:3 * cc]
    ubuf[8:8 + ts, :] = u
    y = (convw_ref[0:1, :] * ubuf[6:6 + ts, :]
         + convw_ref[1:2, :] * ubuf[7:7 + ts, :]
         + convw_ref[2:3, :] * u)
    out_ref[rows, 0:cc] = (proj_ref[:, 0:cc] * y).astype(out_ref.dtype)
    ubuf[0:8, :] = u[ts - 8:ts, :]

    lbp = lb_ref[...]
    e = jnp.exp(lbp - jnp.max(lbp, axis=0, keepdims=True))
    sm = e / jnp.sum(e, axis=0, keepdims=True)
    cum0 = sm[0:1, :]
    cum1 = cum0 + sm[1:2, :]
    lb = cum1 - cum0
    q0 = 3 * cc
    z = proj_ref[:, q0 + HGRN_WIDTH:q0 + 2 * HGRN_WIDTH]
    ez = jnp.exp(-jnp.abs(z))
    inv = 1.0 / (1.0 + ez)
    small = ez * inv
    pos = z >= 0.0
    lf = jnp.log(lb + (1.0 - lb) * jnp.where(pos, inv, small))
    kk = (1.0 - lb) * jnp.where(pos, small, inv)

    lf_hi = lf.astype(BF16)
    lf_lo = (lf - lf_hi.astype(F32)).astype(BF16)
    sums = (jnp.dot(cum_ref[...], lf_hi, preferred_element_type=F32)
            + jnp.dot(cum_ref[...], lf_lo, preferred_element_type=F32))
    g = sums[:ts]
    gl = sums[ts:]
    tn = (((0,), (0,)), ((), ()))
    decay_col = jnp.exp(lax.dot_general(lf_hi, ind_ref[...], tn, preferred_element_type=F32)
                        + lax.dot_general(lf_lo, ind_ref[...], tn, preferred_element_type=F32))

    half = 0.5 * gl
    q = proj_ref[:, q0:q0 + HGRN_WIDTH]
    qd = (q * jnp.exp(g - half)).astype(BF16)
    kd = (kk * jnp.exp(half - g)).astype(BF16)
    qg = (q * jnp.exp(g)).astype(BF16)
    kd2 = (kk * jnp.exp(gl - g)).astype(BF16)
    i_in = proj_ref[:, q0 + 2 * HGRN_WIDTH:q0 + 3 * HGRN_WIDTH]
    v = (i_in * (1.0 / (1.0 + jnp.exp(-i_in)))).astype(BF16)

    row = lax.broadcasted_iota(jnp.int32, (ts, ts), 0)
    col = lax.broadcasted_iota(jnp.int32, (ts, ts), 1)
    intra = (row // c == col // c) & (row >= col)
    hnorm = hnorm_ref[...]

    for hd in range(HGRN_HEADS):
        lanes = slice(hd * LANES, (hd + 1) * LANES)
        vh = v[:, lanes]
        a = lax.dot_general(qd[:, lanes], kd[:, lanes], (((1,), (1,)), ((), ())), preferred_element_type=F32)
        o = jnp.dot(jnp.where(intra, a, 0.0).astype(BF16), vh, preferred_element_type=F32)
        for ci in range(n_chunks):
            vblk[hd, ci * c:(ci + 1) * c, ci * LANES:(ci + 1) * LANES] = vh[ci * c:(ci + 1) * c]
        upd = lax.dot_general(kd2[:, lanes], vblk[hd], tn, preferred_element_type=F32)
        st = state[hd]
        inter = []
        for ci in range(n_chunks):
            inter.append(jnp.dot(qg[ci * c:(ci + 1) * c, lanes], st.astype(BF16), preferred_element_type=F32))
            st = st * decay_col[lanes, ci:ci + 1] + upd[:, ci * LANES:(ci + 1) * LANES]
        state[hd] = st
        o = o + jnp.concatenate(inter, axis=0)
        o = _rmsnorm(o, hnorm)
        gh = proj_ref[:, q0 + 3 * HGRN_WIDTH + hd * LANES:q0 + 3 * HGRN_WIDTH + (hd + 1) * LANES]
        o = o * (gh * (1.0 / (1.0 + jnp.exp(-gh))))
        out_ref[rows, cc + hd * LANES:cc + (hd + 1) * LANES] = o.astype(out_ref.dtype)


def _proj_mixer(x, g, w, layer, conv_w, hgrn_lb, hgrn_norm, batch, seq, cast):
    ts = MIX_TILE
    d = x.shape[1]
    n = w.shape[2]
    tiles_per_row = seq // ts
    n_tiles = batch * tiles_per_row
    steps = n_tiles // 2
    assert tiles_per_row % 2 == 0
    cum, ind = _mixer_consts()
    c_in, c_out, c_shapes, c_ops = _cast_jobs(cast, steps, lambda i: i)
    tile_spec = lambda imap, **kw: pl.BlockSpec((ts, d), imap, **kw)
    outs = pl.pallas_call(
        functools.partial(_proj_mixer_kernel, n_cast=len(cast), tiles_per_row=tiles_per_row),
        grid=(steps,),
        in_specs=[tile_spec(lambda i: (0, 0), pipeline_mode=pl.Buffered(1)),
                  tile_spec(lambda i: (2 * i + 1, 0)),
                  tile_spec(lambda i: (jnp.minimum(2 * i + 2, n_tiles - 1), 0)),
                  _const_spec((1, d)),
                  pl.BlockSpec((None, d, n), lambda i: (layer, 0, 0), pipeline_mode=pl.Buffered(1)),
                  _const_spec((CONV_WIDTH, CONV_CHANNELS)),
                  _const_spec((3, HGRN_WIDTH)),
                  _const_spec((1, HGRN_HEAD_DIM)),
                  _const_spec(cum.shape), _const_spec(ind.shape)] + c_in,
        out_specs=[pl.BlockSpec((2 * ts, D_MODEL), lambda i: (i, 0))] + c_out,
        out_shape=[jax.ShapeDtypeStruct((batch * seq, D_MODEL), BF16)] + c_shapes,
        scratch_shapes=[pltpu.VMEM((d, n), BF16),
                        pltpu.VMEM((ts, n), F32),
                        pltpu.VMEM((ts, n), F32),
                        pltpu.VMEM((ts + 8, CONV_CHANNELS), F32),
                        pltpu.VMEM((HGRN_HEADS, HGRN_HEAD_DIM, HGRN_HEAD_DIM), F32),
                        pltpu.VMEM((HGRN_HEADS, ts, (ts // HGRN_CHUNK) * LANES), BF16)],
        compiler_params=_params(1),
        name="proj_conv_hgrn_mixer",
    )(x, x, x, g, w, conv_w, hgrn_lb, hgrn_norm, cum, ind, *c_ops)
    return outs[0], outs[1:]


N_RES = DILATIONS[-1]
Q_SCALE = ATTN_HEAD_DIM ** -0.5 * float(np.log2(np.e))


def _segments(d):
    count = N_RES // d
    return count, ATTN_BLOCK // count


def _attn_consts():
    blk = ATTN_BLOCK
    e = np.arange(blk)
    none = np.full((blk, blk), MASK_VALUE)
    kaug = []
    for d in DILATIONS:
        count, rows = _segments(d)
        pos = (e % rows) * count + e // rows
        kpos, qpos = pos[:, None], pos[None, :]
        prev_ok = np.where(kpos >= qpos, 0.0, MASK_VALUE)
        cur_ok = np.where(kpos <= qpos, 0.0, MASK_VALUE)
        kaug.append(np.stack([np.concatenate([none, cur_ok]), np.concatenate([prev_ok, cur_ok])]))
    qaug = np.concatenate([np.eye(blk), np.eye(blk)])
    return jnp.asarray(qaug, BF16), jnp.asarray(np.stack(kaug), BF16)


def _qkv_rope_kernel(*refs, n_cast):
    n_slabs = D_MODEL // LANES
    ins, cast_in, (o_ref,), cast_out, _ = _split_refs(refs, n_slabs + 4, n_cast, 1)
    x_refs = ins[:n_slabs]
    g_ref, w_ref, cos_ref, sin_ref = ins[n_slabs:]
    _run_casts(cast_in, cast_out)
    rows = ROW_TILE // N_RES
    x = jnp.concatenate(
        [jnp.concatenate([ref[pl.ds(r, rows, stride=N_RES), :] for r in range(N_RES)], axis=0) for ref in x_refs],
        axis=1)
    h = _rmsnorm(x, g_ref[...]).astype(BF16)
    cos = cos_ref[...].reshape(ROW_TILE, LANES)
    sin = sin_ref[...].reshape(ROW_TILE, LANES)
    lane = lax.broadcasted_iota(jnp.int32, (ROW_TILE, LANES), 1)
    first_half = (lane % ATTN_HEAD_DIM) < (ATTN_HEAD_DIM // 2)
    width = ATTN_HEADS * ATTN_HEAD_DIM
    for part in range(3):
        y = jnp.dot(h, w_ref[:, part * width:(part + 1) * width], preferred_element_type=F32)
        if part < 2:
            cols = []
            for cb in range(width // LANES):
                t = y[:, cb * LANES:(cb + 1) * LANES]
                partner = jnp.where(first_half,
                                    pltpu.roll(t, LANES - ATTN_HEAD_DIM // 2, 1),
                                    pltpu.roll(t, ATTN_HEAD_DIM // 2, 1))
                t = t * cos + partner * sin
                cols.append(t * Q_SCALE if part == 0 else t)
            y = jnp.concatenate(cols, axis=1)
        o_ref[:, :, part * width:(part + 1) * width] = y.reshape(N_RES, rows, width)


def _qkv_rope(x, g, w, cos_t, sin_t, batch, seq, cast):
    d = x.shape[-1]
    n = w.shape[1]
    rows = ROW_TILE // N_RES
    tiles = seq // ROW_TILE
    n_slabs = d // LANES
    x_specs = [pl.BlockSpec((ROW_TILE, LANES), functools.partial(lambda b, c, s: (b * tiles + c, s), s=s))
               for s in range(n_slabs)]
    table_spec = pl.BlockSpec((N_RES, rows, LANES), lambda b, c: (0, c, 0))
    c_in, c_out, c_shapes, c_ops = _cast_jobs(cast, batch * tiles, lambda b, c: b * tiles + c)
    outs = pl.pallas_call(
        functools.partial(_qkv_rope_kernel, n_cast=len(cast)),
        grid=(batch, tiles),
        in_specs=x_specs + [_const_spec((1, d)), _const_spec((d, n)), table_spec, table_spec] + c_in,
        out_specs=[pl.BlockSpec((None, N_RES, rows, n), lambda b, c: (b, 0, c, 0))] + c_out,
        out_shape=[jax.ShapeDtypeStruct((batch, N_RES, seq // N_RES, n), F32)] + c_shapes,
        compiler_params=_params(2),
        name="qkv_rope",
    )(*([x] * n_slabs), g, w, cos_t, sin_t, *c_ops)
    return outs[0], outs[1:]


def _attn_kernel(q_ref, k_ref, v_ref, qaug_ref, kaug_ref, o_ref,
                 s_buf, m_s, l_s, acc_s, *, seq):
    blk = ATTN_BLOCK
    grp = ATTN_GROUP
    n_groups = (seq // blk) // grp
    assert n_groups % 2 == 0 and seq // N_RES == blk
    last = len(DILATIONS) - 1
    lane = lax.broadcasted_iota(jnp.int32, (blk, LANES), 1)
    first_head = lane < ATTN_HEAD_DIM

    def pick(top, bot):
        return jnp.where(first_head, top, bot)

    def block_id(d, j, u):
        n_blocks = seq // (d * blk)
        if n_blocks == 1:
            return j * grp + u, 0, None, None
        if n_blocks >= grp:
            gpr = n_blocks // grp
            res = 0 if d == 1 else j // gpr
            n = (j % gpr) * grp + u if gpr > 1 else u
        else:
            res = j * (grp // n_blocks) + u // n_blocks
            n = u % n_blocks
        if isinstance(n, int):
            return res, n, max(n - 1, 0), min(n, 1)
        if u > 0:
            return res, n, n - 1, 1
        return res, n, jnp.maximum(n - 1, 0), jnp.minimum(n, 1)

    def seg_rows(d, n):
        _, rows = _segments(d)
        start = n * rows
        return pl.ds(start if isinstance(start, int) else pl.multiple_of(start, rows), rows)

    def load_block(ref, d, res, n):
        count, _ = _segments(d)
        return jnp.concatenate([ref[res + d * i, seg_rows(d, n), :] for i in range(count)], axis=0)

    def store_block(ref, branch, d, res, n, val):
        count, rows = _segments(d)
        for i in range(count):
            ref[branch, res + d * i, seg_rows(d, n), :] = val[i * rows:(i + 1) * rows]

    def scores(branch, d, j, u):
        res, n, n_prev, has_prev = block_id(d, j, u)
        qb = load_block(q_ref, d, res, n)
        q2 = jnp.concatenate([jnp.where(first_head, qb, 0.0), jnp.where(first_head, 0.0, qb)], axis=0).astype(BF16)
        q_ext = jnp.concatenate([q2, qaug_ref[...]], axis=1)
        kc = load_block(k_ref, d, res, n)
        if n_prev is None:
            kk = kc.astype(BF16)
            aug = kaug_ref[branch, 1, blk:2 * blk, :]
        else:
            kk = jnp.concatenate([load_block(k_ref, d, res, n_prev), kc], axis=0).astype(BF16)
            aug = kaug_ref[branch, has_prev]
        k_ext = jnp.concatenate([kk, aug], axis=1)
        s = lax.dot_general(q_ext, k_ext, (((1,), (1,)), ((), ())), preferred_element_type=F32)
        s_buf[j % 2, u, :, 0:s.shape[1]] = s

    def softmax_pv(branch, d, j, u):
        res, n, n_prev, _ = block_id(d, j, u)
        vc = load_block(v_ref, d, res, n)
        if n_prev is None:
            vv = vc.astype(BF16)
        else:
            vv = jnp.concatenate([load_block(v_ref, d, res, n_prev), vc], axis=0).astype(BF16)
        s = s_buf[j % 2, u, :, 0:vv.shape[0]]
        m = jnp.max(s, axis=-1, keepdims=True)
        p = jnp.exp2((s - m).astype(BF16))
        key_first_head = lax.broadcasted_iota(jnp.int32, vv.shape, 1) < ATTN_HEAD_DIM
        one = jnp.ones_like(vv)
        out0 = jnp.dot(p[:blk], jnp.where(key_first_head, vv, one), preferred_element_type=F32)
        out1 = jnp.dot(p[blk:], jnp.where(key_first_head, one, vv), preferred_element_type=F32)
        mb = jnp.broadcast_to(m, (2 * blk, LANES))
        m_b = pick(mb[:blk], mb[blk:])
        acc_b = pick(out0, out1)
        l_b = pltpu.roll(pick(out1, out0), ATTN_HEAD_DIM, 1)
        if branch < last:
            store_block(m_s, branch, d, res, n, m_b)
            store_block(l_s, branch, d, res, n, l_b)
            store_block(acc_s, branch, d, res, n, acc_b)
        else:
            ms = [m_s[i, res] for i in range(last)] + [m_b]
            ls = [l_s[i, res] for i in range(last)] + [l_b]
            accs = [acc_s[i, res] for i in range(last)] + [acc_b]
            m_all = functools.reduce(jnp.maximum, ms)
            ws = [jnp.exp2(mi - m_all) for mi in ms]
            num = sum(w * a for w, a in zip(ws, accs))
            den = sum(w * l for w, l in zip(ws, ls))
            o_ref[0, pl.ds(res, blk, stride=N_RES), :] = num / den

    for u in range(grp):
        scores(0, DILATIONS[0], 0, u)
    for branch, d in enumerate(DILATIONS):
        def group_body(j, carry, branch=branch, d=d):
            for u in range(grp):
                softmax_pv(branch, d, j, u)
            for u in range(grp):
                scores(branch, d, j + 1, u)
            return carry

        lax.fori_loop(0, n_groups - 1, group_body, 0)
        for u in range(grp):
            softmax_pv(branch, d, n_groups - 1, u)
        if branch + 1 < len(DILATIONS):
            for u in range(grp):
                scores(branch + 1, DILATIONS[branch + 1], 0, u)


def _attention(qkv, batch, seq):
    n_pairs = ATTN_HEADS * ATTN_HEAD_DIM // LANES
    res_major = (N_RES, seq // N_RES, LANES)
    blk_spec = lambda off: pl.BlockSpec((None,) + res_major, lambda b, h: (b, 0, 0, off + h))
    nb = len(DILATIONS) - 1
    qaug, kaug = _attn_consts()
    return pl.pallas_call(
        functools.partial(_attn_kernel, seq=seq),
        grid=(batch, n_pairs),
        in_specs=[blk_spec(0), blk_spec(n_pairs), blk_spec(2 * n_pairs),
                  _const_spec(qaug.shape), _const_spec(kaug.shape)],
        out_specs=pl.BlockSpec((1, seq, LANES), lambda b, h: (b, 0, h)),
        out_shape=jax.ShapeDtypeStruct((batch, seq, D_MODEL), F32),
        scratch_shapes=[pltpu.VMEM((2, ATTN_GROUP, 2 * ATTN_BLOCK, 2 * ATTN_BLOCK), F32),
                        pltpu.VMEM((nb,) + res_major, F32),
                        pltpu.VMEM((nb,) + res_major, F32),
                        pltpu.VMEM((nb,) + res_major, F32)],
        compiler_params=_params(2),
        name="dilated_attention",
    )(qkv, qkv, qkv, qaug, kaug)


def _oproj_ffn_kernel(*refs, final_norm, n_cast):
    ins, cast_in, (o_ref,), cast_out, _ = _split_refs(refs, 7, n_cast, 1)
    x_ref, a_ref, wo_ref, g_ref, win_ref, wout_ref, gfin_ref = ins
    _run_casts(cast_in, cast_out)
    x1 = x_ref[...] + jnp.dot(a_ref[...].astype(BF16), wo_ref[...], preferred_element_type=F32)
    h = _rmsnorm(x1, g_ref[...]).astype(BF16)
    acc = x1
    for lo, hi in FFN_CHUNKS:
        gate = jnp.dot(h, win_ref[:, lo:hi], preferred_element_type=F32)
        up = jnp.dot(h, win_ref[:, D_FF + lo:D_FF + hi], preferred_element_type=F32)
        act = (gate * (1.0 / (1.0 + jnp.exp(-gate))) * up).astype(BF16)
        acc = acc + jnp.dot(act, wout_ref[lo:hi, :], preferred_element_type=F32)
    if final_norm:
        acc = _rmsnorm(acc, gfin_ref[...])
    o_ref[...] = acc


def _oproj_ffn(x, a, wo, g, w_in, w_out, g_fin, final_norm, cast=()):
    m, d = x.shape
    steps = m // ROW_TILE
    row_spec = pl.BlockSpec((ROW_TILE, d), lambda i: (i, 0))
    c_in, c_out, c_shapes, c_ops = _cast_jobs(cast, steps, lambda i: i)
    outs = pl.pallas_call(
        functools.partial(_oproj_ffn_kernel, final_norm=final_norm, n_cast=len(cast)),
        grid=(steps,),
        in_specs=[row_spec, row_spec,
                  _const_spec(wo.shape), _const_spec((1, d)),
                  _const_spec(w_in.shape), _const_spec(w_out.shape), _const_spec((1, d))] + c_in,
        out_specs=[row_spec] + c_out,
        out_shape=[jax.ShapeDtypeStruct((m, d), F32)] + c_shapes,
        compiler_params=_params(1),
        name="oproj_ffn",
    )(x, a, wo, g, w_in, w_out, g_fin, *c_ops)
    return outs[0], outs[1:]


def _rope_tables(seq):
    half = ATTN_HEAD_DIM // 2
    freqs = ROPE_THETA ** (-jnp.arange(half, dtype=F32) / half)
    ang = jnp.arange(seq, dtype=F32)[:, None] * freqs[None, :]
    cos, sin = jnp.cos(ang), jnp.sin(ang)
    reps = LANES // ATTN_HEAD_DIM
    cos_t = jnp.tile(jnp.concatenate([cos, cos], axis=1), (1, reps))
    sin_t = jnp.tile(jnp.concatenate([-sin, sin], axis=1), (1, reps))
    res_major = lambda t: t.reshape(seq // N_RES, N_RES, LANES).transpose(1, 0, 2)
    return res_major(cos_t), res_major(sin_t)


def kernel(x, mix_norm, ffn_norm, w_in_rec, conv_w, hgrn_lb, hgrn_norm, w_out_rec,
           w_qkv_attn, w_o_attn, w_ffn_in, w_ffn_out, final_norm):
    batch, seq, d = x.shape
    m = batch * seq
    xf = x.reshape(m, d)
    row = lambda v: v.reshape(1, -1)

    mix, (wo0, win0, wout0) = _proj_mixer(xf, row(mix_norm[0]), w_in_rec, 0, conv_w[0], hgrn_lb, row(hgrn_norm[0]),
                                          batch, seq, cast=[(w_out_rec, 0), (w_ffn_in, 0), (w_ffn_out, 0)])
    xf, (wqkv, wo1) = _oproj_ffn(xf, mix, wo0, row(ffn_norm[0]), win0, wout0, row(final_norm), False,
                                 cast=[(w_qkv_attn, 0), (w_o_attn, 0)])

    cos_t, sin_t = _rope_tables(seq)
    qkv, (win1, wout1) = _qkv_rope(xf, row(mix_norm[1]), wqkv, cos_t, sin_t, batch, seq,
                                   cast=[(w_ffn_in, 1), (w_ffn_out, 1)])
    att = _attention(qkv, batch, seq)
    xf, _ = _oproj_ffn(xf, att.reshape(m, d), wo1, row(ffn_norm[1]), win1, wout1, row(final_norm), True)
    return xf.reshape(batch, seq, d)
```

```python
import functools

import jax
import jax.numpy as jnp
import numpy as np
from jax import lax
from jax.experimental import pallas as pl
from jax.experimental.pallas import tpu as pltpu

D_MODEL = 1024
CONV_WIDTH = 3
CONV_CHANNELS = 512
HGRN_HEAD_DIM = 128
HGRN_HEADS = 4
HGRN_WIDTH = 512
ATTN_HEAD_DIM = 64
ATTN_HEADS = 16
ATTN_BLOCK = 128
DILATIONS = (1, 4, 16)
ROPE_THETA = 10000.0
D_FF = 2816
RMS_EPS = 1e-6
REC_IN_WIDTH = 3 * CONV_CHANNELS + 4 * HGRN_WIDTH

LANES = 128
VMEM_LIMIT_BYTES = 56 * 1024 * 1024
ROW_TILE = 512
QKV_SUBTILES = 2
MIX_TILE = 256
HGRN_CHUNK = 32
MXU_TILE = 256
FFN_CHUNKS = ((0, D_FF),)
MASK_VALUE = -1e30
ATTN_GROUP = 8

F32 = jnp.float32
BF16 = jnp.bfloat16


def _rmsnorm(x, g):
    return x * lax.rsqrt(jnp.mean(x * x, axis=-1, keepdims=True) + RMS_EPS) * g


def _const_spec(shape):
    zeros = (0,) * len(shape)
    return pl.BlockSpec(shape, lambda *_: zeros, pipeline_mode=pl.Buffered(1))


def _params(n_grid):
    return pltpu.CompilerParams(
        dimension_semantics=("arbitrary",) * n_grid, vmem_limit_bytes=VMEM_LIMIT_BYTES)


BF16_SUBLANES = 16


def _cast_jobs(jobs, n_steps, step_of):
    in_specs, out_specs, out_shapes, operands = [], [], [], []
    for w, layer in jobs:
        _, rows, cols = w.shape
        n = n_steps
        while rows % (n * BF16_SUBLANES):
            n //= 2
        per = n_steps // n
        in_specs.append(pl.BlockSpec((None, rows // n, cols),
                                     lambda *ids, layer=layer, per=per: (layer, step_of(*ids) // per, 0)))
        out_specs.append(pl.BlockSpec((rows // n, cols), lambda *ids, per=per: (step_of(*ids) // per, 0)))
        out_shapes.append(jax.ShapeDtypeStruct((rows, cols), BF16))
        operands.append(w)
    return in_specs, out_specs, out_shapes, operands


def _split_refs(refs, n_in, n_cast, n_out):
    bounds = np.cumsum([0, n_in, n_cast, n_out, n_cast])
    return [refs[a:b] for a, b in zip(bounds[:-1], bounds[1:])] + [refs[bounds[-1]:]]


def _run_casts(cast_in, cast_out):
    for src, dst in zip(cast_in, cast_out):
        dst[...] = src[...].astype(BF16)


def _mixer_consts():
    ts, c = MIX_TILE, HGRN_CHUNK
    t = np.arange(ts)
    same = (t[:, None] // c) == (t[None, :] // c)
    tril = same & (t[:, None] >= t[None, :])
    cum = np.concatenate([tril, same]).astype(np.float32)
    ind = (t[:, None] // c == np.arange(LANES)[None, :]).astype(np.float32)
    return jnp.asarray(cum, BF16), jnp.asarray(ind, BF16)


def _proj_mixer_kernel(*refs, n_cast, tiles_per_row):
    ins, cast_in, (out_ref,), cast_out, scratch = _split_refs(refs, 10, n_cast, 1)
    x_first, x_odd, x_even, g_ref, w_ref, convw_ref, lb_ref, hnorm_ref, cum_ref, ind_ref = ins
    w_bf, proj_a, proj_b, ubuf, state, vblk = scratch
    step = pl.program_id(0)
    ts = MIX_TILE

    def project(x_ref, dst):
        h = _rmsnorm(x_ref[...], g_ref[...]).astype(BF16)
        dst[...] = jnp.dot(h, w_bf[...], preferred_element_type=F32)

    @pl.when(step == 0)
    def _():
        w_bf[...] = w_ref[...].astype(BF16)
        project(x_first, proj_a)

    @pl.when(step % (tiles_per_row // 2) == 0)
    def _():
        state[...] = jnp.zeros_like(state)
        vblk[...] = jnp.zeros_like(vblk)
        ubuf[0:8, :] = jnp.zeros((8, CONV_CHANNELS), F32)

    mixer_refs = (convw_ref, lb_ref, hnorm_ref, cum_ref, ind_ref, out_ref)
    project(x_odd, proj_b)
    _mixer_tile(proj_a, *mixer_refs, slice(0, ts), ubuf, state, vblk)
    project(x_even, proj_a)
    _mixer_tile(proj_b, *mixer_refs, slice(ts, 2 * ts), ubuf, state, vblk)
    _run_casts(cast_in, cast_out)


def _mixer_tile(proj_ref, convw_ref, lb_ref, hnorm_ref, cum_ref, ind_ref, out_ref, rows, ubuf, state, vblk):
    ts = MIX_TILE
    cc = CONV_CHANNELS
    c = HGRN_CHUNK
    n_chunks = ts // c

    u = proj_ref[:, cc:2 * cc] * proj_ref[:, 2 * cc:3 * cc]
    ubuf[8:8 + ts, :] = u
    y = (convw_ref[0:1, :] * ubuf[6:6 + ts, :]
         + convw_ref[1:2, :] * ubuf[7:7 + ts, :]
         + convw_ref[2:3, :] * u)
    out_ref[rows, 0:cc] = (proj_ref[:, 0:cc] * y).astype(out_ref.dtype)
    ubuf[0:8, :] = u[ts - 8:ts, :]

    lbp = lb_ref[...]
    e = jnp.exp(lbp - jnp.max(lbp, axis=0, keepdims=True))
    sm = e / jnp.sum(e, axis=0, keepdims=True)
    cum0 = sm[0:1, :]
    cum1 = cum0 + sm[1:2, :]
    lb = cum1 - cum0
    q0 = 3 * cc
    z = proj_ref[:, q0 + HGRN_WIDTH:q0 + 2 * HGRN_WIDTH]
    ez = jnp.exp(-jnp.abs(z))
    inv = 1.0 / (1.0 + ez)
    small = ez * inv
    pos = z >= 0.0
    lf = jnp.log(lb + (1.0 - lb) * jnp.where(pos, inv, small))
    kk = (1.0 - lb) * jnp.where(pos, small, inv)

    lf_hi = lf.astype(BF16)
    lf_lo = (lf - lf_hi.astype(F32)).astype(BF16)
    sums = (jnp.dot(cum_ref[...], lf_hi, preferred_element_type=F32)
            + jnp.dot(cum_ref[...], lf_lo, preferred_element_type=F32))
    g = sums[:ts]
    gl = sums[ts:]
    tn = (((0,), (0,)), ((), ()))
    decay_col = jnp.exp(lax.dot_general(lf_hi, ind_ref[...], tn, preferred_element_type=F32)
                        + lax.dot_general(lf_lo, ind_ref[...], tn, preferred_element_type=F32))

    half = 0.5 * gl
    q = proj_ref[:, q0:q0 + HGRN_WIDTH]
    qd = (q * jnp.exp(g - half)).astype(BF16)
    kd = (kk * jnp.exp(half - g)).astype(BF16)
    qg = (q * jnp.exp(g)).astype(BF16)
    kd2 = (kk * jnp.exp(gl - g)).astype(BF16)
    i_in = proj_ref[:, q0 + 2 * HGRN_WIDTH:q0 + 3 * HGRN_WIDTH]
    v = (i_in * (1.0 / (1.0 + jnp.exp(-i_in)))).astype(BF16)

    row = lax.broadcasted_iota(jnp.int32, (ts, ts), 0)
    col = lax.broadcasted_iota(jnp.int32, (ts, ts), 1)
    intra = (row // c == col // c) & (row >= col)
    hnorm = hnorm_ref[...]

    for hd in range(HGRN_HEADS):
        lanes = slice(hd * LANES, (hd + 1) * LANES)
        vh = v[:, lanes]
        a = lax.dot_general(qd[:, lanes], kd[:, lanes], (((1,), (1,)), ((), ())), preferred_element_type=F32)
        o = jnp.dot(jnp.where(intra, a, 0.0).astype(BF16), vh, preferred_element_type=F32)
        for ci in range(n_chunks):
            vblk[hd, ci * c:(ci + 1) * c, ci * LANES:(ci + 1) * LANES] = vh[ci * c:(ci + 1) * c]
        upd = lax.dot_general(kd2[:, lanes], vblk[hd], tn, preferred_element_type=F32)
        st = state[hd]
        inter = []
        for ci in range(n_chunks):
            inter.append(jnp.dot(qg[ci * c:(ci + 1) * c, lanes], st.astype(BF16), preferred_element_type=F32))
            st = st * decay_col[lanes, ci:ci + 1] + upd[:, ci * LANES:(ci + 1) * LANES]
        state[hd] = st
        o = o + jnp.concatenate(inter, axis=0)
        o = _rmsnorm(o, hnorm)
        gh = proj_ref[:, q0 + 3 * HGRN_WIDTH + hd * LANES:q0 + 3 * HGRN_WIDTH + (hd + 1) * LANES]
        o = o * (gh * (1.0 / (1.0 + jnp.exp(-gh))))
        out_ref[rows, cc + hd * LANES:cc + (hd + 1) * LANES] = o.astype(out_ref.dtype)


def _proj_mixer(x, g, w, layer, conv_w, hgrn_lb, hgrn_norm, batch, seq, cast):
    ts = MIX_TILE
    d = x.shape[1]
    n = w.shape[2]
    tiles_per_row = seq // ts
    n_tiles = batch * tiles_per_row
    steps = n_tiles // 2
    assert tiles_per_row % 2 == 0
    cum, ind = _mixer_consts()
    c_in, c_out, c_shapes, c_ops = _cast_jobs(cast, steps, lambda i: i)
    tile_spec = lambda imap, **kw: pl.BlockSpec((ts, d), imap, **kw)
    outs = pl.pallas_call(
        functools.partial(_proj_mixer_kernel, n_cast=len(cast), tiles_per_row=tiles_per_row),
        grid=(steps,),
        in_specs=[tile_spec(lambda i: (0, 0), pipeline_mode=pl.Buffered(1)),
                  tile_spec(lambda i: (2 * i + 1, 0)),
                  tile_spec(lambda i: (jnp.minimum(2 * i + 2, n_tiles - 1), 0)),
                  _const_spec((1, d)),
                  pl.BlockSpec((None, d, n), lambda i: (layer, 0, 0), pipeline_mode=pl.Buffered(1)),
                  _const_spec((CONV_WIDTH, CONV_CHANNELS)),
                  _const_spec((3, HGRN_WIDTH)),
                  _const_spec((1, HGRN_HEAD_DIM)),
                  _const_spec(cum.shape), _const_spec(ind.shape)] + c_in,
        out_specs=[pl.BlockSpec((2 * ts, D_MODEL), lambda i: (i, 0))] + c_out,
        out_shape=[jax.ShapeDtypeStruct((batch * seq, D_MODEL), BF16)] + c_shapes,
        scratch_shapes=[pltpu.VMEM((d, n), BF16),
                        pltpu.VMEM((ts, n), F32),
                        pltpu.VMEM((ts, n), F32),
                        pltpu.VMEM((ts + 8, CONV_CHANNELS), F32),
                        pltpu.VMEM((HGRN_HEADS, HGRN_HEAD_DIM, HGRN_HEAD_DIM), F32),
                        pltpu.VMEM((HGRN_HEADS, ts, (ts // HGRN_CHUNK) * LANES), BF16)],
        compiler_params=_params(1),
        name="proj_conv_hgrn_mixer",
    )(x, x, x, g, w, conv_w, hgrn_lb, hgrn_norm, cum, ind, *c_ops)
    return outs[0], outs[1:]


N_RES = DILATIONS[-1]
Q_SCALE = ATTN_HEAD_DIM ** -0.5 * float(np.log2(np.e))


def _segments(d):
    count = N_RES // d
    return count, ATTN_BLOCK // count


def _attn_consts():
    blk = ATTN_BLOCK
    e = np.arange(blk)
    none = np.full((blk, blk), MASK_VALUE)
    kaug = []
    for d in DILATIONS:
        count, rows = _segments(d)
        pos = (e % rows) * count + e // rows
        kpos, qpos = pos[:, None], pos[None, :]
        prev_ok = np.where(kpos >= qpos, 0.0, MASK_VALUE)
        cur_ok = np.where(kpos <= qpos, 0.0, MASK_VALUE)
        kaug.append(np.stack([np.concatenate([none, cur_ok]), np.concatenate([prev_ok, cur_ok])]))
    qaug = np.concatenate([np.eye(blk), np.eye(blk)])
    return jnp.asarray(qaug, BF16), jnp.asarray(np.stack(kaug), BF16)


def _qkv_rope_kernel(*refs, n_cast):
    n_slabs = D_MODEL // LANES
    ins, cast_in, (o_ref,), cast_out, _ = _split_refs(refs, n_slabs + 4, n_cast, 1)
    x_refs = ins[:n_slabs]
    g_ref, w_ref, cos_ref, sin_ref = ins[n_slabs:]
    _run_casts(cast_in, cast_out)
    rows = ROW_TILE // N_RES
    lane = lax.broadcasted_iota(jnp.int32, (ROW_TILE, LANES), 1)
    first_half = (lane % ATTN_HEAD_DIM) < (ATTN_HEAD_DIM // 2)
    width = ATTN_HEADS * ATTN_HEAD_DIM
    for sub in range(QKV_SUBTILES):
        x = jnp.concatenate(
            [jnp.concatenate([ref[pl.ds(sub * ROW_TILE + r, rows, stride=N_RES), :] for r in range(N_RES)], axis=0)
             for ref in x_refs], axis=1)
        h = _rmsnorm(x, g_ref[...]).astype(BF16)
        out_rows = slice(sub * rows, (sub + 1) * rows)
        cos = cos_ref[:, out_rows, :].reshape(ROW_TILE, LANES)
        sin = sin_ref[:, out_rows, :].reshape(ROW_TILE, LANES)
        for part in range(3):
            y = jnp.dot(h, w_ref[:, part * width:(part + 1) * width], preferred_element_type=F32)
            if part < 2:
                cols = []
                for cb in range(width // LANES):
                    t = y[:, cb * LANES:(cb + 1) * LANES]
                    partner = jnp.where(first_half,
                                        pltpu.roll(t, LANES - ATTN_HEAD_DIM // 2, 1),
                                        pltpu.roll(t, ATTN_HEAD_DIM // 2, 1))
                    t = t * cos + partner * sin
                    cols.append(t * Q_SCALE if part == 0 else t)
                y = jnp.concatenate(cols, axis=1)
            o_ref[:, out_rows, part * width:(part + 1) * width] = y.reshape(N_RES, rows, width)


def _qkv_rope(x, g, w, cos_t, sin_t, batch, seq, cast):
    d = x.shape[-1]
    n = w.shape[1]
    step_rows = QKV_SUBTILES * ROW_TILE
    rows = step_rows // N_RES
    tiles = seq // step_rows
    n_slabs = d // LANES
    x_specs = [pl.BlockSpec((step_rows, LANES), functools.partial(lambda b, c, s: (b * tiles + c, s), s=s))
               for s in range(n_slabs)]
    table_spec = pl.BlockSpec((N_RES, rows, LANES), lambda b, c: (0, c, 0))
    c_in, c_out, c_shapes, c_ops = _cast_jobs(cast, batch * tiles, lambda b, c: b * tiles + c)
    outs = pl.pallas_call(
        functools.partial(_qkv_rope_kernel, n_cast=len(cast)),
        grid=(batch, tiles),
        in_specs=x_specs + [_const_spec((1, d)), _const_spec((d, n)), table_spec, table_spec] + c_in,
        out_specs=[pl.BlockSpec((None, N_RES, rows, n), lambda b, c: (b, 0, c, 0))] + c_out,
        out_shape=[jax.ShapeDtypeStruct((batch, N_RES, seq // N_RES, n), F32)] + c_shapes,
        compiler_params=_params(2),
        name="qkv_rope",
    )(*([x] * n_slabs), g, w, cos_t, sin_t, *c_ops)
    return outs[0], outs[1:]


def _attn_kernel(q_ref, k_ref, v_ref, qaug_ref, kaug_ref, o_ref,
                 s_buf, m_s, l_s, acc_s, *, seq):
    blk = ATTN_BLOCK
    grp = ATTN_GROUP
    n_groups = (seq // blk) // grp
    assert n_groups % 2 == 0 and seq // N_RES == blk
    last = len(DILATIONS) - 1
    lane = lax.broadcasted_iota(jnp.int32, (blk, LANES), 1)
    first_head = lane < ATTN_HEAD_DIM

    def pick(top, bot):
        return jnp.where(first_head, top, bot)

    def block_id(d, j, u):
        n_blocks = seq // (d * blk)
        if n_blocks == 1:
            return j * grp + u, 0, None, None
        if n_blocks >= grp:
            gpr = n_blocks // grp
            res = 0 if d == 1 else j // gpr
            n = (j % gpr) * grp + u if gpr > 1 else u
        else:
            res = j * (grp // n_blocks) + u // n_blocks
            n = u % n_blocks
        if isinstance(n, int):
            return res, n, max(n - 1, 0), min(n, 1)
        if u > 0:
            return res, n, n - 1, 1
        return res, n, jnp.maximum(n - 1, 0), jnp.minimum(n, 1)

    def seg_rows(d, n):
        _, rows = _segments(d)
        start = n * rows
        return pl.ds(start if isinstance(start, int) else pl.multiple_of(start, rows), rows)

    def load_block(ref, d, res, n):
        count, _ = _segments(d)
        return jnp.concatenate([ref[res + d * i, seg_rows(d, n), :] for i in range(count)], axis=0)

    def store_block(ref, branch, d, res, n, val):
        count, rows = _segments(d)
        for i in range(count):
            ref[branch, res + d * i, seg_rows(d, n), :] = val[i * rows:(i + 1) * rows]

    def scores(branch, d, j, u):
        res, n, n_prev, has_prev = block_id(d, j, u)
        qb = load_block(q_ref, d, res, n)
        q2 = jnp.concatenate([jnp.where(first_head, qb, 0.0), jnp.where(first_head, 0.0, qb)], axis=0).astype(BF16)
        q_ext = jnp.concatenate([q2, qaug_ref[...]], axis=1)
        kc = load_block(k_ref, d, res, n)
        if n_prev is None:
            kk = kc.astype(BF16)
            aug = kaug_ref[branch, 1, blk:2 * blk, :]
        else:
            kk = jnp.concatenate([load_block(k_ref, d, res, n_prev), kc], axis=0).astype(BF16)
            aug = kaug_ref[branch, has_prev]
        k_ext = jnp.concatenate([kk, aug], axis=1)
        s = lax.dot_general(q_ext, k_ext, (((1,), (1,)), ((), ())), preferred_element_type=F32)
        s_buf[j % 2, u, :, 0:s.shape[1]] = s

    def softmax_pv(branch, d, j, u):
        res, n, n_prev, _ = block_id(d, j, u)
        vc = load_block(v_ref, d, res, n)
        if n_prev is None:
            vv = vc.astype(BF16)
        else:
            vv = jnp.concatenate([load_block(v_ref, d, res, n_prev), vc], axis=0).astype(BF16)
        s = s_buf[j % 2, u, :, 0:vv.shape[0]]
        m = jnp.max(s, axis=-1, keepdims=True)
        p = jnp.exp2((s - m).astype(BF16))
        key_first_head = lax.broadcasted_iota(jnp.int32, vv.shape, 1) < ATTN_HEAD_DIM
        one = jnp.ones_like(vv)
        out0 = jnp.dot(p[:blk], jnp.where(key_first_head, vv, one), preferred_element_type=F32)
        out1 = jnp.dot(p[blk:], jnp.where(key_first_head, one, vv), preferred_element_type=F32)
        mb = jnp.broadcast_to(m, (2 * blk, LANES))
        m_b = pick(mb[:blk], mb[blk:])
        acc_b = pick(out0, out1)
        l_b = pltpu.roll(pick(out1, out0), ATTN_HEAD_DIM, 1)
        if branch < last:
            store_block(m_s, branch, d, res, n, m_b)
            store_block(l_s, branch, d, res, n, l_b)
            store_block(acc_s, branch, d, res, n, acc_b)
        else:
            ms = [m_s[i, res] for i in range(last)] + [m_b]
            ls = [l_s[i, res] for i in range(last)] + [l_b]
            accs = [acc_s[i, res] for i in range(last)] + [acc_b]
            m_all = functools.reduce(jnp.maximum, ms)
            ws = [jnp.exp2(mi - m_all) for mi in ms]
            num = sum(w * a for w, a in zip(ws, accs))
            den = sum(w * l for w, l in zip(ws, ls))
            o_ref[0, pl.ds(res, blk, stride=N_RES), :] = num / den

    for u in range(grp):
        scores(0, DILATIONS[0], 0, u)
    for branch, d in enumerate(DILATIONS):
        def group_body(j, carry, branch=branch, d=d):
            for u in range(grp):
                softmax_pv(branch, d, j, u)
            for u in range(grp):
                scores(branch, d, j + 1, u)
            return carry

        lax.fori_loop(0, n_groups - 1, group_body, 0)
        for u in range(grp):
            softmax_pv(branch, d, n_groups - 1, u)
        if branch + 1 < len(DILATIONS):
            for u in range(grp):
                scores(branch + 1, DILATIONS[branch + 1], 0, u)


def _attention(qkv, batch, seq):
    n_pairs = ATTN_HEADS * ATTN_HEAD_DIM // LANES
    res_major = (N_RES, seq // N_RES, LANES)
    blk_spec = lambda off: pl.BlockSpec((None,) + res_major, lambda b, h: (b, 0, 0, off + h))
    nb = len(DILATIONS) - 1
    qaug, kaug = _attn_consts()
    return pl.pallas_call(
        functools.partial(_attn_kernel, seq=seq),
        grid=(batch, n_pairs),
        in_specs=[blk_spec(0), blk_spec(n_pairs), blk_spec(2 * n_pairs),
                  _const_spec(qaug.shape), _const_spec(kaug.shape)],
        out_specs=pl.BlockSpec((1, seq, LANES), lambda b, h: (b, 0, h)),
        out_shape=jax.ShapeDtypeStruct((batch, seq, D_MODEL), F32),
        scratch_shapes=[pltpu.VMEM((2, ATTN_GROUP, 2 * ATTN_BLOCK, 2 * ATTN_BLOCK), F32),
                        pltpu.VMEM((nb,) + res_major, F32),
                        pltpu.VMEM((nb,) + res_major, F32),
                        pltpu.VMEM((nb,) + res_major, F32)],
        compiler_params=_params(2),
        name="dilated_attention",
    )(qkv, qkv, qkv, qaug, kaug)


def _oproj_ffn_kernel(*refs, final_norm, n_cast):
    ins, cast_in, (o_ref,), cast_out, _ = _split_refs(refs, 7, n_cast, 1)
    x_ref, a_ref, wo_ref, g_ref, win_ref, wout_ref, gfin_ref = ins
    _run_casts(cast_in, cast_out)
    x1 = x_ref[...] + jnp.dot(a_ref[...].astype(BF16), wo_ref[...], preferred_element_type=F32)
    h = _rmsnorm(x1, g_ref[...]).astype(BF16)
    acc = x1
    for lo, hi in FFN_CHUNKS:
        gate = jnp.dot(h, win_ref[:, lo:hi], preferred_element_type=F32)
        up = jnp.dot(h, win_ref[:, D_FF + lo:D_FF + hi], preferred_element_type=F32)
        act = (gate * (1.0 / (1.0 + jnp.exp(-gate))) * up).astype(BF16)
        acc = acc + jnp.dot(act, wout_ref[lo:hi, :], preferred_element_type=F32)
    if final_norm:
        acc = _rmsnorm(acc, gfin_ref[...])
    o_ref[...] = acc


def _oproj_ffn(x, a, wo, g, w_in, w_out, g_fin, final_norm, cast=()):
    m, d = x.shape
    steps = m // ROW_TILE
    row_spec = pl.BlockSpec((ROW_TILE, d), lambda i: (i, 0))
    c_in, c_out, c_shapes, c_ops = _cast_jobs(cast, steps, lambda i: i)
    outs = pl.pallas_call(
        functools.partial(_oproj_ffn_kernel, final_norm=final_norm, n_cast=len(cast)),
        grid=(steps,),
        in_specs=[row_spec, row_spec,
                  _const_spec(wo.shape), _const_spec((1, d)),
                  _const_spec(w_in.shape), _const_spec(w_out.shape), _const_spec((1, d))] + c_in,
        out_specs=[row_spec] + c_out,
        out_shape=[jax.ShapeDtypeStruct((m, d), F32)] + c_shapes,
        compiler_params=_params(1),
        name="oproj_ffn",
    )(x, a, wo, g, w_in, w_out, g_fin, *c_ops)
    return outs[0], outs[1:]


def _rope_tables(seq):
    half = ATTN_HEAD_DIM // 2
    freqs = ROPE_THETA ** (-np.arange(half, dtype=np.float64) / half)
    ang = np.arange(seq, dtype=np.float64)[:, None] * freqs[None, :]
    cos, sin = np.cos(ang), np.sin(ang)
    reps = LANES // ATTN_HEAD_DIM
    cos_t = np.tile(np.concatenate([cos, cos], axis=1), (1, reps))
    sin_t = np.tile(np.concatenate([-sin, sin], axis=1), (1, reps))
    res_major = lambda t: jnp.asarray(t.reshape(seq // N_RES, N_RES, LANES).transpose(1, 0, 2), F32)
    return res_major(cos_t), res_major(sin_t)


def kernel(x, mix_norm, ffn_norm, w_in_rec, conv_w, hgrn_lb, hgrn_norm, w_out_rec,
           w_qkv_attn, w_o_attn, w_ffn_in, w_ffn_out, final_norm):
    batch, seq, d = x.shape
    m = batch * seq
    xf = x.reshape(m, d)
    row = lambda v: v.reshape(1, -1)

    mix, (wo0, win0, wout0) = _proj_mixer(xf, row(mix_norm[0]), w_in_rec, 0, conv_w[0], hgrn_lb, row(hgrn_norm[0]),
                                          batch, seq, cast=[(w_out_rec, 0), (w_ffn_in, 0), (w_ffn_out, 0)])
    xf, (wqkv, wo1) = _oproj_ffn(xf, mix, wo0, row(ffn_norm[0]), win0, wout0, row(final_norm), False,
                                 cast=[(w_qkv_attn, 0), (w_o_attn, 0)])

    cos_t, sin_t = _rope_tables(seq)
    qkv, (win1, wout1) = _qkv_rope(xf, row(mix_norm[1]), wqkv, cos_t, sin_t, batch, seq,
                                   cast=[(w_ffn_in, 1), (w_ffn_out, 1)])
    att = _attention(qkv, batch, seq)
    xf, _ = _oproj_ffn(xf, att.reshape(m, d), wo1, row(ffn_norm[1]), win1, wout1, row(final_norm), True)
    return xf.reshape(batch, seq, d)
```

```python
import functools

import jax
import jax.numpy as jnp
import numpy as np
from jax import lax
from jax.experimental import pallas as pl
from jax.experimental.pallas import tpu as pltpu

D_MODEL = 1024
CONV_WIDTH = 3
CONV_CHANNELS = 512
HGRN_HEAD_DIM = 128
HGRN_HEADS = 4
HGRN_WIDTH = 512
ATTN_HEAD_DIM = 64
ATTN_HEADS = 16
ATTN_BLOCK = 128
DILATIONS = (1, 4, 16)
ROPE_THETA = 10000.0
D_FF = 2816
RMS_EPS = 1e-6
REC_IN_WIDTH = 3 * CONV_CHANNELS + 4 * HGRN_WIDTH

LANES = 128
VMEM_LIMIT_BYTES = 56 * 1024 * 1024
ROW_TILE = 512
QKV_SUBTILES = 2
MIX_TILE = 256
HGRN_CHUNK = 32
MXU_TILE = 256
FFN_CHUNKS = ((0, D_FF),)
PROJ_CHUNKS = tuple((lo, min(lo + 3 * MXU_TILE, REC_IN_WIDTH))
                    for lo in range(0, REC_IN_WIDTH, 3 * MXU_TILE))
MASK_VALUE = -1e30
ATTN_GROUP = 8

F32 = jnp.float32
BF16 = jnp.bfloat16


def _rmsnorm(x, g):
    return x * lax.rsqrt(jnp.mean(x * x, axis=-1, keepdims=True) + RMS_EPS) * g


def _const_spec(shape):
    zeros = (0,) * len(shape)
    return pl.BlockSpec(shape, lambda *_: zeros, pipeline_mode=pl.Buffered(1))


def _params(n_grid):
    return pltpu.CompilerParams(
        dimension_semantics=("arbitrary",) * n_grid, vmem_limit_bytes=VMEM_LIMIT_BYTES)


BF16_SUBLANES = 16


def _cast_jobs(jobs, n_steps, step_of):
    in_specs, out_specs, out_shapes, operands = [], [], [], []
    for w, layer in jobs:
        _, rows, cols = w.shape
        n = n_steps
        while rows % (n * BF16_SUBLANES):
            n //= 2
        per = n_steps // n
        in_specs.append(pl.BlockSpec((None, rows // n, cols),
                                     lambda *ids, layer=layer, per=per: (layer, step_of(*ids) // per, 0)))
        out_specs.append(pl.BlockSpec((rows // n, cols), lambda *ids, per=per: (step_of(*ids) // per, 0)))
        out_shapes.append(jax.ShapeDtypeStruct((rows, cols), BF16))
        operands.append(w)
    return in_specs, out_specs, out_shapes, operands


def _split_refs(refs, n_in, n_cast, n_out):
    bounds = np.cumsum([0, n_in, n_cast, n_out, n_cast])
    return [refs[a:b] for a, b in zip(bounds[:-1], bounds[1:])] + [refs[bounds[-1]:]]


def _run_casts(cast_in, cast_out):
    for src, dst in zip(cast_in, cast_out):
        dst[...] = src[...].astype(BF16)


def _mixer_consts():
    ts, c = MIX_TILE, HGRN_CHUNK
    t = np.arange(ts)
    same = (t[:, None] // c) == (t[None, :] // c)
    tril = same & (t[:, None] >= t[None, :])
    cum = np.concatenate([tril, same]).astype(np.float32)
    ind = (t[:, None] // c == np.arange(LANES)[None, :]).astype(np.float32)
    return jnp.asarray(cum, BF16), jnp.asarray(ind, BF16)


def _proj_mixer_kernel(*refs, n_cast, tiles_per_row):
    ins, cast_in, (out_ref,), cast_out, scratch = _split_refs(refs, 10, n_cast, 1)
    x_first, x_odd, x_even, g_ref, w_ref, convw_ref, lb_ref, hnorm_ref, cum_ref, ind_ref = ins
    w_bf, proj_a, proj_b, ubuf, state, vblk = scratch
    step = pl.program_id(0)
    ts = MIX_TILE

    def projection(x_ref, dst):
        h = _rmsnorm(x_ref[...], g_ref[...]).astype(BF16)

        def job(lo, hi):
            def run():
                dst[:, lo:hi] = jnp.dot(h, w_bf[:, lo:hi], preferred_element_type=F32)
            return run

        return [job(lo, hi) for lo, hi in PROJ_CHUNKS]

    def project(x_ref, dst):
        for run in projection(x_ref, dst):
            run()

    @pl.when(step == 0)
    def _():
        w_bf[...] = w_ref[...].astype(BF16)
        project(x_first, proj_a)

    @pl.when(step % (tiles_per_row // 2) == 0)
    def _():
        state[...] = jnp.zeros_like(state)
        vblk[...] = jnp.zeros_like(vblk)
        ubuf[0:8, :] = jnp.zeros((8, CONV_CHANNELS), F32)

    mixer_refs = (convw_ref, lb_ref, hnorm_ref, cum_ref, ind_ref, out_ref)
    _mixer_tile(proj_a, *mixer_refs, slice(0, ts), ubuf, state, vblk, projection(x_odd, proj_b))
    _mixer_tile(proj_b, *mixer_refs, slice(ts, 2 * ts), ubuf, state, vblk, projection(x_even, proj_a))
    _run_casts(cast_in, cast_out)


def _mixer_tile(proj_ref, convw_ref, lb_ref, hnorm_ref, cum_ref, ind_ref, out_ref, rows, ubuf, state, vblk, fillers):
    ts = MIX_TILE
    fillers = iter(fillers)

    def fill():
        job = next(fillers, None)
        if job is not None:
            job()

    cc = CONV_CHANNELS
    c = HGRN_CHUNK
    n_chunks = ts // c

    u = proj_ref[:, cc:2 * cc] * proj_ref[:, 2 * cc:3 * cc]
    ubuf[8:8 + ts, :] = u
    y = (convw_ref[0:1, :] * ubuf[6:6 + ts, :]
         + convw_ref[1:2, :] * ubuf[7:7 + ts, :]
         + convw_ref[2:3, :] * u)
    out_ref[rows, 0:cc] = (proj_ref[:, 0:cc] * y).astype(out_ref.dtype)
    ubuf[0:8, :] = u[ts - 8:ts, :]
    fill()

    lbp = lb_ref[...]
    e = jnp.exp(lbp - jnp.max(lbp, axis=0, keepdims=True))
    sm = e / jnp.sum(e, axis=0, keepdims=True)
    cum0 = sm[0:1, :]
    cum1 = cum0 + sm[1:2, :]
    lb = cum1 - cum0
    q0 = 3 * cc
    z = proj_ref[:, q0 + HGRN_WIDTH:q0 + 2 * HGRN_WIDTH]
    ez = jnp.exp(-jnp.abs(z))
    inv = 1.0 / (1.0 + ez)
    small = ez * inv
    pos = z >= 0.0
    lf = jnp.log(lb + (1.0 - lb) * jnp.where(pos, inv, small))
    kk = (1.0 - lb) * jnp.where(pos, small, inv)

    lf_hi = lf.astype(BF16)
    lf_lo = (lf - lf_hi.astype(F32)).astype(BF16)
    sums = (jnp.dot(cum_ref[...], lf_hi, preferred_element_type=F32)
            + jnp.dot(cum_ref[...], lf_lo, preferred_element_type=F32))
    g = sums[:ts]
    gl = sums[ts:]
    tn = (((0,), (0,)), ((), ()))
    decay_col = jnp.exp(lax.dot_general(lf_hi, ind_ref[...], tn, preferred_element_type=F32)
                        + lax.dot_general(lf_lo, ind_ref[...], tn, preferred_element_type=F32))

    half = 0.5 * gl
    q = proj_ref[:, q0:q0 + HGRN_WIDTH]
    qd = (q * jnp.exp(g - half)).astype(BF16)
    kd = (kk * jnp.exp(half - g)).astype(BF16)
    qg = (q * jnp.exp(g)).astype(BF16)
    kd2 = (kk * jnp.exp(gl - g)).astype(BF16)
    i_in = proj_ref[:, q0 + 2 * HGRN_WIDTH:q0 + 3 * HGRN_WIDTH]
    v = (i_in * (1.0 / (1.0 + jnp.exp(-i_in)))).astype(BF16)

    row = lax.broadcasted_iota(jnp.int32, (ts, ts), 0)
    col = lax.broadcasted_iota(jnp.int32, (ts, ts), 1)
    intra = (row // c == col // c) & (row >= col)
    hnorm = hnorm_ref[...]

    for hd in range(HGRN_HEADS):
        lanes = slice(hd * LANES, (hd + 1) * LANES)
        vh = v[:, lanes]
        a = lax.dot_general(qd[:, lanes], kd[:, lanes], (((1,), (1,)), ((), ())), preferred_element_type=F32)
        o = jnp.dot(jnp.where(intra, a, 0.0).astype(BF16), vh, preferred_element_type=F32)
        for ci in range(n_chunks):
            vblk[hd, ci * c:(ci + 1) * c, ci * LANES:(ci + 1) * LANES] = vh[ci * c:(ci + 1) * c]
        upd = lax.dot_general(kd2[:, lanes], vblk[hd], tn, preferred_element_type=F32)
        st = state[hd]
        inter = []
        for ci in range(n_chunks):
            inter.append(jnp.dot(qg[ci * c:(ci + 1) * c, lanes], st.astype(BF16), preferred_element_type=F32))
            st = st * decay_col[lanes, ci:ci + 1] + upd[:, ci * LANES:(ci + 1) * LANES]
        state[hd] = st
        o = o + jnp.concatenate(inter, axis=0)
        o = _rmsnorm(o, hnorm)
        gh = proj_ref[:, q0 + 3 * HGRN_WIDTH + hd * LANES:q0 + 3 * HGRN_WIDTH + (hd + 1) * LANES]
        o = o * (gh * (1.0 / (1.0 + jnp.exp(-gh))))
        out_ref[rows, cc + hd * LANES:cc + (hd + 1) * LANES] = o.astype(out_ref.dtype)
        fill()
    for job in fillers:
        job()


def _proj_mixer(x, g, w, layer, conv_w, hgrn_lb, hgrn_norm, batch, seq, cast):
    ts = MIX_TILE
    d = x.shape[1]
    n = w.shape[2]
    tiles_per_row = seq // ts
    n_tiles = batch * tiles_per_row
    steps = n_tiles // 2
    assert tiles_per_row % 2 == 0
    cum, ind = _mixer_consts()
    c_in, c_out, c_shapes, c_ops = _cast_jobs(cast, steps, lambda i: i)
    tile_spec = lambda imap, **kw: pl.BlockSpec((ts, d), imap, **kw)
    outs = pl.pallas_call(
        functools.partial(_proj_mixer_kernel, n_cast=len(cast), tiles_per_row=tiles_per_row),
        grid=(steps,),
        in_specs=[tile_spec(lambda i: (0, 0), pipeline_mode=pl.Buffered(1)),
                  tile_spec(lambda i: (2 * i + 1, 0)),
                  tile_spec(lambda i: (jnp.minimum(2 * i + 2, n_tiles - 1), 0)),
                  _const_spec((1, d)),
                  pl.BlockSpec((None, d, n), lambda i: (layer, 0, 0), pipeline_mode=pl.Buffered(1)),
                  _const_spec((CONV_WIDTH, CONV_CHANNELS)),
                  _const_spec((3, HGRN_WIDTH)),
                  _const_spec((1, HGRN_HEAD_DIM)),
                  _const_spec(cum.shape), _const_spec(ind.shape)] + c_in,
        out_specs=[pl.BlockSpec((2 * ts, D_MODEL), lambda i: (i, 0))] + c_out,
        out_shape=[jax.ShapeDtypeStruct((batch * seq, D_MODEL), BF16)] + c_shapes,
        scratch_shapes=[pltpu.VMEM((d, n), BF16),
                        pltpu.VMEM((ts, n), F32),
                        pltpu.VMEM((ts, n), F32),
                        pltpu.VMEM((ts + 8, CONV_CHANNELS), F32),
                        pltpu.VMEM((HGRN_HEADS, HGRN_HEAD_DIM, HGRN_HEAD_DIM), F32),
                        pltpu.VMEM((HGRN_HEADS, ts, (ts // HGRN_CHUNK) * LANES), BF16)],
        compiler_params=_params(1),
        name="proj_conv_hgrn_mixer",
    )(x, x, x, g, w, conv_w, hgrn_lb, hgrn_norm, cum, ind, *c_ops)
    return outs[0], outs[1:]


N_RES = DILATIONS[-1]
Q_SCALE = ATTN_HEAD_DIM ** -0.5 * float(np.log2(np.e))


def _segments(d):
    count = N_RES // d
    return count, ATTN_BLOCK // count


def _attn_consts():
    blk = ATTN_BLOCK
    e = np.arange(blk)
    none = np.full((blk, blk), MASK_VALUE)
    kaug = []
    for d in DILATIONS:
        count, rows = _segments(d)
        pos = (e % rows) * count + e // rows
        kpos, qpos = pos[:, None], pos[None, :]
        prev_ok = np.where(kpos >= qpos, 0.0, MASK_VALUE)
        cur_ok = np.where(kpos <= qpos, 0.0, MASK_VALUE)
        kaug.append(np.stack([np.concatenate([none, cur_ok]), np.concatenate([prev_ok, cur_ok])]))
    qaug = np.concatenate([np.eye(blk), np.eye(blk)])
    return jnp.asarray(qaug, BF16), jnp.asarray(np.stack(kaug), BF16)


def _qkv_rope_kernel(*refs, n_cast):
    n_slabs = D_MODEL // LANES
    ins, cast_in, (o_ref,), cast_out, _ = _split_refs(refs, n_slabs + 4, n_cast, 1)
    x_refs = ins[:n_slabs]
    g_ref, w_ref, cos_ref, sin_ref = ins[n_slabs:]
    _run_casts(cast_in, cast_out)
    rows = ROW_TILE // N_RES
    lane = lax.broadcasted_iota(jnp.int32, (ROW_TILE, LANES), 1)
    first_half = (lane % ATTN_HEAD_DIM) < (ATTN_HEAD_DIM // 2)
    width = ATTN_HEADS * ATTN_HEAD_DIM
    for sub in range(QKV_SUBTILES):
        x = jnp.concatenate(
            [jnp.concatenate([ref[pl.ds(sub * ROW_TILE + r, rows, stride=N_RES), :] for r in range(N_RES)], axis=0)
             for ref in x_refs], axis=1)
        h = _rmsnorm(x, g_ref[...]).astype(BF16)
        out_rows = slice(sub * rows, (sub + 1) * rows)
        cos = cos_ref[:, out_rows, :].reshape(ROW_TILE, LANES)
        sin = sin_ref[:, out_rows, :].reshape(ROW_TILE, LANES)
        for part in range(3):
            y = jnp.dot(h, w_ref[:, part * width:(part + 1) * width], preferred_element_type=F32)
            if part < 2:
                cols = []
                for cb in range(width // LANES):
                    t = y[:, cb * LANES:(cb + 1) * LANES]
                    partner = jnp.where(first_half,
                                        pltpu.roll(t, LANES - ATTN_HEAD_DIM // 2, 1),
                                        pltpu.roll(t, ATTN_HEAD_DIM // 2, 1))
                    t = t * cos + partner * sin
                    cols.append(t * Q_SCALE if part == 0 else t)
                y = jnp.concatenate(cols, axis=1)
            o_ref[:, out_rows, part * width:(part + 1) * width] = y.reshape(N_RES, rows, width)


def _qkv_rope(x, g, w, cos_t, sin_t, batch, seq, cast):
    d = x.shape[-1]
    n = w.shape[1]
    step_rows = QKV_SUBTILES * ROW_TILE
    rows = step_rows // N_RES
    tiles = seq // step_rows
    n_slabs = d // LANES
    x_specs = [pl.BlockSpec((step_rows, LANES), functools.partial(lambda b, c, s: (b * tiles + c, s), s=s))
               for s in range(n_slabs)]
    table_spec = pl.BlockSpec((N_RES, rows, LANES), lambda b, c: (0, c, 0))
    c_in, c_out, c_shapes, c_ops = _cast_jobs(cast, batch * tiles, lambda b, c: b * tiles + c)
    outs = pl.pallas_call(
        functools.partial(_qkv_rope_kernel, n_cast=len(cast)),
        grid=(batch, tiles),
        in_specs=x_specs + [_const_spec((1, d)), _const_spec((d, n)), table_spec, table_spec] + c_in,
        out_specs=[pl.BlockSpec((None, N_RES, rows, n), lambda b, c: (b, 0, c, 0))] + c_out,
        out_shape=[jax.ShapeDtypeStruct((batch, N_RES, seq // N_RES, n), F32)] + c_shapes,
        compiler_params=_params(2),
        name="qkv_rope",
    )(*([x] * n_slabs), g, w, cos_t, sin_t, *c_ops)
    return outs[0], outs[1:]


def _attn_kernel(q_ref, k_ref, v_ref, qaug_ref, kaug_ref, o_ref,
                 s_buf, m_s, l_s, acc_s, *, seq):
    blk = ATTN_BLOCK
    grp = ATTN_GROUP
    n_groups = (seq // blk) // grp
    assert seq // N_RES == blk
    last = len(DILATIONS) - 1
    lane = lax.broadcasted_iota(jnp.int32, (blk, LANES), 1)
    first_head = lane < ATTN_HEAD_DIM

    def pick(top, bot):
        return jnp.where(first_head, top, bot)

    def block_id(d, j, u):
        n_blocks = seq // (d * blk)
        if n_blocks == 1:
            return j * grp + u, 0, None, None
        if n_blocks >= grp:
            gpr = n_blocks // grp
            res = 0 if d == 1 else j // gpr
            n = (j % gpr) * grp + u if gpr > 1 else u
        else:
            res = j * (grp // n_blocks) + u // n_blocks
            n = u % n_blocks
        if isinstance(n, int):
            return res, n, max(n - 1, 0), min(n, 1)
        if u > 0:
            return res, n, n - 1, 1
        return res, n, jnp.maximum(n - 1, 0), jnp.minimum(n, 1)

    def seg_rows(d, n):
        _, rows = _segments(d)
        start = n * rows
        return pl.ds(start if isinstance(start, int) else pl.multiple_of(start, rows), rows)

    def load_block(ref, d, res, n):
        count, _ = _segments(d)
        return jnp.concatenate([ref[res + d * i, seg_rows(d, n), :] for i in range(count)], axis=0)

    def store_block(ref, branch, d, res, n, val):
        count, rows = _segments(d)
        for i in range(count):
            ref[branch, res + d * i, seg_rows(d, n), :] = val[i * rows:(i + 1) * rows]

    def scores(branch, d, j, u):
        res, n, n_prev, has_prev = block_id(d, j, u)
        qb = load_block(q_ref, d, res, n)
        q2 = jnp.concatenate([jnp.where(first_head, qb, 0.0), jnp.where(first_head, 0.0, qb)], axis=0).astype(BF16)
        q_ext = jnp.concatenate([q2, qaug_ref[...]], axis=1)
        kc = load_block(k_ref, d, res, n)
        if n_prev is None:
            kk = kc.astype(BF16)
            aug = kaug_ref[branch, 1, blk:2 * blk, :]
        else:
            kk = jnp.concatenate([load_block(k_ref, d, res, n_prev), kc], axis=0).astype(BF16)
            aug = kaug_ref[branch, has_prev]
        k_ext = jnp.concatenate([kk, aug], axis=1)
        s = lax.dot_general(q_ext, k_ext, (((1,), (1,)), ((), ())), preferred_element_type=F32)
        s_buf[(branch * n_groups + j) % 2, u, :, 0:s.shape[1]] = s

    def softmax(branch, d, j, u):
        _, _, n_prev, _ = block_id(d, j, u)
        keys = blk if n_prev is None else 2 * blk
        s = s_buf[(branch * n_groups + j) % 2, u, :, 0:keys]
        m = jnp.max(s, axis=-1, keepdims=True)
        return jnp.exp2((s - m).astype(BF16)), m

    def pv(branch, d, j, u, p, m):
        res, n, n_prev, _ = block_id(d, j, u)
        vc = load_block(v_ref, d, res, n)
        if n_prev is None:
            vv = vc.astype(BF16)
        else:
            vv = jnp.concatenate([load_block(v_ref, d, res, n_prev), vc], axis=0).astype(BF16)
        key_first_head = lax.broadcasted_iota(jnp.int32, vv.shape, 1) < ATTN_HEAD_DIM
        one = jnp.ones_like(vv)
        out0 = jnp.dot(p[:blk], jnp.where(key_first_head, vv, one), preferred_element_type=F32)
        out1 = jnp.dot(p[blk:], jnp.where(key_first_head, one, vv), preferred_element_type=F32)
        mb = jnp.broadcast_to(m, (2 * blk, LANES))
        m_b = pick(mb[:blk], mb[blk:])
        acc_b = pick(out0, out1)
        l_b = pltpu.roll(pick(out1, out0), ATTN_HEAD_DIM, 1)
        if branch < last:
            store_block(m_s, branch, d, res, n, m_b)
            store_block(l_s, branch, d, res, n, l_b)
            store_block(acc_s, branch, d, res, n, acc_b)
        else:
            ms = [m_s[i, res] for i in range(last)] + [m_b]
            ls = [l_s[i, res] for i in range(last)] + [l_b]
            accs = [acc_s[i, res] for i in range(last)] + [acc_b]
            m_all = functools.reduce(jnp.maximum, ms)
            ws = [jnp.exp2(mi - m_all) for mi in ms]
            num = sum(w * a for w, a in zip(ws, accs))
            den = sum(w * l for w, l in zip(ws, ls))
            o_ref[0, pl.ds(res, blk, stride=N_RES), :] = num / den

    def stage(branch, d, j, nxt):
        for u in range(grp):
            pv(branch, d, j, u, *softmax(branch, d, j, u))
        if nxt is not None:
            for u in range(grp):
                scores(*nxt, u)

    for u in range(grp):
        scores(0, DILATIONS[0], 0, u)
    for branch, d in enumerate(DILATIONS):
        def group_body(j, carry, branch=branch, d=d):
            stage(branch, d, j, (branch, d, j + 1))
            return carry

        lax.fori_loop(0, n_groups - 1, group_body, 0)
        following = (branch + 1, DILATIONS[branch + 1], 0) if branch + 1 < len(DILATIONS) else None
        stage(branch, d, n_groups - 1, following)


def _attention(qkv, batch, seq):
    n_pairs = ATTN_HEADS * ATTN_HEAD_DIM // LANES
    res_major = (N_RES, seq // N_RES, LANES)
    blk_spec = lambda off: pl.BlockSpec((None,) + res_major, lambda b, h: (b, 0, 0, off + h))
    nb = len(DILATIONS) - 1
    qaug, kaug = _attn_consts()
    return pl.pallas_call(
        functools.partial(_attn_kernel, seq=seq),
        grid=(batch, n_pairs),
        in_specs=[blk_spec(0), blk_spec(n_pairs), blk_spec(2 * n_pairs),
                  _const_spec(qaug.shape), _const_spec(kaug.shape)],
        out_specs=pl.BlockSpec((1, seq, LANES), lambda b, h: (b, 0, h)),
        out_shape=jax.ShapeDtypeStruct((batch, seq, D_MODEL), F32),
        scratch_shapes=[pltpu.VMEM((2, ATTN_GROUP, 2 * ATTN_BLOCK, 2 * ATTN_BLOCK), F32),
                        pltpu.VMEM((nb,) + res_major, F32),
                        pltpu.VMEM((nb,) + res_major, F32),
                        pltpu.VMEM((nb,) + res_major, F32)],
        compiler_params=_params(2),
        name="dilated_attention",
    )(qkv, qkv, qkv, qaug, kaug)


def _oproj_ffn_kernel(*refs, final_norm, n_cast):
    ins, cast_in, (o_ref,), cast_out, _ = _split_refs(refs, 7, n_cast, 1)
    x_ref, a_ref, wo_ref, g_ref, win_ref, wout_ref, gfin_ref = ins
    _run_casts(cast_in, cast_out)
    x1 = x_ref[...] + jnp.dot(a_ref[...].astype(BF16), wo_ref[...], preferred_element_type=F32)
    h = _rmsnorm(x1, g_ref[...]).astype(BF16)
    acc = x1
    for lo, hi in FFN_CHUNKS:
        gate = jnp.dot(h, win_ref[:, lo:hi], preferred_element_type=F32)
        up = jnp.dot(h, win_ref[:, D_FF + lo:D_FF + hi], preferred_element_type=F32)
        act = (gate * (1.0 / (1.0 + jnp.exp(-gate))) * up).astype(BF16)
        acc = acc + jnp.dot(act, wout_ref[lo:hi, :], preferred_element_type=F32)
    if final_norm:
        acc = _rmsnorm(acc, gfin_ref[...])
    o_ref[...] = acc


def _oproj_ffn(x, a, wo, g, w_in, w_out, g_fin, final_norm, cast=()):
    m, d = x.shape
    steps = m // ROW_TILE
    row_spec = pl.BlockSpec((ROW_TILE, d), lambda i: (i, 0))
    c_in, c_out, c_shapes, c_ops = _cast_jobs(cast, steps, lambda i: i)
    outs = pl.pallas_call(
        functools.partial(_oproj_ffn_kernel, final_norm=final_norm, n_cast=len(cast)),
        grid=(steps,),
        in_specs=[row_spec, row_spec,
                  _const_spec(wo.shape), _const_spec((1, d)),
                  _const_spec(w_in.shape), _const_spec(w_out.shape), _const_spec((1, d))] + c_in,
        out_specs=[row_spec] + c_out,
        out_shape=[jax.ShapeDtypeStruct((m, d), F32)] + c_shapes,
        compiler_params=_params(1),
        name="oproj_ffn",
    )(x, a, wo, g, w_in, w_out, g_fin, *c_ops)
    return outs[0], outs[1:]


def _rope_tables(seq):
    half = ATTN_HEAD_DIM // 2
    freqs = ROPE_THETA ** (-np.arange(half, dtype=np.float64) / half)
    ang = np.arange(seq, dtype=np.float64)[:, None] * freqs[None, :]
    cos, sin = np.cos(ang), np.sin(ang)
    reps = LANES // ATTN_HEAD_DIM
    cos_t = np.tile(np.concatenate([cos, cos], axis=1), (1, reps))
    sin_t = np.tile(np.concatenate([-sin, sin], axis=1), (1, reps))
    res_major = lambda t: jnp.asarray(t.reshape(seq // N_RES, N_RES, LANES).transpose(1, 0, 2), F32)
    return res_major(cos_t), res_major(sin_t)


def kernel(x, mix_norm, ffn_norm, w_in_rec, conv_w, hgrn_lb, hgrn_norm, w_out_rec,
           w_qkv_attn, w_o_attn, w_ffn_in, w_ffn_out, final_norm):
    batch, seq, d = x.shape
    m = batch * seq
    xf = x.reshape(m, d)
    row = lambda v: v.reshape(1, -1)

    mix, (wo0, win0, wout0) = _proj_mixer(xf, row(mix_norm[0]), w_in_rec, 0, conv_w[0], hgrn_lb, row(hgrn_norm[0]),
                                          batch, seq, cast=[(w_out_rec, 0), (w_ffn_in, 0), (w_ffn_out, 0)])
    xf, (wqkv, wo1) = _oproj_ffn(xf, mix, wo0, row(ffn_norm[0]), win0, wout0, row(final_norm), False,
                                 cast=[(w_qkv_attn, 0), (w_o_attn, 0)])

    cos_t, sin_t = _rope_tables(seq)
    qkv, (win1, wout1) = _qkv_rope(xf, row(mix_norm[1]), wqkv, cos_t, sin_t, batch, seq,
                                   cast=[(w_ffn_in, 1), (w_ffn_out, 1)])
    att = _attention(qkv, batch, seq)
    xf, _ = _oproj_ffn(xf, att.reshape(m, d), wo1, row(ffn_norm[1]), win1, wout1, row(final_norm), True)
    return xf.reshape(batch, seq, d)
```

```python
import functools

import jax
import jax.numpy as jnp
import numpy as np
from jax import lax
from jax.experimental import pallas as pl
from jax.experimental.pallas import tpu as pltpu

D_MODEL = 1024
CONV_WIDTH = 3
CONV_CHANNELS = 512
HGRN_HEAD_DIM = 128
HGRN_HEADS = 4
HGRN_WIDTH = 512
ATTN_HEAD_DIM = 64
ATTN_HEADS = 16
ATTN_BLOCK = 128
DILATIONS = (1, 4, 16)
ROPE_THETA = 10000.0
D_FF = 2816
RMS_EPS = 1e-6
REC_IN_WIDTH = 3 * CONV_CHANNELS + 4 * HGRN_WIDTH

LANES = 128
VMEM_LIMIT_BYTES = 56 * 1024 * 1024
ROW_TILE = 512
QKV_SUBTILES = 2
MIX_TILE = 256
HGRN_CHUNK = 32
MXU_TILE = 256
FFN_CHUNKS = ((0, D_FF),)
PROJ_CHUNKS = tuple((lo, min(lo + 3 * MXU_TILE, REC_IN_WIDTH))
                    for lo in range(0, REC_IN_WIDTH, 3 * MXU_TILE))
MASK_VALUE = -1e30
ATTN_GROUP = 8

F32 = jnp.float32
BF16 = jnp.bfloat16


def _rmsnorm(x, g):
    return x * lax.rsqrt(jnp.mean(x * x, axis=-1, keepdims=True) + RMS_EPS) * g


def _const_spec(shape):
    zeros = (0,) * len(shape)
    return pl.BlockSpec(shape, lambda *_: zeros, pipeline_mode=pl.Buffered(1))


def _params(n_grid):
    return pltpu.CompilerParams(
        dimension_semantics=("arbitrary",) * n_grid, vmem_limit_bytes=VMEM_LIMIT_BYTES)


BF16_SUBLANES = 16


def _cast_jobs(jobs, n_steps, step_of):
    in_specs, out_specs, out_shapes, operands = [], [], [], []
    for w, layer in jobs:
        _, rows, cols = w.shape
        n = n_steps
        while rows % (n * BF16_SUBLANES):
            n //= 2
        per = n_steps // n
        in_specs.append(pl.BlockSpec((None, rows // n, cols),
                                     lambda *ids, layer=layer, per=per: (layer, step_of(*ids) // per, 0)))
        out_specs.append(pl.BlockSpec((rows // n, cols), lambda *ids, per=per: (step_of(*ids) // per, 0)))
        out_shapes.append(jax.ShapeDtypeStruct((rows, cols), BF16))
        operands.append(w)
    return in_specs, out_specs, out_shapes, operands


def _split_refs(refs, n_in, n_cast, n_out):
    bounds = np.cumsum([0, n_in, n_cast, n_out, n_cast])
    return [refs[a:b] for a, b in zip(bounds[:-1], bounds[1:])] + [refs[bounds[-1]:]]


def _run_casts(cast_in, cast_out):
    for src, dst in zip(cast_in, cast_out):
        dst[...] = src[...].astype(BF16)


def _mixer_consts():
    ts, c = MIX_TILE, HGRN_CHUNK
    t = np.arange(ts)
    same = (t[:, None] // c) == (t[None, :] // c)
    tril = same & (t[:, None] >= t[None, :])
    cum = np.concatenate([tril, same]).astype(np.float32)
    ind = (t[:, None] // c == np.arange(LANES)[None, :]).astype(np.float32)
    return jnp.asarray(cum, BF16), jnp.asarray(ind, BF16)


def _proj_mixer_kernel(*refs, n_cast, tiles_per_row):
    ins, cast_in, (out_ref,), cast_out, scratch = _split_refs(refs, 10, n_cast, 1)
    x_first, x_odd, x_even, g_ref, w_ref, convw_ref, lb_ref, hnorm_ref, cum_ref, ind_ref = ins
    w_bf, proj_a, proj_b, ubuf, state, vblk = scratch
    step = pl.program_id(0)
    ts = MIX_TILE

    def projection(x_ref, dst):
        h = _rmsnorm(x_ref[...], g_ref[...]).astype(BF16)

        def job(lo, hi):
            def run():
                dst[:, lo:hi] = jnp.dot(h, w_bf[:, lo:hi], preferred_element_type=F32)
            return run

        return [job(lo, hi) for lo, hi in PROJ_CHUNKS]

    def project(x_ref, dst):
        for run in projection(x_ref, dst):
            run()

    @pl.when(step == 0)
    def _():
        w_bf[...] = w_ref[...].astype(BF16)
        project(x_first, proj_a)

    @pl.when(step % (tiles_per_row // 2) == 0)
    def _():
        state[...] = jnp.zeros_like(state)
        vblk[...] = jnp.zeros_like(vblk)
        ubuf[0:8, :] = jnp.zeros((8, CONV_CHANNELS), F32)

    mixer_refs = (convw_ref, lb_ref, hnorm_ref, cum_ref, ind_ref, out_ref)
    _mixer_tile(proj_a, *mixer_refs, slice(0, ts), ubuf, state, vblk, projection(x_odd, proj_b))
    _mixer_tile(proj_b, *mixer_refs, slice(ts, 2 * ts), ubuf, state, vblk, projection(x_even, proj_a))
    _run_casts(cast_in, cast_out)


def _mixer_tile(proj_ref, convw_ref, lb_ref, hnorm_ref, cum_ref, ind_ref, out_ref, rows, ubuf, state, vblk, fillers):
    ts = MIX_TILE
    fillers = iter(fillers)

    def fill():
        job = next(fillers, None)
        if job is not None:
            job()

    cc = CONV_CHANNELS
    c = HGRN_CHUNK
    n_chunks = ts // c

    u = proj_ref[:, cc:2 * cc] * proj_ref[:, 2 * cc:3 * cc]
    ubuf[8:8 + ts, :] = u
    y = (convw_ref[0:1, :] * ubuf[6:6 + ts, :]
         + convw_ref[1:2, :] * ubuf[7:7 + ts, :]
         + convw_ref[2:3, :] * u)
    out_ref[rows, 0:cc] = (proj_ref[:, 0:cc] * y).astype(out_ref.dtype)
    ubuf[0:8, :] = u[ts - 8:ts, :]
    fill()

    lbp = lb_ref[...]
    e = jnp.exp(lbp - jnp.max(lbp, axis=0, keepdims=True))
    sm = e / jnp.sum(e, axis=0, keepdims=True)
    cum0 = sm[0:1, :]
    cum1 = cum0 + sm[1:2, :]
    lb = cum1 - cum0
    q0 = 3 * cc
    z = proj_ref[:, q0 + HGRN_WIDTH:q0 + 2 * HGRN_WIDTH]
    ez = jnp.exp(-jnp.abs(z))
    inv = 1.0 / (1.0 + ez)
    small = ez * inv
    pos = z >= 0.0
    lf = jnp.log(lb + (1.0 - lb) * jnp.where(pos, inv, small))
    kk = (1.0 - lb) * jnp.where(pos, small, inv)

    lf_hi = lf.astype(BF16)
    lf_lo = (lf - lf_hi.astype(F32)).astype(BF16)
    sums = (jnp.dot(cum_ref[...], lf_hi, preferred_element_type=F32)
            + jnp.dot(cum_ref[...], lf_lo, preferred_element_type=F32))
    g = sums[:ts]
    gl = sums[ts:]
    tn = (((0,), (0,)), ((), ()))
    decay_col = jnp.exp(lax.dot_general(lf_hi, ind_ref[...], tn, preferred_element_type=F32)
                        + lax.dot_general(lf_lo, ind_ref[...], tn, preferred_element_type=F32))

    half = 0.5 * gl
    q = proj_ref[:, q0:q0 + HGRN_WIDTH]
    qd = (q * jnp.exp(g - half)).astype(BF16)
    kd = (kk * jnp.exp(half - g)).astype(BF16)
    qg = (q * jnp.exp(g)).astype(BF16)
    kd2 = (kk * jnp.exp(gl - g)).astype(BF16)
    i_in = proj_ref[:, q0 + 2 * HGRN_WIDTH:q0 + 3 * HGRN_WIDTH]
    v = (i_in * (1.0 / (1.0 + jnp.exp(-i_in)))).astype(BF16)

    row = lax.broadcasted_iota(jnp.int32, (ts, ts), 0)
    col = lax.broadcasted_iota(jnp.int32, (ts, ts), 1)
    intra = (row // c == col // c) & (row >= col)
    hnorm = hnorm_ref[...]

    for hd in range(HGRN_HEADS):
        lanes = slice(hd * LANES, (hd + 1) * LANES)
        vh = v[:, lanes]
        a = lax.dot_general(qd[:, lanes], kd[:, lanes], (((1,), (1,)), ((), ())), preferred_element_type=F32)
        o = jnp.dot(jnp.where(intra, a, 0.0).astype(BF16), vh, preferred_element_type=F32)
        for ci in range(n_chunks):
            vblk[hd, ci * c:(ci + 1) * c, ci * LANES:(ci + 1) * LANES] = vh[ci * c:(ci + 1) * c]
        upd = lax.dot_general(kd2[:, lanes], vblk[hd], tn, preferred_element_type=F32)
        st = state[hd]
        inter = []
        for ci in range(n_chunks):
            inter.append(jnp.dot(qg[ci * c:(ci + 1) * c, lanes], st.astype(BF16), preferred_element_type=F32))
            st = st * decay_col[lanes, ci:ci + 1] + upd[:, ci * LANES:(ci + 1) * LANES]
        state[hd] = st
        o = o + jnp.concatenate(inter, axis=0)
        o = _rmsnorm(o, hnorm)
        gh = proj_ref[:, q0 + 3 * HGRN_WIDTH + hd * LANES:q0 + 3 * HGRN_WIDTH + (hd + 1) * LANES]
        o = o * (gh * (1.0 / (1.0 + jnp.exp(-gh))))
        out_ref[rows, cc + hd * LANES:cc + (hd + 1) * LANES] = o.astype(out_ref.dtype)
        fill()
    for job in fillers:
        job()


def _proj_mixer(x, g, w, layer, conv_w, hgrn_lb, hgrn_norm, batch, seq, cast):
    ts = MIX_TILE
    d = x.shape[1]
    n = w.shape[2]
    tiles_per_row = seq // ts
    n_tiles = batch * tiles_per_row
    steps = n_tiles // 2
    assert tiles_per_row % 2 == 0
    cum, ind = _mixer_consts()
    c_in, c_out, c_shapes, c_ops = _cast_jobs(cast, steps, lambda i: i)
    tile_spec = lambda imap, **kw: pl.BlockSpec((ts, d), imap, **kw)
    outs = pl.pallas_call(
        functools.partial(_proj_mixer_kernel, n_cast=len(cast), tiles_per_row=tiles_per_row),
        grid=(steps,),
        in_specs=[tile_spec(lambda i: (0, 0), pipeline_mode=pl.Buffered(1)),
                  tile_spec(lambda i: (2 * i + 1, 0)),
                  tile_spec(lambda i: (jnp.minimum(2 * i + 2, n_tiles - 1), 0)),
                  _const_spec((1, d)),
                  pl.BlockSpec((None, d, n), lambda i: (layer, 0, 0), pipeline_mode=pl.Buffered(1)),
                  _const_spec((CONV_WIDTH, CONV_CHANNELS)),
                  _const_spec((3, HGRN_WIDTH)),
                  _const_spec((1, HGRN_HEAD_DIM)),
                  _const_spec(cum.shape), _const_spec(ind.shape)] + c_in,
        out_specs=[pl.BlockSpec((2 * ts, D_MODEL), lambda i: (i, 0))] + c_out,
        out_shape=[jax.ShapeDtypeStruct((batch * seq, D_MODEL), BF16)] + c_shapes,
        scratch_shapes=[pltpu.VMEM((d, n), BF16),
                        pltpu.VMEM((ts, n), F32),
                        pltpu.VMEM((ts, n), F32),
                        pltpu.VMEM((ts + 8, CONV_CHANNELS), F32),
                        pltpu.VMEM((HGRN_HEADS, HGRN_HEAD_DIM, HGRN_HEAD_DIM), F32),
                        pltpu.VMEM((HGRN_HEADS, ts, (ts // HGRN_CHUNK) * LANES), BF16)],
        compiler_params=_params(1),
        name="proj_conv_hgrn_mixer",
    )(x, x, x, g, w, conv_w, hgrn_lb, hgrn_norm, cum, ind, *c_ops)
    return outs[0], outs[1:]


N_RES = DILATIONS[-1]
Q_SCALE = ATTN_HEAD_DIM ** -0.5 * float(np.log2(np.e))


def _segments(d):
    count = N_RES // d
    return count, ATTN_BLOCK // count


def _attn_consts():
    blk = ATTN_BLOCK
    e = np.arange(blk)
    none = np.full((blk, blk), MASK_VALUE)
    kaug = []
    for d in DILATIONS:
        count, rows = _segments(d)
        pos = (e % rows) * count + e // rows
        kpos, qpos = pos[:, None], pos[None, :]
        prev_ok = np.where(kpos >= qpos, 0.0, MASK_VALUE)
        cur_ok = np.where(kpos <= qpos, 0.0, MASK_VALUE)
        kaug.append(np.stack([np.concatenate([none, cur_ok]), np.concatenate([prev_ok, cur_ok])]))
    qaug = np.concatenate([np.eye(blk), np.eye(blk)])
    return jnp.asarray(qaug, BF16), jnp.asarray(np.stack(kaug), BF16)


def _qkv_rope_kernel(*refs, n_cast):
    n_slabs = D_MODEL // LANES
    ins, cast_in, (o_ref,), cast_out, _ = _split_refs(refs, n_slabs + 4, n_cast, 1)
    x_refs = ins[:n_slabs]
    g_ref, w_ref, cos_ref, sin_ref = ins[n_slabs:]
    _run_casts(cast_in, cast_out)
    rows = ROW_TILE // N_RES
    lane = lax.broadcasted_iota(jnp.int32, (ROW_TILE, LANES), 1)
    first_half = (lane % ATTN_HEAD_DIM) < (ATTN_HEAD_DIM // 2)
    width = ATTN_HEADS * ATTN_HEAD_DIM
    for sub in range(QKV_SUBTILES):
        x = jnp.concatenate(
            [jnp.concatenate([ref[pl.ds(sub * ROW_TILE + r, rows, stride=N_RES), :] for r in range(N_RES)], axis=0)
             for ref in x_refs], axis=1)
        h = _rmsnorm(x, g_ref[...]).astype(BF16)
        out_rows = slice(sub * rows, (sub + 1) * rows)
        cos = cos_ref[:, out_rows, :].reshape(ROW_TILE, LANES)
        sin = sin_ref[:, out_rows, :].reshape(ROW_TILE, LANES)
        for part in range(3):
            y = jnp.dot(h, w_ref[:, part * width:(part + 1) * width], preferred_element_type=F32)
            if part < 2:
                cols = []
                for cb in range(width // LANES):
                    t = y[:, cb * LANES:(cb + 1) * LANES]
                    partner = jnp.where(first_half,
                                        pltpu.roll(t, LANES - ATTN_HEAD_DIM // 2, 1),
                                        pltpu.roll(t, ATTN_HEAD_DIM // 2, 1))
                    t = t * cos + partner * sin
                    cols.append(t * Q_SCALE if part == 0 else t)
                y = jnp.concatenate(cols, axis=1)
            o_ref[:, out_rows, part * width:(part + 1) * width] = y.reshape(N_RES, rows, width)


def _qkv_rope(x, g, w, cos_t, sin_t, batch, seq, cast):
    d = x.shape[-1]
    n = w.shape[1]
    step_rows = QKV_SUBTILES * ROW_TILE
    rows = step_rows // N_RES
    tiles = seq // step_rows
    n_slabs = d // LANES
    x_specs = [pl.BlockSpec((step_rows, LANES), functools.partial(lambda b, c, s: (b * tiles + c, s), s=s))
               for s in range(n_slabs)]
    table_spec = pl.BlockSpec((N_RES, rows, LANES), lambda b, c: (0, c, 0))
    c_in, c_out, c_shapes, c_ops = _cast_jobs(cast, batch * tiles, lambda b, c: b * tiles + c)
    outs = pl.pallas_call(
        functools.partial(_qkv_rope_kernel, n_cast=len(cast)),
        grid=(batch, tiles),
        in_specs=x_specs + [_const_spec((1, d)), _const_spec((d, n)), table_spec, table_spec] + c_in,
        out_specs=[pl.BlockSpec((None, N_RES, rows, n), lambda b, c: (b, 0, c, 0))] + c_out,
        out_shape=[jax.ShapeDtypeStruct((batch, N_RES, seq // N_RES, n), F32)] + c_shapes,
        compiler_params=_params(2),
        name="qkv_rope",
    )(*([x] * n_slabs), g, w, cos_t, sin_t, *c_ops)
    return outs[0], outs[1:]


def _attn_kernel(q_ref, k_ref, v_ref, qaug_ref, kaug_ref, o_ref,
                 s_buf, m_s, l_s, acc_s, *, seq):
    blk = ATTN_BLOCK
    grp = ATTN_GROUP
    n_groups = (seq // blk) // grp
    assert seq // N_RES == blk
    last = len(DILATIONS) - 1
    lane = lax.broadcasted_iota(jnp.int32, (blk, LANES), 1)
    first_head = lane < ATTN_HEAD_DIM

    def pick(top, bot):
        return jnp.where(first_head, top, bot)

    def block_id(d, j, u):
        n_blocks = seq // (d * blk)
        res, n = divmod(j * grp + u, n_blocks)
        if n_blocks == 1:
            return res, n, None, None
        return res, n, max(n - 1, 0), min(n, 1)

    def seg_rows(d, n):
        _, rows = _segments(d)
        return pl.ds(n * rows, rows)

    def load_block(ref, d, res, n):
        count, _ = _segments(d)
        return jnp.concatenate([ref[res + d * i, seg_rows(d, n), :] for i in range(count)], axis=0)

    def store_block(ref, branch, d, res, n, val):
        count, rows = _segments(d)
        for i in range(count):
            ref[branch, res + d * i, seg_rows(d, n), :] = val[i * rows:(i + 1) * rows]

    def scores(branch, d, j, u):
        res, n, n_prev, has_prev = block_id(d, j, u)
        qb = load_block(q_ref, d, res, n)
        q2 = jnp.concatenate([jnp.where(first_head, qb, 0.0), jnp.where(first_head, 0.0, qb)], axis=0).astype(BF16)
        q_ext = jnp.concatenate([q2, qaug_ref[...]], axis=1)
        kc = load_block(k_ref, d, res, n)
        if n_prev is None:
            kk = kc.astype(BF16)
            aug = kaug_ref[branch, 1, blk:2 * blk, :]
        else:
            kk = jnp.concatenate([load_block(k_ref, d, res, n_prev), kc], axis=0).astype(BF16)
            aug = kaug_ref[branch, has_prev]
        k_ext = jnp.concatenate([kk, aug], axis=1)
        s = lax.dot_general(q_ext, k_ext, (((1,), (1,)), ((), ())), preferred_element_type=F32)
        s_buf[(branch * n_groups + j) % 2, u, :, 0:s.shape[1]] = s

    def softmax(branch, d, j, u):
        _, _, n_prev, _ = block_id(d, j, u)
        keys = blk if n_prev is None else 2 * blk
        s = s_buf[(branch * n_groups + j) % 2, u, :, 0:keys]
        m = jnp.max(s, axis=-1, keepdims=True)
        return jnp.exp2((s - m).astype(BF16)), m

    def pv(branch, d, j, u, p, m):
        res, n, n_prev, _ = block_id(d, j, u)
        vc = load_block(v_ref, d, res, n)
        if n_prev is None:
            vv = vc.astype(BF16)
        else:
            vv = jnp.concatenate([load_block(v_ref, d, res, n_prev), vc], axis=0).astype(BF16)
        key_first_head = lax.broadcasted_iota(jnp.int32, vv.shape, 1) < ATTN_HEAD_DIM
        one = jnp.ones_like(vv)
        out0 = jnp.dot(p[:blk], jnp.where(key_first_head, vv, one), preferred_element_type=F32)
        out1 = jnp.dot(p[blk:], jnp.where(key_first_head, one, vv), preferred_element_type=F32)
        mb = jnp.broadcast_to(m, (2 * blk, LANES))
        m_b = pick(mb[:blk], mb[blk:])
        acc_b = pick(out0, out1)
        l_b = pltpu.roll(pick(out1, out0), ATTN_HEAD_DIM, 1)
        if branch < last:
            store_block(m_s, branch, d, res, n, m_b)
            store_block(l_s, branch, d, res, n, l_b)
            store_block(acc_s, branch, d, res, n, acc_b)
        else:
            ms = [m_s[i, res] for i in range(last)] + [m_b]
            ls = [l_s[i, res] for i in range(last)] + [l_b]
            accs = [acc_s[i, res] for i in range(last)] + [acc_b]
            m_all = functools.reduce(jnp.maximum, ms)
            ws = [jnp.exp2(mi - m_all) for mi in ms]
            num = sum(w * a for w, a in zip(ws, accs))
            den = sum(w * l for w, l in zip(ws, ls))
            o_ref[0, pl.ds(res, blk, stride=N_RES), :] = num / den

    def stage(branch, d, j, nxt):
        for u in range(grp):
            pv(branch, d, j, u, *softmax(branch, d, j, u))
        if nxt is not None:
            for u in range(grp):
                scores(*nxt, u)

    for u in range(grp):
        scores(0, DILATIONS[0], 0, u)
    for branch, d in enumerate(DILATIONS):
        for j in range(n_groups - 1):
            stage(branch, d, j, (branch, d, j + 1))
        following = (branch + 1, DILATIONS[branch + 1], 0) if branch + 1 < len(DILATIONS) else None
        stage(branch, d, n_groups - 1, following)


def _attention(qkv, batch, seq):
    n_pairs = ATTN_HEADS * ATTN_HEAD_DIM // LANES
    res_major = (N_RES, seq // N_RES, LANES)
    blk_spec = lambda off: pl.BlockSpec((None,) + res_major, lambda b, h: (b, 0, 0, off + h))
    nb = len(DILATIONS) - 1
    qaug, kaug = _attn_consts()
    return pl.pallas_call(
        functools.partial(_attn_kernel, seq=seq),
        grid=(batch, n_pairs),
        in_specs=[blk_spec(0), blk_spec(n_pairs), blk_spec(2 * n_pairs),
                  _const_spec(qaug.shape), _const_spec(kaug.shape)],
        out_specs=pl.BlockSpec((1, seq, LANES), lambda b, h: (b, 0, h)),
        out_shape=jax.ShapeDtypeStruct((batch, seq, D_MODEL), F32),
        scratch_shapes=[pltpu.VMEM((2, ATTN_GROUP, 2 * ATTN_BLOCK, 2 * ATTN_BLOCK), F32),
                        pltpu.VMEM((nb,) + res_major, F32),
                        pltpu.VMEM((nb,) + res_major, F32),
                        pltpu.VMEM((nb,) + res_major, F32)],
        compiler_params=_params(2),
        name="dilated_attention",
    )(qkv, qkv, qkv, qaug, kaug)


def _oproj_ffn_kernel(*refs, final_norm, n_cast):
    ins, cast_in, (o_ref,), cast_out, _ = _split_refs(refs, 7, n_cast, 1)
    x_ref, a_ref, wo_ref, g_ref, win_ref, wout_ref, gfin_ref = ins
    _run_casts(cast_in, cast_out)
    x1 = x_ref[...] + jnp.dot(a_ref[...].astype(BF16), wo_ref[...], preferred_element_type=F32)
    h = _rmsnorm(x1, g_ref[...]).astype(BF16)
    acc = x1
    for lo, hi in FFN_CHUNKS:
        gate = jnp.dot(h, win_ref[:, lo:hi], preferred_element_type=F32)
        up = jnp.dot(h, win_ref[:, D_FF + lo:D_FF + hi], preferred_element_type=F32)
        act = (gate * (1.0 / (1.0 + jnp.exp(-gate))) * up).astype(BF16)
        acc = acc + jnp.dot(act, wout_ref[lo:hi, :], preferred_element_type=F32)
    if final_norm:
        acc = _rmsnorm(acc, gfin_ref[...])
    o_ref[...] = acc


def _oproj_ffn(x, a, wo, g, w_in, w_out, g_fin, final_norm, cast=()):
    m, d = x.shape
    steps = m // ROW_TILE
    row_spec = pl.BlockSpec((ROW_TILE, d), lambda i: (i, 0))
    c_in, c_out, c_shapes, c_ops = _cast_jobs(cast, steps, lambda i: i)
    outs = pl.pallas_call(
        functools.partial(_oproj_ffn_kernel, final_norm=final_norm, n_cast=len(cast)),
        grid=(steps,),
        in_specs=[row_spec, row_spec,
                  _const_spec(wo.shape), _const_spec((1, d)),
                  _const_spec(w_in.shape), _const_spec(w_out.shape), _const_spec((1, d))] + c_in,
        out_specs=[row_spec] + c_out,
        out_shape=[jax.ShapeDtypeStruct((m, d), F32)] + c_shapes,
        compiler_params=_params(1),
        name="oproj_ffn",
    )(x, a, wo, g, w_in, w_out, g_fin, *c_ops)
    return outs[0], outs[1:]


def _rope_tables(seq):
    half = ATTN_HEAD_DIM // 2
    freqs = ROPE_THETA ** (-np.arange(half, dtype=np.float64) / half)
    ang = np.arange(seq, dtype=np.float64)[:, None] * freqs[None, :]
    cos, sin = np.cos(ang), np.sin(ang)
    reps = LANES // ATTN_HEAD_DIM
    cos_t = np.tile(np.concatenate([cos, cos], axis=1), (1, reps))
    sin_t = np.tile(np.concatenate([-sin, sin], axis=1), (1, reps))
    res_major = lambda t: jnp.asarray(t.reshape(seq // N_RES, N_RES, LANES).transpose(1, 0, 2), F32)
    return res_major(cos_t), res_major(sin_t)


def kernel(x, mix_norm, ffn_norm, w_in_rec, conv_w, hgrn_lb, hgrn_norm, w_out_rec,
           w_qkv_attn, w_o_attn, w_ffn_in, w_ffn_out, final_norm):
    batch, seq, d = x.shape
    m = batch * seq
    xf = x.reshape(m, d)
    row = lambda v: v.reshape(1, -1)

    mix, (wo0, win0, wout0) = _proj_mixer(xf, row(mix_norm[0]), w_in_rec, 0, conv_w[0], hgrn_lb, row(hgrn_norm[0]),
                                          batch, seq, cast=[(w_out_rec, 0), (w_ffn_in, 0), (w_ffn_out, 0)])
    xf, (wqkv, wo1) = _oproj_ffn(xf, mix, wo0, row(ffn_norm[0]), win0, wout0, row(final_norm), False,
                                 cast=[(w_qkv_attn, 0), (w_o_attn, 0)])

    cos_t, sin_t = _rope_tables(seq)
    qkv, (win1, wout1) = _qkv_rope(xf, row(mix_norm[1]), wqkv, cos_t, sin_t, batch, seq,
                                   cast=[(w_ffn_in, 1), (w_ffn_out, 1)])
    att = _attention(qkv, batch, seq)
    xf, _ = _oproj_ffn(xf, att.reshape(m, d), wo1, row(ffn_norm[1]), win1, wout1, row(final_norm), True)
    return xf.reshape(batch, seq, d)
```

```python
import functools

import jax
import jax.numpy as jnp
import numpy as np
from jax import lax
from jax.experimental import pallas as pl
from jax.experimental.pallas import tpu as pltpu

D_MODEL = 1024
CONV_WIDTH = 3
CONV_CHANNELS = 512
HGRN_HEAD_DIM = 128
HGRN_HEADS = 4
HGRN_WIDTH = 512
ATTN_HEAD_DIM = 64
ATTN_HEADS = 16
ATTN_BLOCK = 128
DILATIONS = (1, 4, 16)
ROPE_THETA = 10000.0
D_FF = 2816
RMS_EPS = 1e-6
REC_IN_WIDTH = 3 * CONV_CHANNELS + 4 * HGRN_WIDTH

LANES = 128
VMEM_LIMIT_BYTES = 56 * 1024 * 1024
ROW_TILE = 512
QKV_SUBTILES = 2
MIX_TILE = 256
HGRN_CHUNK = 32
MXU_TILE = 256
PROJ_CHUNKS = tuple((lo, min(lo + 3 * MXU_TILE, REC_IN_WIDTH))
                    for lo in range(0, REC_IN_WIDTH, 3 * MXU_TILE))
MASK_VALUE = -1e30
ATTN_GROUP = 8

F32 = jnp.float32
BF16 = jnp.bfloat16


def _rmsnorm(x, g):
    return x * lax.rsqrt(jnp.mean(x * x, axis=-1, keepdims=True) + RMS_EPS) * g


def _const_spec(shape):
    zeros = (0,) * len(shape)
    return pl.BlockSpec(shape, lambda *_: zeros, pipeline_mode=pl.Buffered(1))


def _params(n_grid):
    return pltpu.CompilerParams(
        dimension_semantics=("arbitrary",) * n_grid, vmem_limit_bytes=VMEM_LIMIT_BYTES)


BF16_SUBLANES = 16


def _cast_jobs(jobs, n_steps, step_of):
    in_specs, out_specs, out_shapes, operands = [], [], [], []
    for w, layer in jobs:
        _, rows, cols = w.shape
        n = n_steps
        while rows % (n * BF16_SUBLANES):
            n //= 2
        per = n_steps // n
        in_specs.append(pl.BlockSpec((None, rows // n, cols),
                                     lambda *ids, layer=layer, per=per: (layer, step_of(*ids) // per, 0)))
        out_specs.append(pl.BlockSpec((rows // n, cols), lambda *ids, per=per: (step_of(*ids) // per, 0)))
        out_shapes.append(jax.ShapeDtypeStruct((rows, cols), BF16))
        operands.append(w)
    return in_specs, out_specs, out_shapes, operands


def _split_refs(refs, n_in, n_cast, n_out):
    bounds = np.cumsum([0, n_in, n_cast, n_out, n_cast])
    return [refs[a:b] for a, b in zip(bounds[:-1], bounds[1:])] + [refs[bounds[-1]:]]


def _run_casts(cast_in, cast_out):
    for src, dst in zip(cast_in, cast_out):
        dst[...] = src[...].astype(BF16)


def _mixer_consts():
    ts, c = MIX_TILE, HGRN_CHUNK
    t = np.arange(ts)
    same = (t[:, None] // c) == (t[None, :] // c)
    tril = same & (t[:, None] >= t[None, :])
    cum = np.concatenate([tril, same]).astype(np.float32)
    ind = (t[:, None] // c == np.arange(LANES)[None, :]).astype(np.float32)
    return jnp.asarray(cum, BF16), jnp.asarray(ind, BF16)


def _proj_mixer_kernel(*refs, n_cast, tiles_per_row):
    ins, cast_in, (out_ref,), cast_out, scratch = _split_refs(refs, 10, n_cast, 1)
    x_first, x_odd, x_even, g_ref, w_ref, convw_ref, lb_ref, hnorm_ref, cum_ref, ind_ref = ins
    w_bf, proj_a, proj_b, ubuf, state, vblk = scratch
    step = pl.program_id(0)
    ts = MIX_TILE

    def projection(x_ref, dst):
        h = _rmsnorm(x_ref[...], g_ref[...]).astype(BF16)

        def job(lo, hi):
            def run():
                dst[:, lo:hi] = jnp.dot(h, w_bf[:, lo:hi], preferred_element_type=F32)
            return run

        return [job(lo, hi) for lo, hi in PROJ_CHUNKS]

    def project(x_ref, dst):
        for run in projection(x_ref, dst):
            run()

    @pl.when(step == 0)
    def _():
        w_bf[...] = w_ref[...].astype(BF16)
        project(x_first, proj_a)

    @pl.when(step % (tiles_per_row // 2) == 0)
    def _():
        state[...] = jnp.zeros_like(state)
        vblk[...] = jnp.zeros_like(vblk)
        ubuf[0:8, :] = jnp.zeros((8, CONV_CHANNELS), F32)

    mixer_refs = (convw_ref, lb_ref, hnorm_ref, cum_ref, ind_ref, out_ref)
    _mixer_tile(proj_a, *mixer_refs, slice(0, ts), ubuf, state, vblk, projection(x_odd, proj_b))
    _mixer_tile(proj_b, *mixer_refs, slice(ts, 2 * ts), ubuf, state, vblk, projection(x_even, proj_a))
    _run_casts(cast_in, cast_out)


def _mixer_tile(proj_ref, convw_ref, lb_ref, hnorm_ref, cum_ref, ind_ref, out_ref, rows, ubuf, state, vblk, fillers):
    ts = MIX_TILE
    fillers = iter(fillers)

    def fill():
        job = next(fillers, None)
        if job is not None:
            job()

    cc = CONV_CHANNELS
    c = HGRN_CHUNK
    n_chunks = ts // c

    u = proj_ref[:, cc:2 * cc] * proj_ref[:, 2 * cc:3 * cc]
    ubuf[8:8 + ts, :] = u
    y = (convw_ref[0:1, :] * ubuf[6:6 + ts, :]
         + convw_ref[1:2, :] * ubuf[7:7 + ts, :]
         + convw_ref[2:3, :] * u)
    out_ref[rows, 0:cc] = (proj_ref[:, 0:cc] * y).astype(out_ref.dtype)
    ubuf[0:8, :] = u[ts - 8:ts, :]
    fill()

    lbp = lb_ref[...]
    e = jnp.exp(lbp - jnp.max(lbp, axis=0, keepdims=True))
    sm = e / jnp.sum(e, axis=0, keepdims=True)
    cum0 = sm[0:1, :]
    cum1 = cum0 + sm[1:2, :]
    lb = cum1 - cum0
    q0 = 3 * cc
    z = proj_ref[:, q0 + HGRN_WIDTH:q0 + 2 * HGRN_WIDTH]
    ez = jnp.exp(-jnp.abs(z))
    inv = 1.0 / (1.0 + ez)
    small = ez * inv
    pos = z >= 0.0
    lf = jnp.log(lb + (1.0 - lb) * jnp.where(pos, inv, small))
    kk = (1.0 - lb) * jnp.where(pos, small, inv)

    lf_hi = lf.astype(BF16)
    lf_lo = (lf - lf_hi.astype(F32)).astype(BF16)
    sums = (jnp.dot(cum_ref[...], lf_hi, preferred_element_type=F32)
            + jnp.dot(cum_ref[...], lf_lo, preferred_element_type=F32))
    g = sums[:ts]
    gl = sums[ts:]
    tn = (((0,), (0,)), ((), ()))
    decay_col = jnp.exp(lax.dot_general(lf_hi, ind_ref[...], tn, preferred_element_type=F32)
                        + lax.dot_general(lf_lo, ind_ref[...], tn, preferred_element_type=F32))

    half = 0.5 * gl
    q = proj_ref[:, q0:q0 + HGRN_WIDTH]
    qd = (q * jnp.exp(g - half)).astype(BF16)
    kd = (kk * jnp.exp(half - g)).astype(BF16)
    qg = (q * jnp.exp(g)).astype(BF16)
    kd2 = (kk * jnp.exp(gl - g)).astype(BF16)
    i_in = proj_ref[:, q0 + 2 * HGRN_WIDTH:q0 + 3 * HGRN_WIDTH]
    v = (i_in * (1.0 / (1.0 + jnp.exp(-i_in)))).astype(BF16)

    row = lax.broadcasted_iota(jnp.int32, (ts, ts), 0)
    col = lax.broadcasted_iota(jnp.int32, (ts, ts), 1)
    intra = (row // c == col // c) & (row >= col)
    hnorm = hnorm_ref[...]

    for hd in range(HGRN_HEADS):
        lanes = slice(hd * LANES, (hd + 1) * LANES)
        vh = v[:, lanes]
        a = lax.dot_general(qd[:, lanes], kd[:, lanes], (((1,), (1,)), ((), ())), preferred_element_type=F32)
        o = jnp.dot(jnp.where(intra, a, 0.0).astype(BF16), vh, preferred_element_type=F32)
        for ci in range(n_chunks):
            vblk[hd, ci * c:(ci + 1) * c, ci * LANES:(ci + 1) * LANES] = vh[ci * c:(ci + 1) * c]
        upd = lax.dot_general(kd2[:, lanes], vblk[hd], tn, preferred_element_type=F32)
        st = state[hd]
        inter = []
        for ci in range(n_chunks):
            inter.append(jnp.dot(qg[ci * c:(ci + 1) * c, lanes], st.astype(BF16), preferred_element_type=F32))
            st = st * decay_col[lanes, ci:ci + 1] + upd[:, ci * LANES:(ci + 1) * LANES]
        state[hd] = st
        o = o + jnp.concatenate(inter, axis=0)
        o = _rmsnorm(o, hnorm)
        gh = proj_ref[:, q0 + 3 * HGRN_WIDTH + hd * LANES:q0 + 3 * HGRN_WIDTH + (hd + 1) * LANES]
        o = o * (gh * (1.0 / (1.0 + jnp.exp(-gh))))
        out_ref[rows, cc + hd * LANES:cc + (hd + 1) * LANES] = o.astype(out_ref.dtype)
        fill()
    for job in fillers:
        job()


def _proj_mixer(x, g, w, layer, conv_w, hgrn_lb, hgrn_norm, batch, seq, cast):
    ts = MIX_TILE
    d = x.shape[1]
    n = w.shape[2]
    tiles_per_row = seq // ts
    n_tiles = batch * tiles_per_row
    steps = n_tiles // 2
    assert tiles_per_row % 2 == 0
    cum, ind = _mixer_consts()
    c_in, c_out, c_shapes, c_ops = _cast_jobs(cast, steps, lambda i: i)
    tile_spec = lambda imap, **kw: pl.BlockSpec((ts, d), imap, **kw)
    outs = pl.pallas_call(
        functools.partial(_proj_mixer_kernel, n_cast=len(cast), tiles_per_row=tiles_per_row),
        grid=(steps,),
        in_specs=[tile_spec(lambda i: (0, 0), pipeline_mode=pl.Buffered(1)),
                  tile_spec(lambda i: (2 * i + 1, 0)),
                  tile_spec(lambda i: (jnp.minimum(2 * i + 2, n_tiles - 1), 0)),
                  _const_spec((1, d)),
                  pl.BlockSpec((None, d, n), lambda i: (layer, 0, 0), pipeline_mode=pl.Buffered(1)),
                  _const_spec((CONV_WIDTH, CONV_CHANNELS)),
                  _const_spec((3, HGRN_WIDTH)),
                  _const_spec((1, HGRN_HEAD_DIM)),
                  _const_spec(cum.shape), _const_spec(ind.shape)] + c_in,
        out_specs=[pl.BlockSpec((2 * ts, D_MODEL), lambda i: (i, 0))] + c_out,
        out_shape=[jax.ShapeDtypeStruct((batch * seq, D_MODEL), BF16)] + c_shapes,
        scratch_shapes=[pltpu.VMEM((d, n), BF16),
                        pltpu.VMEM((ts, n), F32),
                        pltpu.VMEM((ts, n), F32),
                        pltpu.VMEM((ts + 8, CONV_CHANNELS), F32),
                        pltpu.VMEM((HGRN_HEADS, HGRN_HEAD_DIM, HGRN_HEAD_DIM), F32),
                        pltpu.VMEM((HGRN_HEADS, ts, (ts // HGRN_CHUNK) * LANES), BF16)],
        compiler_params=_params(1),
        name="proj_conv_hgrn_mixer",
    )(x, x, x, g, w, conv_w, hgrn_lb, hgrn_norm, cum, ind, *c_ops)
    return outs[0], outs[1:]


N_RES = DILATIONS[-1]
Q_SCALE = ATTN_HEAD_DIM ** -0.5 * float(np.log2(np.e))


def _segments(d):
    count = N_RES // d
    return count, ATTN_BLOCK // count


def _attn_consts():
    blk = ATTN_BLOCK
    e = np.arange(blk)
    none = np.full((blk, blk), MASK_VALUE)
    kaug = []
    for d in DILATIONS:
        count, rows = _segments(d)
        pos = (e % rows) * count + e // rows
        kpos, qpos = pos[:, None], pos[None, :]
        prev_ok = np.where(kpos >= qpos, 0.0, MASK_VALUE)
        cur_ok = np.where(kpos <= qpos, 0.0, MASK_VALUE)
        kaug.append(np.stack([np.concatenate([none, cur_ok]), np.concatenate([prev_ok, cur_ok])]))
    qaug = np.concatenate([np.eye(blk), np.eye(blk)])
    return jnp.asarray(qaug, BF16), jnp.asarray(np.stack(kaug), BF16)


def _qkv_rope_kernel(*refs, n_cast):
    n_slabs = D_MODEL // LANES
    ins, cast_in, (o_ref,), cast_out, _ = _split_refs(refs, n_slabs + 4, n_cast, 1)
    x_refs = ins[:n_slabs]
    g_ref, w_ref, cos_ref, sin_ref = ins[n_slabs:]
    _run_casts(cast_in, cast_out)
    rows = ROW_TILE // N_RES
    lane = lax.broadcasted_iota(jnp.int32, (ROW_TILE, LANES), 1)
    first_half = (lane % ATTN_HEAD_DIM) < (ATTN_HEAD_DIM // 2)
    width = ATTN_HEADS * ATTN_HEAD_DIM
    for sub in range(QKV_SUBTILES):
        x = jnp.concatenate(
            [jnp.concatenate([ref[pl.ds(sub * ROW_TILE + r, rows, stride=N_RES), :] for r in range(N_RES)], axis=0)
             for ref in x_refs], axis=1)
        h = _rmsnorm(x, g_ref[...]).astype(BF16)
        out_rows = slice(sub * rows, (sub + 1) * rows)
        cos = cos_ref[:, out_rows, :].reshape(ROW_TILE, LANES)
        sin = sin_ref[:, out_rows, :].reshape(ROW_TILE, LANES)
        for part in range(3):
            y = jnp.dot(h, w_ref[:, part * width:(part + 1) * width], preferred_element_type=F32)
            if part < 2:
                cols = []
                for cb in range(width // LANES):
                    t = y[:, cb * LANES:(cb + 1) * LANES]
                    partner = jnp.where(first_half,
                                        pltpu.roll(t, LANES - ATTN_HEAD_DIM // 2, 1),
                                        pltpu.roll(t, ATTN_HEAD_DIM // 2, 1))
                    t = t * cos + partner * sin
                    cols.append(t * Q_SCALE if part == 0 else t)
                y = jnp.concatenate(cols, axis=1)
            o_ref[:, out_rows, part * width:(part + 1) * width] = y.reshape(N_RES, rows, width)


def _qkv_rope(x, g, w, cos_t, sin_t, batch, seq, cast):
    d = x.shape[-1]
    n = w.shape[1]
    step_rows = QKV_SUBTILES * ROW_TILE
    rows = step_rows // N_RES
    tiles = seq // step_rows
    n_slabs = d // LANES
    x_specs = [pl.BlockSpec((step_rows, LANES), functools.partial(lambda b, c, s: (b * tiles + c, s), s=s))
               for s in range(n_slabs)]
    table_spec = pl.BlockSpec((N_RES, rows, LANES), lambda b, c: (0, c, 0))
    c_in, c_out, c_shapes, c_ops = _cast_jobs(cast, batch * tiles, lambda b, c: b * tiles + c)
    outs = pl.pallas_call(
        functools.partial(_qkv_rope_kernel, n_cast=len(cast)),
        grid=(batch, tiles),
        in_specs=x_specs + [_const_spec((1, d)), _const_spec((d, n)), table_spec, table_spec] + c_in,
        out_specs=[pl.BlockSpec((None, N_RES, rows, n), lambda b, c: (b, 0, c, 0))] + c_out,
        out_shape=[jax.ShapeDtypeStruct((batch, N_RES, seq // N_RES, n), F32)] + c_shapes,
        compiler_params=_params(2),
        name="qkv_rope",
    )(*([x] * n_slabs), g, w, cos_t, sin_t, *c_ops)
    return outs[0], outs[1:]


def _attn_kernel(q_ref, k_ref, v_ref, qaug_ref, kaug_ref, o_ref,
                 s_buf, m_s, l_s, acc_s, *, seq):
    blk = ATTN_BLOCK
    grp = ATTN_GROUP
    n_groups = (seq // blk) // grp
    assert seq // N_RES == blk
    last = len(DILATIONS) - 1
    lane = lax.broadcasted_iota(jnp.int32, (blk, LANES), 1)
    first_head = lane < ATTN_HEAD_DIM

    def pick(top, bot):
        return jnp.where(first_head, top, bot)

    def block_id(d, j, u):
        n_blocks = seq // (d * blk)
        res, n = divmod(j * grp + u, n_blocks)
        if n_blocks == 1:
            return res, n, None, None
        return res, n, max(n - 1, 0), min(n, 1)

    def seg_rows(d, n):
        _, rows = _segments(d)
        return pl.ds(n * rows, rows)

    def load_block(ref, d, res, n):
        count, _ = _segments(d)
        return jnp.concatenate([ref[res + d * i, seg_rows(d, n), :] for i in range(count)], axis=0)

    def store_block(ref, branch, d, res, n, val):
        count, rows = _segments(d)
        for i in range(count):
            ref[branch, res + d * i, seg_rows(d, n), :] = val[i * rows:(i + 1) * rows]

    def scores(branch, d, j, u):
        res, n, n_prev, has_prev = block_id(d, j, u)
        qb = load_block(q_ref, d, res, n)
        q2 = jnp.concatenate([jnp.where(first_head, qb, 0.0), jnp.where(first_head, 0.0, qb)], axis=0).astype(BF16)
        q_ext = jnp.concatenate([q2, qaug_ref[...]], axis=1)
        kc = load_block(k_ref, d, res, n)
        if n_prev is None:
            kk = kc.astype(BF16)
            aug = kaug_ref[branch, 1, blk:2 * blk, :]
        else:
            kk = jnp.concatenate([load_block(k_ref, d, res, n_prev), kc], axis=0).astype(BF16)
            aug = kaug_ref[branch, has_prev]
        k_ext = jnp.concatenate([kk, aug], axis=1)
        s = lax.dot_general(q_ext, k_ext, (((1,), (1,)), ((), ())), preferred_element_type=F32)
        s_buf[(branch * n_groups + j) % 2, u, :, 0:s.shape[1]] = s

    def softmax(branch, d, j, u):
        _, _, n_prev, _ = block_id(d, j, u)
        keys = blk if n_prev is None else 2 * blk
        s = s_buf[(branch * n_groups + j) % 2, u, :, 0:keys]
        m = jnp.max(s, axis=-1, keepdims=True)
        return jnp.exp2((s - m).astype(BF16)), m

    def pv(branch, d, j, u, p, m):
        res, n, n_prev, _ = block_id(d, j, u)
        vc = load_block(v_ref, d, res, n)
        if n_prev is None:
            vv = vc.astype(BF16)
        else:
            vv = jnp.concatenate([load_block(v_ref, d, res, n_prev), vc], axis=0).astype(BF16)
        key_first_head = lax.broadcasted_iota(jnp.int32, vv.shape, 1) < ATTN_HEAD_DIM
        one = jnp.ones_like(vv)
        out0 = jnp.dot(p[:blk], jnp.where(key_first_head, vv, one), preferred_element_type=F32)
        out1 = jnp.dot(p[blk:], jnp.where(key_first_head, one, vv), preferred_element_type=F32)
        mb = jnp.broadcast_to(m, (2 * blk, LANES))
        m_b = pick(mb[:blk], mb[blk:])
        acc_b = pick(out0, out1)
        l_b = pltpu.roll(pick(out1, out0), ATTN_HEAD_DIM, 1)
        if branch < last:
            store_block(m_s, branch, d, res, n, m_b)
            store_block(l_s, branch, d, res, n, l_b)
            store_block(acc_s, branch, d, res, n, acc_b)
        else:
            ms = [m_s[i, res] for i in range(last)] + [m_b]
            ls = [l_s[i, res] for i in range(last)] + [l_b]
            accs = [acc_s[i, res] for i in range(last)] + [acc_b]
            m_all = functools.reduce(jnp.maximum, ms)
            ws = [jnp.exp2(mi - m_all) for mi in ms]
            num = sum(w * a for w, a in zip(ws, accs))
            den = sum(w * l for w, l in zip(ws, ls))
            o_ref[0, pl.ds(res, blk, stride=N_RES), :] = num / den

    def stage(branch, d, j, nxt):
        for u in range(grp):
            pv(branch, d, j, u, *softmax(branch, d, j, u))
        if nxt is not None:
            for u in range(grp):
                scores(*nxt, u)

    for u in range(grp):
        scores(0, DILATIONS[0], 0, u)
    for branch, d in enumerate(DILATIONS):
        for j in range(n_groups - 1):
            stage(branch, d, j, (branch, d, j + 1))
        following = (branch + 1, DILATIONS[branch + 1], 0) if branch + 1 < len(DILATIONS) else None
        stage(branch, d, n_groups - 1, following)


def _attention(qkv, batch, seq):
    n_pairs = ATTN_HEADS * ATTN_HEAD_DIM // LANES
    res_major = (N_RES, seq // N_RES, LANES)
    blk_spec = lambda off: pl.BlockSpec((None,) + res_major, lambda b, h: (b, 0, 0, off + h))
    nb = len(DILATIONS) - 1
    qaug, kaug = _attn_consts()
    return pl.pallas_call(
        functools.partial(_attn_kernel, seq=seq),
        grid=(batch, n_pairs),
        in_specs=[blk_spec(0), blk_spec(n_pairs), blk_spec(2 * n_pairs),
                  _const_spec(qaug.shape), _const_spec(kaug.shape)],
        out_specs=pl.BlockSpec((1, seq, LANES), lambda b, h: (b, 0, h)),
        out_shape=jax.ShapeDtypeStruct((batch, seq, D_MODEL), F32),
        scratch_shapes=[pltpu.VMEM((2, ATTN_GROUP, 2 * ATTN_BLOCK, 2 * ATTN_BLOCK), F32),
                        pltpu.VMEM((nb,) + res_major, F32),
                        pltpu.VMEM((nb,) + res_major, F32),
                        pltpu.VMEM((nb,) + res_major, F32)],
        compiler_params=_params(2),
        name="dilated_attention",
    )(qkv, qkv, qkv, qaug, kaug)


def _oproj_ffn_kernel(*refs, final_norm, n_cast):
    ins, cast_in, (o_ref,), cast_out, _ = _split_refs(refs, 7, n_cast, 1)
    x_ref, a_ref, wo_ref, g_ref, win_ref, wout_ref, gfin_ref = ins
    _run_casts(cast_in, cast_out)
    half = ROW_TILE // 2
    halves = [slice(0, half), slice(half, ROW_TILE)]

    def oproj_norm(rows):
        x1 = x_ref[rows, :] + jnp.dot(a_ref[rows, :].astype(BF16), wo_ref[...], preferred_element_type=F32)
        return x1, _rmsnorm(x1, g_ref[...]).astype(BF16)

    def ffn(rows, x1, h):
        gate = jnp.dot(h, win_ref[:, 0:D_FF], preferred_element_type=F32)
        up = jnp.dot(h, win_ref[:, D_FF:2 * D_FF], preferred_element_type=F32)
        act = (gate * (1.0 / (1.0 + jnp.exp(-gate))) * up).astype(BF16)
        acc = x1 + jnp.dot(act, wout_ref[...], preferred_element_type=F32)
        if final_norm:
            acc = _rmsnorm(acc, gfin_ref[...])
        o_ref[rows, :] = acc

    heads = [oproj_norm(rows) for rows in halves]
    for rows, (x1, h) in zip(halves, heads):
        ffn(rows, x1, h)


def _oproj_ffn(x, a, wo, g, w_in, w_out, g_fin, final_norm, cast=()):
    m, d = x.shape
    steps = m // ROW_TILE
    row_spec = pl.BlockSpec((ROW_TILE, d), lambda i: (i, 0))
    c_in, c_out, c_shapes, c_ops = _cast_jobs(cast, steps, lambda i: i)
    outs = pl.pallas_call(
        functools.partial(_oproj_ffn_kernel, final_norm=final_norm, n_cast=len(cast)),
        grid=(steps,),
        in_specs=[row_spec, row_spec,
                  _const_spec(wo.shape), _const_spec((1, d)),
                  _const_spec(w_in.shape), _const_spec(w_out.shape), _const_spec((1, d))] + c_in,
        out_specs=[row_spec] + c_out,
        out_shape=[jax.ShapeDtypeStruct((m, d), F32)] + c_shapes,
        compiler_params=_params(1),
        name="oproj_ffn",
    )(x, a, wo, g, w_in, w_out, g_fin, *c_ops)
    return outs[0], outs[1:]


def _rope_tables(seq):
    half = ATTN_HEAD_DIM // 2
    freqs = ROPE_THETA ** (-np.arange(half, dtype=np.float64) / half)
    ang = np.arange(seq, dtype=np.float64)[:, None] * freqs[None, :]
    cos, sin = np.cos(ang), np.sin(ang)
    reps = LANES // ATTN_HEAD_DIM
    cos_t = np.tile(np.concatenate([cos, cos], axis=1), (1, reps))
    sin_t = np.tile(np.concatenate([-sin, sin], axis=1), (1, reps))
    res_major = lambda t: jnp.asarray(t.reshape(seq // N_RES, N_RES, LANES).transpose(1, 0, 2), F32)
    return res_major(cos_t), res_major(sin_t)


def kernel(x, mix_norm, ffn_norm, w_in_rec, conv_w, hgrn_lb, hgrn_norm, w_out_rec,
           w_qkv_attn, w_o_attn, w_ffn_in, w_ffn_out, final_norm):
    batch, seq, d = x.shape
    m = batch * seq
    xf = x.reshape(m, d)
    row = lambda v: v.reshape(1, -1)

    mix, (wo0, win0, wout0) = _proj_mixer(xf, row(mix_norm[0]), w_in_rec, 0, conv_w[0], hgrn_lb, row(hgrn_norm[0]),
                                          batch, seq, cast=[(w_out_rec, 0), (w_ffn_in, 0), (w_ffn_out, 0)])
    xf, (wqkv, wo1) = _oproj_ffn(xf, mix, wo0, row(ffn_norm[0]), win0, wout0, row(final_norm), False,
                                 cast=[(w_qkv_attn, 0), (w_o_attn, 0)])

    cos_t, sin_t = _rope_tables(seq)
    qkv, (win1, wout1) = _qkv_rope(xf, row(mix_norm[1]), wqkv, cos_t, sin_t, batch, seq,
                                   cast=[(w_ffn_in, 1), (w_ffn_out, 1)])
    att = _attention(qkv, batch, seq)
    xf, _ = _oproj_ffn(xf, att.reshape(m, d), wo1, row(ffn_norm[1]), win1, wout1, row(final_norm), True)
    return xf.reshape(batch, seq, d)
```

```python
import functools

import jax
import jax.numpy as jnp
import numpy as np
from jax import lax
from jax.experimental import pallas as pl
from jax.experimental.pallas import tpu as pltpu

D_MODEL = 1024
CONV_WIDTH = 3
CONV_CHANNELS = 512
HGRN_HEAD_DIM = 128
HGRN_HEADS = 4
HGRN_WIDTH = 512
ATTN_HEAD_DIM = 64
ATTN_HEADS = 16
ATTN_BLOCK = 128
DILATIONS = (1, 4, 16)
ROPE_THETA = 10000.0
D_FF = 2816
RMS_EPS = 1e-6
REC_IN_WIDTH = 3 * CONV_CHANNELS + 4 * HGRN_WIDTH

LANES = 128
VMEM_LIMIT_BYTES = 56 * 1024 * 1024
ROW_TILE = 512
QKV_SUBTILES = 2
MIX_TILE = 256
HGRN_CHUNK = 32
MXU_TILE = 256
PROJ_CHUNKS = tuple((lo, min(lo + 3 * MXU_TILE, REC_IN_WIDTH))
                    for lo in range(0, REC_IN_WIDTH, 3 * MXU_TILE))
MASK_VALUE = -1e30
ATTN_GROUP = 8

F32 = jnp.float32
BF16 = jnp.bfloat16


def _rmsnorm(x, g):
    return x * lax.rsqrt(jnp.mean(x * x, axis=-1, keepdims=True) + RMS_EPS) * g


def _const_spec(shape):
    zeros = (0,) * len(shape)
    return pl.BlockSpec(shape, lambda *_: zeros, pipeline_mode=pl.Buffered(1))


def _params(n_grid):
    return pltpu.CompilerParams(
        dimension_semantics=("arbitrary",) * n_grid, vmem_limit_bytes=VMEM_LIMIT_BYTES)


BF16_SUBLANES = 16


def _cast_jobs(jobs, n_steps, step_of):
    in_specs, out_specs, out_shapes, operands = [], [], [], []
    for w, layer in jobs:
        _, rows, cols = w.shape
        n = n_steps
        while rows % (n * BF16_SUBLANES):
            n //= 2
        per = n_steps // n
        in_specs.append(pl.BlockSpec((None, rows // n, cols),
                                     lambda *ids, layer=layer, per=per: (layer, step_of(*ids) // per, 0)))
        out_specs.append(pl.BlockSpec((rows // n, cols), lambda *ids, per=per: (step_of(*ids) // per, 0)))
        out_shapes.append(jax.ShapeDtypeStruct((rows, cols), BF16))
        operands.append(w)
    return in_specs, out_specs, out_shapes, operands


def _split_refs(refs, n_in, n_cast, n_out):
    bounds = np.cumsum([0, n_in, n_cast, n_out, n_cast])
    return [refs[a:b] for a, b in zip(bounds[:-1], bounds[1:])] + [refs[bounds[-1]:]]


def _run_casts(cast_in, cast_out):
    for src, dst in zip(cast_in, cast_out):
        dst[...] = src[...].astype(BF16)


def _mixer_consts():
    ts, c = MIX_TILE, HGRN_CHUNK
    t = np.arange(ts)
    same = (t[:, None] // c) == (t[None, :] // c)
    tril = same & (t[:, None] >= t[None, :])
    cum = np.concatenate([tril, same]).astype(np.float32)
    ind = (t[:, None] // c == np.arange(LANES)[None, :]).astype(np.float32)
    return jnp.asarray(cum, BF16), jnp.asarray(ind, BF16)


def _proj_mixer_kernel(*refs, n_cast, tiles_per_row):
    ins, cast_in, (out_ref,), cast_out, scratch = _split_refs(refs, 10, n_cast, 1)
    x_first, x_odd, x_even, g_ref, w_ref, convw_ref, lb_ref, hnorm_ref, cum_ref, ind_ref = ins
    w_bf, proj_a, proj_b, ubuf, state, vblk = scratch
    step = pl.program_id(0)
    ts = MIX_TILE

    def projection(x_ref, dst):
        h = _rmsnorm(x_ref[...], g_ref[...]).astype(BF16)

        def job(lo, hi):
            def run():
                dst[:, lo:hi] = jnp.dot(h, w_bf[:, lo:hi], preferred_element_type=F32)
            return run

        return [job(lo, hi) for lo, hi in PROJ_CHUNKS]

    def project(x_ref, dst):
        for run in projection(x_ref, dst):
            run()

    @pl.when(step == 0)
    def _():
        w_bf[...] = w_ref[...].astype(BF16)
        project(x_first, proj_a)

    @pl.when(step % (tiles_per_row // 2) == 0)
    def _():
        state[...] = jnp.zeros_like(state)
        vblk[...] = jnp.zeros_like(vblk)
        ubuf[0:8, :] = jnp.zeros((8, CONV_CHANNELS), F32)

    mixer_refs = (convw_ref, lb_ref, hnorm_ref, cum_ref, ind_ref, out_ref)
    _mixer_tile(proj_a, *mixer_refs, slice(0, ts), ubuf, state, vblk, projection(x_odd, proj_b))
    _mixer_tile(proj_b, *mixer_refs, slice(ts, 2 * ts), ubuf, state, vblk, projection(x_even, proj_a))
    _run_casts(cast_in, cast_out)


def _mixer_tile(proj_ref, convw_ref, lb_ref, hnorm_ref, cum_ref, ind_ref, out_ref, rows, ubuf, state, vblk, fillers):
    ts = MIX_TILE
    fillers = iter(fillers)

    def fill():
        job = next(fillers, None)
        if job is not None:
            job()

    cc = CONV_CHANNELS
    c = HGRN_CHUNK
    n_chunks = ts // c

    u = proj_ref[:, cc:2 * cc] * proj_ref[:, 2 * cc:3 * cc]
    ubuf[8:8 + ts, :] = u
    y = (convw_ref[0:1, :] * ubuf[6:6 + ts, :]
         + convw_ref[1:2, :] * ubuf[7:7 + ts, :]
         + convw_ref[2:3, :] * u)
    out_ref[rows, 0:cc] = (proj_ref[:, 0:cc] * y).astype(out_ref.dtype)
    ubuf[0:8, :] = u[ts - 8:ts, :]
    fill()

    lbp = lb_ref[...]
    e = jnp.exp(lbp - jnp.max(lbp, axis=0, keepdims=True))
    sm = e / jnp.sum(e, axis=0, keepdims=True)
    cum0 = sm[0:1, :]
    cum1 = cum0 + sm[1:2, :]
    lb = cum1 - cum0
    q0 = 3 * cc
    z = proj_ref[:, q0 + HGRN_WIDTH:q0 + 2 * HGRN_WIDTH]
    ez = jnp.exp(-jnp.abs(z))
    inv = 1.0 / (1.0 + ez)
    small = ez * inv
    pos = z >= 0.0
    lf = jnp.log(lb + (1.0 - lb) * jnp.where(pos, inv, small))
    kk = (1.0 - lb) * jnp.where(pos, small, inv)

    lf_hi = lf.astype(BF16)
    lf_lo = (lf - lf_hi.astype(F32)).astype(BF16)
    sums = (jnp.dot(cum_ref[...], lf_hi, preferred_element_type=F32)
            + jnp.dot(cum_ref[...], lf_lo, preferred_element_type=F32))
    g = sums[:ts]
    gl = sums[ts:]
    tn = (((0,), (0,)), ((), ()))
    decay_col = jnp.exp(lax.dot_general(lf_hi, ind_ref[...], tn, preferred_element_type=F32)
                        + lax.dot_general(lf_lo, ind_ref[...], tn, preferred_element_type=F32))

    half = 0.5 * gl
    q = proj_ref[:, q0:q0 + HGRN_WIDTH]
    qd = (q * jnp.exp(g - half)).astype(BF16)
    kd = (kk * jnp.exp(half - g)).astype(BF16)
    qg = (q * jnp.exp(g)).astype(BF16)
    kd2 = (kk * jnp.exp(gl - g)).astype(BF16)
    i_in = proj_ref[:, q0 + 2 * HGRN_WIDTH:q0 + 3 * HGRN_WIDTH]
    v = (i_in * (1.0 / (1.0 + jnp.exp(-i_in)))).astype(BF16)

    row = lax.broadcasted_iota(jnp.int32, (ts, ts), 0)
    col = lax.broadcasted_iota(jnp.int32, (ts, ts), 1)
    intra = (row // c == col // c) & (row >= col)
    hnorm = hnorm_ref[...]

    intra_out, updates = [], []
    for hd in range(HGRN_HEADS):
        lanes = slice(hd * LANES, (hd + 1) * LANES)
        vh = v[:, lanes]
        a = lax.dot_general(qd[:, lanes], kd[:, lanes], (((1,), (1,)), ((), ())), preferred_element_type=F32)
        intra_out.append(jnp.dot(jnp.where(intra, a, 0.0).astype(BF16), vh, preferred_element_type=F32))
        for ci in range(n_chunks):
            vblk[hd, ci * c:(ci + 1) * c, ci * LANES:(ci + 1) * LANES] = vh[ci * c:(ci + 1) * c]
        updates.append(lax.dot_general(kd2[:, lanes], vblk[hd], tn, preferred_element_type=F32))

    for hd in range(HGRN_HEADS):
        lanes = slice(hd * LANES, (hd + 1) * LANES)
        o, upd = intra_out[hd], updates[hd]
        st = state[hd]
        inter = []
        for ci in range(n_chunks):
            inter.append(jnp.dot(qg[ci * c:(ci + 1) * c, lanes], st.astype(BF16), preferred_element_type=F32))
            st = st * decay_col[lanes, ci:ci + 1] + upd[:, ci * LANES:(ci + 1) * LANES]
        state[hd] = st
        o = o + jnp.concatenate(inter, axis=0)
        o = _rmsnorm(o, hnorm)
        gh = proj_ref[:, q0 + 3 * HGRN_WIDTH + hd * LANES:q0 + 3 * HGRN_WIDTH + (hd + 1) * LANES]
        o = o * (gh * (1.0 / (1.0 + jnp.exp(-gh))))
        out_ref[rows, cc + hd * LANES:cc + (hd + 1) * LANES] = o.astype(out_ref.dtype)
        fill()
    for job in fillers:
        job()


def _proj_mixer(x, g, w, layer, conv_w, hgrn_lb, hgrn_norm, batch, seq, cast):
    ts = MIX_TILE
    d = x.shape[1]
    n = w.shape[2]
    tiles_per_row = seq // ts
    n_tiles = batch * tiles_per_row
    steps = n_tiles // 2
    assert tiles_per_row % 2 == 0
    cum, ind = _mixer_consts()
    c_in, c_out, c_shapes, c_ops = _cast_jobs(cast, steps, lambda i: i)
    tile_spec = lambda imap, **kw: pl.BlockSpec((ts, d), imap, **kw)
    outs = pl.pallas_call(
        functools.partial(_proj_mixer_kernel, n_cast=len(cast), tiles_per_row=tiles_per_row),
        grid=(steps,),
        in_specs=[tile_spec(lambda i: (0, 0), pipeline_mode=pl.Buffered(1)),
                  tile_spec(lambda i: (2 * i + 1, 0)),
                  tile_spec(lambda i: (jnp.minimum(2 * i + 2, n_tiles - 1), 0)),
                  _const_spec((1, d)),
                  pl.BlockSpec((None, d, n), lambda i: (layer, 0, 0), pipeline_mode=pl.Buffered(1)),
                  _const_spec((CONV_WIDTH, CONV_CHANNELS)),
                  _const_spec((3, HGRN_WIDTH)),
                  _const_spec((1, HGRN_HEAD_DIM)),
                  _const_spec(cum.shape), _const_spec(ind.shape)] + c_in,
        out_specs=[pl.BlockSpec((2 * ts, D_MODEL), lambda i: (i, 0))] + c_out,
        out_shape=[jax.ShapeDtypeStruct((batch * seq, D_MODEL), BF16)] + c_shapes,
        scratch_shapes=[pltpu.VMEM((d, n), BF16),
                        pltpu.VMEM((ts, n), F32),
                        pltpu.VMEM((ts, n), F32),
                        pltpu.VMEM((ts + 8, CONV_CHANNELS), F32),
                        pltpu.VMEM((HGRN_HEADS, HGRN_HEAD_DIM, HGRN_HEAD_DIM), F32),
                        pltpu.VMEM((HGRN_HEADS, ts, (ts // HGRN_CHUNK) * LANES), BF16)],
        compiler_params=_params(1),
        name="proj_conv_hgrn_mixer",
    )(x, x, x, g, w, conv_w, hgrn_lb, hgrn_norm, cum, ind, *c_ops)
    return outs[0], outs[1:]


N_RES = DILATIONS[-1]
Q_SCALE = ATTN_HEAD_DIM ** -0.5 * float(np.log2(np.e))


def _segments(d):
    count = N_RES // d
    return count, ATTN_BLOCK // count


def _attn_consts():
    blk = ATTN_BLOCK
    e = np.arange(blk)
    none = np.full((blk, blk), MASK_VALUE)
    kaug = []
    for d in DILATIONS:
        count, rows = _segments(d)
        pos = (e % rows) * count + e // rows
        kpos, qpos = pos[:, None], pos[None, :]
        prev_ok = np.where(kpos >= qpos, 0.0, MASK_VALUE)
        cur_ok = np.where(kpos <= qpos, 0.0, MASK_VALUE)
        kaug.append(np.stack([np.concatenate([none, cur_ok]), np.concatenate([prev_ok, cur_ok])]))
    qaug = np.concatenate([np.eye(blk), np.eye(blk)])
    return jnp.asarray(qaug, BF16), jnp.asarray(np.stack(kaug), BF16)


def _qkv_rope_kernel(*refs, n_cast):
    n_slabs = D_MODEL // LANES
    ins, cast_in, (o_ref,), cast_out, _ = _split_refs(refs, n_slabs + 4, n_cast, 1)
    x_refs = ins[:n_slabs]
    g_ref, w_ref, cos_ref, sin_ref = ins[n_slabs:]
    _run_casts(cast_in, cast_out)
    rows = ROW_TILE // N_RES
    lane = lax.broadcasted_iota(jnp.int32, (ROW_TILE, LANES), 1)
    first_half = (lane % ATTN_HEAD_DIM) < (ATTN_HEAD_DIM // 2)
    width = ATTN_HEADS * ATTN_HEAD_DIM
    for sub in range(QKV_SUBTILES):
        x = jnp.concatenate(
            [jnp.concatenate([ref[pl.ds(sub * ROW_TILE + r, rows, stride=N_RES), :] for r in range(N_RES)], axis=0)
             for ref in x_refs], axis=1)
        h = _rmsnorm(x, g_ref[...]).astype(BF16)
        out_rows = slice(sub * rows, (sub + 1) * rows)
        cos = cos_ref[:, out_rows, :].reshape(ROW_TILE, LANES)
        sin = sin_ref[:, out_rows, :].reshape(ROW_TILE, LANES)
        for part in range(3):
            y = jnp.dot(h, w_ref[:, part * width:(part + 1) * width], preferred_element_type=F32)
            if part < 2:
                cols = []
                for cb in range(width // LANES):
                    t = y[:, cb * LANES:(cb + 1) * LANES]
                    partner = jnp.where(first_half,
                                        pltpu.roll(t, LANES - ATTN_HEAD_DIM // 2, 1),
                                        pltpu.roll(t, ATTN_HEAD_DIM // 2, 1))
                    t = t * cos + partner * sin
                    cols.append(t * Q_SCALE if part == 0 else t)
                y = jnp.concatenate(cols, axis=1)
            o_ref[:, out_rows, part * width:(part + 1) * width] = y.reshape(N_RES, rows, width)


def _qkv_rope(x, g, w, cos_t, sin_t, batch, seq, cast):
    d = x.shape[-1]
    n = w.shape[1]
    step_rows = QKV_SUBTILES * ROW_TILE
    rows = step_rows // N_RES
    tiles = seq // step_rows
    n_slabs = d // LANES
    x_specs = [pl.BlockSpec((step_rows, LANES), functools.partial(lambda b, c, s: (b * tiles + c, s), s=s))
               for s in range(n_slabs)]
    table_spec = pl.BlockSpec((N_RES, rows, LANES), lambda b, c: (0, c, 0))
    c_in, c_out, c_shapes, c_ops = _cast_jobs(cast, batch * tiles, lambda b, c: b * tiles + c)
    outs = pl.pallas_call(
        functools.partial(_qkv_rope_kernel, n_cast=len(cast)),
        grid=(batch, tiles),
        in_specs=x_specs + [_const_spec((1, d)), _const_spec((d, n)), table_spec, table_spec] + c_in,
        out_specs=[pl.BlockSpec((None, N_RES, rows, n), lambda b, c: (b, 0, c, 0))] + c_out,
        out_shape=[jax.ShapeDtypeStruct((batch, N_RES, seq // N_RES, n), F32)] + c_shapes,
        compiler_params=_params(2),
        name="qkv_rope",
    )(*([x] * n_slabs), g, w, cos_t, sin_t, *c_ops)
    return outs[0], outs[1:]


def _attn_kernel(q_ref, k_ref, v_ref, qaug_ref, kaug_ref, o_ref,
                 s_buf, m_s, l_s, acc_s, *, seq):
    blk = ATTN_BLOCK
    grp = ATTN_GROUP
    n_groups = (seq // blk) // grp
    assert seq // N_RES == blk
    last = len(DILATIONS) - 1
    lane = lax.broadcasted_iota(jnp.int32, (blk, LANES), 1)
    first_head = lane < ATTN_HEAD_DIM

    def pick(top, bot):
        return jnp.where(first_head, top, bot)

    def block_id(d, j, u):
        n_blocks = seq // (d * blk)
        res, n = divmod(j * grp + u, n_blocks)
        if n_blocks == 1:
            return res, n, None, None
        return res, n, max(n - 1, 0), min(n, 1)

    def seg_rows(d, n):
        _, rows = _segments(d)
        return pl.ds(n * rows, rows)

    def load_block(ref, d, res, n):
        count, _ = _segments(d)
        return jnp.concatenate([ref[res + d * i, seg_rows(d, n), :] for i in range(count)], axis=0)

    def store_block(ref, branch, d, res, n, val):
        count, rows = _segments(d)
        for i in range(count):
            ref[branch, res + d * i, seg_rows(d, n), :] = val[i * rows:(i + 1) * rows]

    def scores(branch, d, j, u):
        res, n, n_prev, has_prev = block_id(d, j, u)
        qb = load_block(q_ref, d, res, n)
        q2 = jnp.concatenate([jnp.where(first_head, qb, 0.0), jnp.where(first_head, 0.0, qb)], axis=0).astype(BF16)
        q_ext = jnp.concatenate([q2, qaug_ref[...]], axis=1)
        kc = load_block(k_ref, d, res, n)
        if n_prev is None:
            kk = kc.astype(BF16)
            aug = kaug_ref[branch, 1, blk:2 * blk, :]
        else:
            kk = jnp.concatenate([load_block(k_ref, d, res, n_prev), kc], axis=0).astype(BF16)
            aug = kaug_ref[branch, has_prev]
        k_ext = jnp.concatenate([kk, aug], axis=1)
        s = lax.dot_general(q_ext, k_ext, (((1,), (1,)), ((), ())), preferred_element_type=F32)
        s_buf[(branch * n_groups + j) % 2, u, :, 0:s.shape[1]] = s

    def softmax(branch, d, j, u):
        _, _, n_prev, _ = block_id(d, j, u)
        keys = blk if n_prev is None else 2 * blk
        s = s_buf[(branch * n_groups + j) % 2, u, :, 0:keys]
        m = jnp.max(s, axis=-1, keepdims=True)
        return jnp.exp2((s - m).astype(BF16)), m

    def pv(branch, d, j, u, p, m):
        res, n, n_prev, _ = block_id(d, j, u)
        vc = load_block(v_ref, d, res, n)
        if n_prev is None:
            vv = vc.astype(BF16)
        else:
            vv = jnp.concatenate([load_block(v_ref, d, res, n_prev), vc], axis=0).astype(BF16)
        key_first_head = lax.broadcasted_iota(jnp.int32, vv.shape, 1) < ATTN_HEAD_DIM
        one = jnp.ones_like(vv)
        out0 = jnp.dot(p[:blk], jnp.where(key_first_head, vv, one), preferred_element_type=F32)
        out1 = jnp.dot(p[blk:], jnp.where(key_first_head, one, vv), preferred_element_type=F32)
        mb = jnp.broadcast_to(m, (2 * blk, LANES))
        m_b = pick(mb[:blk], mb[blk:])
        acc_b = pick(out0, out1)
        l_b = pltpu.roll(pick(out1, out0), ATTN_HEAD_DIM, 1)
        if branch < last:
            store_block(m_s, branch, d, res, n, m_b)
            store_block(l_s, branch, d, res, n, l_b)
            store_block(acc_s, branch, d, res, n, acc_b)
        else:
            ms = [m_s[i, res] for i in range(last)] + [m_b]
            ls = [l_s[i, res] for i in range(last)] + [l_b]
            accs = [acc_s[i, res] for i in range(last)] + [acc_b]
            m_all = functools.reduce(jnp.maximum, ms)
            ws = [jnp.exp2(mi - m_all) for mi in ms]
            num = sum(w * a for w, a in zip(ws, accs))
            den = sum(w * l for w, l in zip(ws, ls))
            o_ref[0, pl.ds(res, blk, stride=N_RES), :] = num / den

    def stage(branch, d, j, nxt):
        for u in range(grp):
            pv(branch, d, j, u, *softmax(branch, d, j, u))
        if nxt is not None:
            for u in range(grp):
                scores(*nxt, u)

    for u in range(grp):
        scores(0, DILATIONS[0], 0, u)
    for branch, d in enumerate(DILATIONS):
        for j in range(n_groups - 1):
            stage(branch, d, j, (branch, d, j + 1))
        following = (branch + 1, DILATIONS[branch + 1], 0) if branch + 1 < len(DILATIONS) else None
        stage(branch, d, n_groups - 1, following)


def _attention(qkv, batch, seq):
    n_pairs = ATTN_HEADS * ATTN_HEAD_DIM // LANES
    res_major = (N_RES, seq // N_RES, LANES)
    blk_spec = lambda off: pl.BlockSpec((None,) + res_major, lambda b, h: (b, 0, 0, off + h))
    nb = len(DILATIONS) - 1
    qaug, kaug = _attn_consts()
    return pl.pallas_call(
        functools.partial(_attn_kernel, seq=seq),
        grid=(batch, n_pairs),
        in_specs=[blk_spec(0), blk_spec(n_pairs), blk_spec(2 * n_pairs),
                  _const_spec(qaug.shape), _const_spec(kaug.shape)],
        out_specs=pl.BlockSpec((1, seq, LANES), lambda b, h: (b, 0, h)),
        out_shape=jax.ShapeDtypeStruct((batch, seq, D_MODEL), F32),
        scratch_shapes=[pltpu.VMEM((2, ATTN_GROUP, 2 * ATTN_BLOCK, 2 * ATTN_BLOCK), F32),
                        pltpu.VMEM((nb,) + res_major, F32),
                        pltpu.VMEM((nb,) + res_major, F32),
                        pltpu.VMEM((nb,) + res_major, F32)],
        compiler_params=_params(2),
        name="dilated_attention",
    )(qkv, qkv, qkv, qaug, kaug)


def _oproj_ffn_kernel(*refs, final_norm, n_cast):
    ins, cast_in, (o_ref,), cast_out, _ = _split_refs(refs, 7, n_cast, 1)
    x_ref, a_ref, wo_ref, g_ref, win_ref, wout_ref, gfin_ref = ins
    _run_casts(cast_in, cast_out)
    half = ROW_TILE // 2
    halves = [slice(0, half), slice(half, ROW_TILE)]

    def oproj_norm(rows):
        x1 = x_ref[rows, :] + jnp.dot(a_ref[rows, :].astype(BF16), wo_ref[...], preferred_element_type=F32)
        return x1, _rmsnorm(x1, g_ref[...]).astype(BF16)

    def ffn(rows, x1, h):
        gate = jnp.dot(h, win_ref[:, 0:D_FF], preferred_element_type=F32)
        up = jnp.dot(h, win_ref[:, D_FF:2 * D_FF], preferred_element_type=F32)
        act = (gate * (1.0 / (1.0 + jnp.exp(-gate))) * up).astype(BF16)
        acc = x1 + jnp.dot(act, wout_ref[...], preferred_element_type=F32)
        if final_norm:
            acc = _rmsnorm(acc, gfin_ref[...])
        o_ref[rows, :] = acc

    heads = [oproj_norm(rows) for rows in halves]
    for rows, (x1, h) in zip(halves, heads):
        ffn(rows, x1, h)


def _oproj_ffn(x, a, wo, g, w_in, w_out, g_fin, final_norm, cast=()):
    m, d = x.shape
    steps = m // ROW_TILE
    row_spec = pl.BlockSpec((ROW_TILE, d), lambda i: (i, 0))
    c_in, c_out, c_shapes, c_ops = _cast_jobs(cast, steps, lambda i: i)
    outs = pl.pallas_call(
        functools.partial(_oproj_ffn_kernel, final_norm=final_norm, n_cast=len(cast)),
        grid=(steps,),
        in_specs=[row_spec, row_spec,
                  _const_spec(wo.shape), _const_spec((1, d)),
                  _const_spec(w_in.shape), _const_spec(w_out.shape), _const_spec((1, d))] + c_in,
        out_specs=[row_spec] + c_out,
        out_shape=[jax.ShapeDtypeStruct((m, d), F32)] + c_shapes,
        compiler_params=_params(1),
        name="oproj_ffn",
    )(x, a, wo, g, w_in, w_out, g_fin, *c_ops)
    return outs[0], outs[1:]


def _rope_tables(seq):
    half = ATTN_HEAD_DIM // 2
    freqs = ROPE_THETA ** (-np.arange(half, dtype=np.float64) / half)
    ang = np.arange(seq, dtype=np.float64)[:, None] * freqs[None, :]
    cos, sin = np.cos(ang), np.sin(ang)
    reps = LANES // ATTN_HEAD_DIM
    cos_t = np.tile(np.concatenate([cos, cos], axis=1), (1, reps))
    sin_t = np.tile(np.concatenate([-sin, sin], axis=1), (1, reps))
    res_major = lambda t: jnp.asarray(t.reshape(seq // N_RES, N_RES, LANES).transpose(1, 0, 2), F32)
    return res_major(cos_t), res_major(sin_t)


def kernel(x, mix_norm, ffn_norm, w_in_rec, conv_w, hgrn_lb, hgrn_norm, w_out_rec,
           w_qkv_attn, w_o_attn, w_ffn_in, w_ffn_out, final_norm):
    batch, seq, d = x.shape
    m = batch * seq
    xf = x.reshape(m, d)
    row = lambda v: v.reshape(1, -1)

    mix, (wo0, win0, wout0) = _proj_mixer(xf, row(mix_norm[0]), w_in_rec, 0, conv_w[0], hgrn_lb, row(hgrn_norm[0]),
                                          batch, seq, cast=[(w_out_rec, 0), (w_ffn_in, 0), (w_ffn_out, 0)])
    xf, (wqkv, wo1) = _oproj_ffn(xf, mix, wo0, row(ffn_norm[0]), win0, wout0, row(final_norm), False,
                                 cast=[(w_qkv_attn, 0), (w_o_attn, 0)])

    cos_t, sin_t = _rope_tables(seq)
    qkv, (win1, wout1) = _qkv_rope(xf, row(mix_norm[1]), wqkv, cos_t, sin_t, batch, seq,
                                   cast=[(w_ffn_in, 1), (w_ffn_out, 1)])
    att = _attention(qkv, batch, seq)
    xf, _ = _oproj_ffn(xf, att.reshape(m, d), wo1, row(ffn_norm[1]), win1, wout1, row(final_norm), True)
    return xf.reshape(batch, seq, d)
```

```python
import functools

import jax
import jax.numpy as jnp
import numpy as np
from jax import lax
from jax.experimental import pallas as pl
from jax.experimental.pallas import tpu as pltpu

D_MODEL = 1024
CONV_WIDTH = 3
CONV_CHANNELS = 512
HGRN_HEAD_DIM = 128
HGRN_HEADS = 4
HGRN_WIDTH = 512
ATTN_HEAD_DIM = 64
ATTN_HEADS = 16
ATTN_BLOCK = 128
DILATIONS = (1, 4, 16)
ROPE_THETA = 10000.0
D_FF = 2816
RMS_EPS = 1e-6
REC_IN_WIDTH = 3 * CONV_CHANNELS + 4 * HGRN_WIDTH

LANES = 128
VMEM_LIMIT_BYTES = 56 * 1024 * 1024
ROW_TILE = 512
QKV_SUBTILES = 2
MIX_TILE = 256
HGRN_CHUNK = 32
MXU_TILE = 256
PROJ_CHUNKS = tuple((lo, min(lo + 3 * MXU_TILE, REC_IN_WIDTH))
                    for lo in range(0, REC_IN_WIDTH, 3 * MXU_TILE))
MASK_VALUE = -1e30
ATTN_GROUP = 8

F32 = jnp.float32
BF16 = jnp.bfloat16


def _rmsnorm(x, g):
    return x * lax.rsqrt(jnp.mean(x * x, axis=-1, keepdims=True) + RMS_EPS) * g


def _const_spec(shape):
    zeros = (0,) * len(shape)
    return pl.BlockSpec(shape, lambda *_: zeros, pipeline_mode=pl.Buffered(1))


def _params(n_grid):
    return pltpu.CompilerParams(
        dimension_semantics=("arbitrary",) * n_grid, vmem_limit_bytes=VMEM_LIMIT_BYTES)


BF16_SUBLANES = 16


def _cast_jobs(jobs, n_steps, step_of):
    in_specs, out_specs, out_shapes, operands = [], [], [], []
    for w, layer in jobs:
        _, rows, cols = w.shape
        n = n_steps
        while rows % (n * BF16_SUBLANES):
            n //= 2
        per = n_steps // n
        in_specs.append(pl.BlockSpec((None, rows // n, cols),
                                     lambda *ids, layer=layer, per=per: (layer, step_of(*ids) // per, 0)))
        out_specs.append(pl.BlockSpec((rows // n, cols), lambda *ids, per=per: (step_of(*ids) // per, 0)))
        out_shapes.append(jax.ShapeDtypeStruct((rows, cols), BF16))
        operands.append(w)
    return in_specs, out_specs, out_shapes, operands


def _split_refs(refs, n_in, n_cast, n_out):
    bounds = np.cumsum([0, n_in, n_cast, n_out, n_cast])
    return [refs[a:b] for a, b in zip(bounds[:-1], bounds[1:])] + [refs[bounds[-1]:]]


def _run_casts(cast_in, cast_out):
    for src, dst in zip(cast_in, cast_out):
        dst[...] = src[...].astype(BF16)


def _mixer_consts():
    ts, c = MIX_TILE, HGRN_CHUNK
    t = np.arange(ts)
    same = (t[:, None] // c) == (t[None, :] // c)
    tril = same & (t[:, None] >= t[None, :])
    cum = np.concatenate([tril, same]).astype(np.float32)
    ind = (t[:, None] // c == np.arange(LANES)[None, :]).astype(np.float32)
    return jnp.asarray(cum, BF16), jnp.asarray(ind, BF16)


def _proj_mixer_kernel(*refs, n_cast, tiles_per_row):
    ins, cast_in, (out_ref,), cast_out, scratch = _split_refs(refs, 10, n_cast, 1)
    x_first, x_odd, x_even, g_ref, w_ref, convw_ref, lb_ref, hnorm_ref, cum_ref, ind_ref = ins
    w_bf, proj_a, proj_b, ubuf, state, vblk = scratch
    step = pl.program_id(0)
    ts = MIX_TILE

    def projection(x_ref, dst):
        h = _rmsnorm(x_ref[...], g_ref[...]).astype(BF16)

        def job(lo, hi):
            def run():
                dst[:, lo:hi] = jnp.dot(h, w_bf[:, lo:hi], preferred_element_type=F32)
            return run

        return [job(lo, hi) for lo, hi in PROJ_CHUNKS]

    def project(x_ref, dst):
        for run in projection(x_ref, dst):
            run()

    @pl.when(step == 0)
    def _():
        w_bf[...] = w_ref[...].astype(BF16)
        project(x_first, proj_a)

    @pl.when(step % (tiles_per_row // 2) == 0)
    def _():
        state[...] = jnp.zeros_like(state)
        vblk[...] = jnp.zeros_like(vblk)
        ubuf[0:8, :] = jnp.zeros((8, CONV_CHANNELS), F32)

    mixer_refs = (convw_ref, lb_ref, hnorm_ref, cum_ref, ind_ref, out_ref)
    _mixer_tile(proj_a, *mixer_refs, slice(0, ts), ubuf, state, vblk, projection(x_odd, proj_b))
    _mixer_tile(proj_b, *mixer_refs, slice(ts, 2 * ts), ubuf, state, vblk, projection(x_even, proj_a))
    _run_casts(cast_in, cast_out)


def _mixer_tile(proj_ref, convw_ref, lb_ref, hnorm_ref, cum_ref, ind_ref, out_ref, rows, ubuf, state, vblk, fillers):
    ts = MIX_TILE
    fillers = iter(fillers)

    def fill():
        job = next(fillers, None)
        if job is not None:
            job()

    cc = CONV_CHANNELS
    c = HGRN_CHUNK
    n_chunks = ts // c

    u = proj_ref[:, cc:2 * cc] * proj_ref[:, 2 * cc:3 * cc]
    ubuf[8:8 + ts, :] = u
    y = (convw_ref[0:1, :] * ubuf[6:6 + ts, :]
         + convw_ref[1:2, :] * ubuf[7:7 + ts, :]
         + convw_ref[2:3, :] * u)
    out_ref[rows, 0:cc] = (proj_ref[:, 0:cc] * y).astype(out_ref.dtype)
    ubuf[0:8, :] = u[ts - 8:ts, :]
    fill()

    lbp = lb_ref[...]
    e = jnp.exp(lbp - jnp.max(lbp, axis=0, keepdims=True))
    sm = e / jnp.sum(e, axis=0, keepdims=True)
    cum0 = sm[0:1, :]
    cum1 = cum0 + sm[1:2, :]
    lb = cum1 - cum0
    q0 = 3 * cc
    z = proj_ref[:, q0 + HGRN_WIDTH:q0 + 2 * HGRN_WIDTH]
    ez = jnp.exp(-jnp.abs(z))
    inv = 1.0 / (1.0 + ez)
    small = ez * inv
    pos = z >= 0.0
    lf = jnp.log(lb + (1.0 - lb) * jnp.where(pos, inv, small))
    kk = (1.0 - lb) * jnp.where(pos, small, inv)

    lf_hi = lf.astype(BF16)
    lf_lo = (lf - lf_hi.astype(F32)).astype(BF16)
    sums = (jnp.dot(cum_ref[...], lf_hi, preferred_element_type=F32)
            + jnp.dot(cum_ref[...], lf_lo, preferred_element_type=F32))
    g = sums[:ts]
    gl = sums[ts:]
    tn = (((0,), (0,)), ((), ()))
    decay_col = jnp.exp(lax.dot_general(lf_hi, ind_ref[...], tn, preferred_element_type=F32)
                        + lax.dot_general(lf_lo, ind_ref[...], tn, preferred_element_type=F32))

    def head_operands(hd):
        lanes = slice(hd * LANES, (hd + 1) * LANES)
        gh_, glh, kh = g[:, lanes], gl[:, lanes], kk[:, lanes]
        half = 0.5 * glh
        q = proj_ref[:, q0 + hd * LANES:q0 + (hd + 1) * LANES]
        i_in = proj_ref[:, q0 + 2 * HGRN_WIDTH + hd * LANES:q0 + 2 * HGRN_WIDTH + (hd + 1) * LANES]
        return ((q * jnp.exp(gh_ - half)).astype(BF16),
                (kh * jnp.exp(half - gh_)).astype(BF16),
                (q * jnp.exp(gh_)).astype(BF16),
                (kh * jnp.exp(glh - gh_)).astype(BF16),
                (i_in * (1.0 / (1.0 + jnp.exp(-i_in)))).astype(BF16))

    row = lax.broadcasted_iota(jnp.int32, (ts, ts), 0)
    col = lax.broadcasted_iota(jnp.int32, (ts, ts), 1)
    intra = (row // c == col // c) & (row >= col)
    hnorm = hnorm_ref[...]

    intra_out, updates, qgs = [], [], []
    for hd in range(HGRN_HEADS):
        lanes = slice(hd * LANES, (hd + 1) * LANES)
        qd, kd, qg, kd2, vh = head_operands(hd)
        qgs.append(qg)
        a = lax.dot_general(qd, kd, (((1,), (1,)), ((), ())), preferred_element_type=F32)
        intra_out.append(jnp.dot(jnp.where(intra, a, 0.0).astype(BF16), vh, preferred_element_type=F32))
        for ci in range(n_chunks):
            vblk[hd, ci * c:(ci + 1) * c, ci * LANES:(ci + 1) * LANES] = vh[ci * c:(ci + 1) * c]
        updates.append(lax.dot_general(kd2, vblk[hd], tn, preferred_element_type=F32))

    for hd in range(HGRN_HEADS):
        lanes = slice(hd * LANES, (hd + 1) * LANES)
        o, upd = intra_out[hd], updates[hd]
        st = state[hd]
        inter = []
        for ci in range(n_chunks):
            inter.append(jnp.dot(qgs[hd][ci * c:(ci + 1) * c], st.astype(BF16), preferred_element_type=F32))
            st = st * decay_col[lanes, ci:ci + 1] + upd[:, ci * LANES:(ci + 1) * LANES]
        state[hd] = st
        o = o + jnp.concatenate(inter, axis=0)
        o = _rmsnorm(o, hnorm)
        gh = proj_ref[:, q0 + 3 * HGRN_WIDTH + hd * LANES:q0 + 3 * HGRN_WIDTH + (hd + 1) * LANES]
        o = o * (gh * (1.0 / (1.0 + jnp.exp(-gh))))
        out_ref[rows, cc + hd * LANES:cc + (hd + 1) * LANES] = o.astype(out_ref.dtype)
        fill()
    for job in fillers:
        job()


def _proj_mixer(x, g, w, layer, conv_w, hgrn_lb, hgrn_norm, batch, seq, cast):
    ts = MIX_TILE
    d = x.shape[1]
    n = w.shape[2]
    tiles_per_row = seq // ts
    n_tiles = batch * tiles_per_row
    steps = n_tiles // 2
    assert tiles_per_row % 2 == 0
    cum, ind = _mixer_consts()
    c_in, c_out, c_shapes, c_ops = _cast_jobs(cast, steps, lambda i: i)
    tile_spec = lambda imap, **kw: pl.BlockSpec((ts, d), imap, **kw)
    outs = pl.pallas_call(
        functools.partial(_proj_mixer_kernel, n_cast=len(cast), tiles_per_row=tiles_per_row),
        grid=(steps,),
        in_specs=[tile_spec(lambda i: (0, 0), pipeline_mode=pl.Buffered(1)),
                  tile_spec(lambda i: (2 * i + 1, 0)),
                  tile_spec(lambda i: (jnp.minimum(2 * i + 2, n_tiles - 1), 0)),
                  _const_spec((1, d)),
                  pl.BlockSpec((None, d, n), lambda i: (layer, 0, 0), pipeline_mode=pl.Buffered(1)),
                  _const_spec((CONV_WIDTH, CONV_CHANNELS)),
                  _const_spec((3, HGRN_WIDTH)),
                  _const_spec((1, HGRN_HEAD_DIM)),
                  _const_spec(cum.shape), _const_spec(ind.shape)] + c_in,
        out_specs=[pl.BlockSpec((2 * ts, D_MODEL), lambda i: (i, 0))] + c_out,
        out_shape=[jax.ShapeDtypeStruct((batch * seq, D_MODEL), BF16)] + c_shapes,
        scratch_shapes=[pltpu.VMEM((d, n), BF16),
                        pltpu.VMEM((ts, n), F32),
                        pltpu.VMEM((ts, n), F32),
                        pltpu.VMEM((ts + 8, CONV_CHANNELS), F32),
                        pltpu.VMEM((HGRN_HEADS, HGRN_HEAD_DIM, HGRN_HEAD_DIM), F32),
                        pltpu.VMEM((HGRN_HEADS, ts, (ts // HGRN_CHUNK) * LANES), BF16)],
        compiler_params=_params(1),
        name="proj_conv_hgrn_mixer",
    )(x, x, x, g, w, conv_w, hgrn_lb, hgrn_norm, cum, ind, *c_ops)
    return outs[0], outs[1:]


N_RES = DILATIONS[-1]
Q_SCALE = ATTN_HEAD_DIM ** -0.5 * float(np.log2(np.e))


def _segments(d):
    count = N_RES // d
    return count, ATTN_BLOCK // count


def _attn_consts():
    blk = ATTN_BLOCK
    e = np.arange(blk)
    none = np.full((blk, blk), MASK_VALUE)
    kaug = []
    for d in DILATIONS:
        count, rows = _segments(d)
        pos = (e % rows) * count + e // rows
        kpos, qpos = pos[:, None], pos[None, :]
        prev_ok = np.where(kpos >= qpos, 0.0, MASK_VALUE)
        cur_ok = np.where(kpos <= qpos, 0.0, MASK_VALUE)
        kaug.append(np.stack([np.concatenate([none, cur_ok]), np.concatenate([prev_ok, cur_ok])]))
    qaug = np.concatenate([np.eye(blk), np.eye(blk)])
    return jnp.asarray(qaug, BF16), jnp.asarray(np.stack(kaug), BF16)


def _qkv_rope_kernel(*refs, n_cast):
    n_slabs = D_MODEL // LANES
    ins, cast_in, (o_ref,), cast_out, _ = _split_refs(refs, n_slabs + 4, n_cast, 1)
    x_refs = ins[:n_slabs]
    g_ref, w_ref, cos_ref, sin_ref = ins[n_slabs:]
    _run_casts(cast_in, cast_out)
    rows = ROW_TILE // N_RES
    lane = lax.broadcasted_iota(jnp.int32, (ROW_TILE, LANES), 1)
    first_half = (lane % ATTN_HEAD_DIM) < (ATTN_HEAD_DIM // 2)
    width = ATTN_HEADS * ATTN_HEAD_DIM
    for sub in range(QKV_SUBTILES):
        x = jnp.concatenate(
            [jnp.concatenate([ref[pl.ds(sub * ROW_TILE + r, rows, stride=N_RES), :] for r in range(N_RES)], axis=0)
             for ref in x_refs], axis=1)
        h = _rmsnorm(x, g_ref[...]).astype(BF16)
        out_rows = slice(sub * rows, (sub + 1) * rows)
        cos = cos_ref[:, out_rows, :].reshape(ROW_TILE, LANES)
        sin = sin_ref[:, out_rows, :].reshape(ROW_TILE, LANES)
        for part in range(3):
            y = jnp.dot(h, w_ref[:, part * width:(part + 1) * width], preferred_element_type=F32)
            if part < 2:
                cols = []
                for cb in range(width // LANES):
                    t = y[:, cb * LANES:(cb + 1) * LANES]
                    partner = jnp.where(first_half,
                                        pltpu.roll(t, LANES - ATTN_HEAD_DIM // 2, 1),
                                        pltpu.roll(t, ATTN_HEAD_DIM // 2, 1))
                    t = t * cos + partner * sin
                    cols.append(t * Q_SCALE if part == 0 else t)
                y = jnp.concatenate(cols, axis=1)
            o_ref[:, out_rows, part * width:(part + 1) * width] = y.reshape(N_RES, rows, width)


def _qkv_rope(x, g, w, cos_t, sin_t, batch, seq, cast):
    d = x.shape[-1]
    n = w.shape[1]
    step_rows = QKV_SUBTILES * ROW_TILE
    rows = step_rows // N_RES
    tiles = seq // step_rows
    n_slabs = d // LANES
    x_specs = [pl.BlockSpec((step_rows, LANES), functools.partial(lambda b, c, s: (b * tiles + c, s), s=s))
               for s in range(n_slabs)]
    table_spec = pl.BlockSpec((N_RES, rows, LANES), lambda b, c: (0, c, 0))
    c_in, c_out, c_shapes, c_ops = _cast_jobs(cast, batch * tiles, lambda b, c: b * tiles + c)
    outs = pl.pallas_call(
        functools.partial(_qkv_rope_kernel, n_cast=len(cast)),
        grid=(batch, tiles),
        in_specs=x_specs + [_const_spec((1, d)), _const_spec((d, n)), table_spec, table_spec] + c_in,
        out_specs=[pl.BlockSpec((None, N_RES, rows, n), lambda b, c: (b, 0, c, 0))] + c_out,
        out_shape=[jax.ShapeDtypeStruct((batch, N_RES, seq // N_RES, n), F32)] + c_shapes,
        compiler_params=_params(2),
        name="qkv_rope",
    )(*([x] * n_slabs), g, w, cos_t, sin_t, *c_ops)
    return outs[0], outs[1:]


def _attn_kernel(q_ref, k_ref, v_ref, qaug_ref, kaug_ref, o_ref,
                 s_buf, m_s, l_s, acc_s, *, seq):
    blk = ATTN_BLOCK
    grp = ATTN_GROUP
    n_groups = (seq // blk) // grp
    assert seq // N_RES == blk
    last = len(DILATIONS) - 1
    lane = lax.broadcasted_iota(jnp.int32, (blk, LANES), 1)
    first_head = lane < ATTN_HEAD_DIM

    def pick(top, bot):
        return jnp.where(first_head, top, bot)

    def block_id(d, j, u):
        n_blocks = seq // (d * blk)
        res, n = divmod(j * grp + u, n_blocks)
        if n_blocks == 1:
            return res, n, None, None
        return res, n, max(n - 1, 0), min(n, 1)

    def seg_rows(d, n):
        _, rows = _segments(d)
        return pl.ds(n * rows, rows)

    def load_block(ref, d, res, n):
        count, _ = _segments(d)
        return jnp.concatenate([ref[res + d * i, seg_rows(d, n), :] for i in range(count)], axis=0)

    def store_block(ref, branch, d, res, n, val):
        count, rows = _segments(d)
        for i in range(count):
            ref[branch, res + d * i, seg_rows(d, n), :] = val[i * rows:(i + 1) * rows]

    def scores(branch, d, j, u):
        res, n, n_prev, has_prev = block_id(d, j, u)
        qb = load_block(q_ref, d, res, n)
        q2 = jnp.concatenate([jnp.where(first_head, qb, 0.0), jnp.where(first_head, 0.0, qb)], axis=0).astype(BF16)
        q_ext = jnp.concatenate([q2, qaug_ref[...]], axis=1)
        kc = load_block(k_ref, d, res, n)
        if n_prev is None:
            kk = kc.astype(BF16)
            aug = kaug_ref[branch, 1, blk:2 * blk, :]
        else:
            kk = jnp.concatenate([load_block(k_ref, d, res, n_prev), kc], axis=0).astype(BF16)
            aug = kaug_ref[branch, has_prev]
        k_ext = jnp.concatenate([kk, aug], axis=1)
        s = lax.dot_general(q_ext, k_ext, (((1,), (1,)), ((), ())), preferred_element_type=F32)
        s_buf[(branch * n_groups + j) % 2, u, :, 0:s.shape[1]] = s

    def softmax(branch, d, j, u):
        _, _, n_prev, _ = block_id(d, j, u)
        keys = blk if n_prev is None else 2 * blk
        s = s_buf[(branch * n_groups + j) % 2, u, :, 0:keys]
        m = jnp.max(s, axis=-1, keepdims=True)
        return jnp.exp2((s - m).astype(BF16)), m

    def pv(branch, d, j, u, p, m):
        res, n, n_prev, _ = block_id(d, j, u)
        vc = load_block(v_ref, d, res, n)
        if n_prev is None:
            vv = vc.astype(BF16)
        else:
            vv = jnp.concatenate([load_block(v_ref, d, res, n_prev), vc], axis=0).astype(BF16)
        key_first_head = lax.broadcasted_iota(jnp.int32, vv.shape, 1) < ATTN_HEAD_DIM
        one = jnp.ones_like(vv)
        out0 = jnp.dot(p[:blk], jnp.where(key_first_head, vv, one), preferred_element_type=F32)
        out1 = jnp.dot(p[blk:], jnp.where(key_first_head, one, vv), preferred_element_type=F32)
        mb = jnp.broadcast_to(m, (2 * blk, LANES))
        m_b = pick(mb[:blk], mb[blk:])
        acc_b = pick(out0, out1)
        l_b = pltpu.roll(pick(out1, out0), ATTN_HEAD_DIM, 1)
        if branch < last:
            store_block(m_s, branch, d, res, n, m_b)
            store_block(l_s, branch, d, res, n, l_b)
            store_block(acc_s, branch, d, res, n, acc_b)
        else:
            ms = [m_s[i, res] for i in range(last)] + [m_b]
            ls = [l_s[i, res] for i in range(last)] + [l_b]
            accs = [acc_s[i, res] for i in range(last)] + [acc_b]
            m_all = functools.reduce(jnp.maximum, ms)
            ws = [jnp.exp2(mi - m_all) for mi in ms]
            num = sum(w * a for w, a in zip(ws, accs))
            den = sum(w * l for w, l in zip(ws, ls))
            o_ref[0, pl.ds(res, blk, stride=N_RES), :] = num / den

    def stage(branch, d, j, nxt):
        for u in range(grp):
            pv(branch, d, j, u, *softmax(branch, d, j, u))
        if nxt is not None:
            for u in range(grp):
                scores(*nxt, u)

    for u in range(grp):
        scores(0, DILATIONS[0], 0, u)
    for branch, d in enumerate(DILATIONS):
        for j in range(n_groups - 1):
            stage(branch, d, j, (branch, d, j + 1))
        following = (branch + 1, DILATIONS[branch + 1], 0) if branch + 1 < len(DILATIONS) else None
        stage(branch, d, n_groups - 1, following)


def _attention(qkv, batch, seq):
    n_pairs = ATTN_HEADS * ATTN_HEAD_DIM // LANES
    res_major = (N_RES, seq // N_RES, LANES)
    blk_spec = lambda off: pl.BlockSpec((None,) + res_major, lambda b, h: (b, 0, 0, off + h))
    nb = len(DILATIONS) - 1
    qaug, kaug = _attn_consts()
    return pl.pallas_call(
        functools.partial(_attn_kernel, seq=seq),
        grid=(batch, n_pairs),
        in_specs=[blk_spec(0), blk_spec(n_pairs), blk_spec(2 * n_pairs),
                  _const_spec(qaug.shape), _const_spec(kaug.shape)],
        out_specs=pl.BlockSpec((1, seq, LANES), lambda b, h: (b, 0, h)),
        out_shape=jax.ShapeDtypeStruct((batch, seq, D_MODEL), F32),
        scratch_shapes=[pltpu.VMEM((2, ATTN_GROUP, 2 * ATTN_BLOCK, 2 * ATTN_BLOCK), F32),
                        pltpu.VMEM((nb,) + res_major, F32),
                        pltpu.VMEM((nb,) + res_major, F32),
                        pltpu.VMEM((nb,) + res_major, F32)],
        compiler_params=_params(2),
        name="dilated_attention",
    )(qkv, qkv, qkv, qaug, kaug)


def _oproj_ffn_kernel(*refs, final_norm, n_cast):
    ins, cast_in, (o_ref,), cast_out, _ = _split_refs(refs, 7, n_cast, 1)
    x_ref, a_ref, wo_ref, g_ref, win_ref, wout_ref, gfin_ref = ins
    _run_casts(cast_in, cast_out)
    half = ROW_TILE // 2
    halves = [slice(0, half), slice(half, ROW_TILE)]

    def oproj_norm(rows):
        x1 = x_ref[rows, :] + jnp.dot(a_ref[rows, :].astype(BF16), wo_ref[...], preferred_element_type=F32)
        return x1, _rmsnorm(x1, g_ref[...]).astype(BF16)

    def ffn(rows, x1, h):
        gate = jnp.dot(h, win_ref[:, 0:D_FF], preferred_element_type=F32)
        up = jnp.dot(h, win_ref[:, D_FF:2 * D_FF], preferred_element_type=F32)
        act = (gate * (1.0 / (1.0 + jnp.exp(-gate))) * up).astype(BF16)
        acc = x1 + jnp.dot(act, wout_ref[...], preferred_element_type=F32)
        if final_norm:
            acc = _rmsnorm(acc, gfin_ref[...])
        o_ref[rows, :] = acc

    heads = [oproj_norm(rows) for rows in halves]
    for rows, (x1, h) in zip(halves, heads):
        ffn(rows, x1, h)


def _oproj_ffn(x, a, wo, g, w_in, w_out, g_fin, final_norm, cast=()):
    m, d = x.shape
    steps = m // ROW_TILE
    row_spec = pl.BlockSpec((ROW_TILE, d), lambda i: (i, 0))
    c_in, c_out, c_shapes, c_ops = _cast_jobs(cast, steps, lambda i: i)
    outs = pl.pallas_call(
        functools.partial(_oproj_ffn_kernel, final_norm=final_norm, n_cast=len(cast)),
        grid=(steps,),
        in_specs=[row_spec, row_spec,
                  _const_spec(wo.shape), _const_spec((1, d)),
                  _const_spec(w_in.shape), _const_spec(w_out.shape), _const_spec((1, d))] + c_in,
        out_specs=[row_spec] + c_out,
        out_shape=[jax.ShapeDtypeStruct((m, d), F32)] + c_shapes,
        compiler_params=_params(1),
        name="oproj_ffn",
    )(x, a, wo, g, w_in, w_out, g_fin, *c_ops)
    return outs[0], outs[1:]


def _rope_tables(seq):
    half = ATTN_HEAD_DIM // 2
    freqs = ROPE_THETA ** (-np.arange(half, dtype=np.float64) / half)
    ang = np.arange(seq, dtype=np.float64)[:, None] * freqs[None, :]
    cos, sin = np.cos(ang), np.sin(ang)
    reps = LANES // ATTN_HEAD_DIM
    cos_t = np.tile(np.concatenate([cos, cos], axis=1), (1, reps))
    sin_t = np.tile(np.concatenate([-sin, sin], axis=1), (1, reps))
    res_major = lambda t: jnp.asarray(t.reshape(seq // N_RES, N_RES, LANES).transpose(1, 0, 2), F32)
    return res_major(cos_t), res_major(sin_t)


def kernel(x, mix_norm, ffn_norm, w_in_rec, conv_w, hgrn_lb, hgrn_norm, w_out_rec,
           w_qkv_attn, w_o_attn, w_ffn_in, w_ffn_out, final_norm):
    batch, seq, d = x.shape
    m = batch * seq
    xf = x.reshape(m, d)
    row = lambda v: v.reshape(1, -1)

    mix, (wo0, win0, wout0) = _proj_mixer(xf, row(mix_norm[0]), w_in_rec, 0, conv_w[0], hgrn_lb, row(hgrn_norm[0]),
                                          batch, seq, cast=[(w_out_rec, 0), (w_ffn_in, 0), (w_ffn_out, 0)])
    xf, (wqkv, wo1) = _oproj_ffn(xf, mix, wo0, row(ffn_norm[0]), win0, wout0, row(final_norm), False,
                                 cast=[(w_qkv_attn, 0), (w_o_attn, 0)])

    cos_t, sin_t = _rope_tables(seq)
    qkv, (win1, wout1) = _qkv_rope(xf, row(mix_norm[1]), wqkv, cos_t, sin_t, batch, seq,
                                   cast=[(w_ffn_in, 1), (w_ffn_out, 1)])
    att = _attention(qkv, batch, seq)
    xf, _ = _oproj_ffn(xf, att.reshape(m, d), wo1, row(ffn_norm[1]), win1, wout1, row(final_norm), True)
    return xf.reshape(batch, seq, d)
```
